```python
import math
import jax
import jax.numpy as jnp
from jax import lax
import numpy as np

D_MODEL = 1024
BATCH = 4
SEQ = 8192
DEPTH = 2

EPS = 1e-6
RET_HEADS = 4
RET_DK = D_MODEL // 8
RET_DV = D_MODEL // 4
RET_QK = RET_HEADS * RET_DK
RET_VW = RET_HEADS * RET_DV
RET_CHUNK = 128
ROPE_BASE = 10000.0
HG_EXPAND = 128
HG_DIM = D_MODEL
HG_HEADS = HG_DIM // HG_EXPAND
HG_CHUNK = 64
SSD_INNER = 2 * D_MODEL
SSD_HEADDIM = 64
SSD_HEADS = SSD_INNER // SSD_HEADDIM
SSD_GROUPS = 4
SSD_STATE = 128
SSD_CONV = 4
SSD_CONV_DIM = SSD_INNER + 2 * SSD_GROUPS * SSD_STATE
SSD_CHUNK = 128
N_BRANCH = 3
IN_SIZES = (RET_QK, RET_QK, RET_VW, RET_VW,
            HG_DIM, HG_DIM, HG_DIM, HG_DIM,
            SSD_INNER, SSD_CONV_DIM, SSD_HEADS,
            N_BRANCH * D_MODEL)
IN_DIM = sum(IN_SIZES)
MOE_GROUPS = 4
MOE_PER_GROUP = 8
MOE_EXPERTS = MOE_GROUPS * MOE_PER_GROUP
MOE_TOPK = 2
MOE_FF = D_MODEL // 2
MOE_BLOCK = 256

kernel_name = 'hybrid_ret_hgrn2_ssd_hmoe'


def split_last(a, sizes):
    out, start = [], 0
    for n in sizes:
        out.append(a[..., start:start + n])
        start += n
    return out


def rms_norm(x, w):
    xf = x.astype(jnp.float32)
    y = xf * lax.rsqrt(jnp.mean(xf * xf, axis=-1, keepdims=True) + EPS)
    return y * w.astype(jnp.float32)


def to_chunks(a, chunk):
    b, s = a.shape[:2]
    a = a.reshape((b, s // chunk, chunk) + a.shape[2:])
    return jnp.moveaxis(a, 1, 0)


def from_chunks(a):
    a = jnp.moveaxis(a, 0, 1)
    return a.reshape((a.shape[0], a.shape[1] * a.shape[2]) + a.shape[3:])


def masked_exp(mask, logits):
    return jnp.where(mask, jnp.exp(jnp.where(mask, logits, 0.0)), 0.0)


def rotary(x, pos):
    half = x.shape[-1] // 2
    inv_freq = ROPE_BASE ** (-jnp.linspace(0.0, 1.0, half, dtype=jnp.float32))
    ang = pos.astype(jnp.float32)[..., None] * inv_freq
    cos = jnp.cos(ang)[:, :, None, :]
    sin = jnp.sin(ang)[:, :, None, :]
    x1, x2 = x[..., :half], x[..., half:]
    return jnp.concatenate([x1 * cos - x2 * sin, x1 * sin + x2 * cos], axis=-1)


def retention(q, k, v, g, positions):
    b, s = q.shape[:2]
    q = rotary(q.reshape(b, s, RET_HEADS, RET_DK), positions)
    k = rotary(k.reshape(b, s, RET_HEADS, RET_DK), positions) * (RET_DK ** -0.5)
    v = v.reshape(b, s, RET_HEADS, RET_DV)
    log_gamma = jnp.log(1.0 - 2.0 ** (-5.0 - jnp.arange(RET_HEADS, dtype=jnp.float32)))
    idx = jnp.arange(RET_CHUNK, dtype=jnp.float32)
    rel = idx[:, None] - idx[None, :]
    decay_intra = jnp.where(rel >= 0.0,
                            jnp.exp(log_gamma[:, None, None] * jnp.maximum(rel, 0.0)), 0.0)
    decay_q = jnp.exp(log_gamma[None, :] * (idx[:, None] + 1.0))[None, :, :, None]
    decay_k = jnp.exp(log_gamma[None, :] * (RET_CHUNK - 1.0 - idx[:, None]))[None, :, :, None]
    decay_chunk = jnp.exp(log_gamma * RET_CHUNK)[None, :, None, None]

    def step(state, inp):
        qc, kc, vc = inp
        scores = jnp.einsum('bthk,bshk->bhts', qc, kc) * decay_intra
        o = jnp.einsum('bhts,bshv->bthv', scores, vc)
        o = o + jnp.einsum('bthk,bhkv->bthv', qc, state) * decay_q
        new = state * decay_chunk + jnp.einsum('bshk,bshv->bhkv', kc * decay_k, vc)
        return new, o

    init = jnp.zeros((b, RET_HEADS, RET_DK, RET_DV), jnp.float32)
    _, o = lax.scan(step, init, (to_chunks(q, RET_CHUNK), to_chunks(k, RET_CHUNK), to_chunks(v, RET_CHUNK)))
    o = from_chunks(o)
    mu = jnp.mean(o, axis=-1, keepdims=True)
    var = jnp.mean(jnp.square(o - mu), axis=-1, keepdims=True)
    o = (o - mu) * lax.rsqrt(var + EPS)
    return jax.nn.silu(g) * o.reshape(b, s, RET_VW)


def hgrn2(q, f_logit, i, g, lb, norm_w):
    b, s = q.shape[:2]
    shp = (b, s, HG_HEADS, HG_EXPAND)
    q = jax.nn.silu(q).reshape(shp)
    f_logit = f_logit.reshape(shp)
    lb = lb.reshape(HG_HEADS, HG_EXPAND)
    f = lb + (1.0 - lb) * jax.nn.sigmoid(f_logit)
    log_f = jnp.log(f)
    k = (1.0 - lb) * jax.nn.sigmoid(-f_logit)
    v = i.reshape(shp)
    causal = jnp.tril(jnp.ones((HG_CHUNK, HG_CHUNK), bool))[None, :, :, None, None]

    def step(state, inp):
        qc, kc, vc, lfc = inp
        G = jnp.cumsum(lfc, axis=1)
        diff = G[:, :, None] - G[:, None, :]
        decay = masked_exp(causal, diff)
        attn = jnp.sum(qc[:, :, None] * kc[:, None, :] * decay, axis=-1)
        o = jnp.einsum('btsh,bshv->bthv', attn, vc)
        o = o + jnp.einsum('bthk,bhkv->bthv', qc * jnp.exp(G), state)
        g_last = G[:, -1]
        new = state * jnp.exp(g_last)[..., None] + jnp.einsum(
            'bshk,bshv->bhkv', kc * jnp.exp(g_last[:, None] - G), vc)
        return new, o

    init = jnp.zeros((b, HG_HEADS, HG_EXPAND, HG_EXPAND), jnp.float32)
    _, o = lax.scan(step, init, (to_chunks(q, HG_CHUNK), to_chunks(k, HG_CHUNK),
                                 to_chunks(v, HG_CHUNK), to_chunks(log_f, HG_CHUNK)))
    o = from_chunks(o)
    o = rms_norm(o, norm_w) * jax.nn.silu(g.reshape(shp))
    return o.reshape(b, s, HG_DIM)


def causal_depthwise_conv(x, w):
    ch = x.shape[-1]
    return lax.conv_general_dilated(
        x, w[:, None, :].astype(x.dtype), window_strides=(1,),
        padding=[(SSD_CONV - 1, 0)], dimension_numbers=('NWC', 'WIO', 'NWC'),
        feature_group_count=ch)


def ssd_mixer(z, xbc, dt_raw, conv_w, conv_b, dt_bias, a_log, d_skip, norm_w):
    b, s = z.shape[:2]
    hpg = SSD_HEADS // SSD_GROUPS
    xbc = jax.nn.silu(causal_depthwise_conv(xbc, conv_w) + conv_b.astype(jnp.float32))
    xs, bm, cm = split_last(xbc, (SSD_INNER, SSD_GROUPS * SSD_STATE, SSD_GROUPS * SSD_STATE))
    xs = xs.reshape(b, s, SSD_HEADS, SSD_HEADDIM)
    bm = bm.reshape(b, s, SSD_GROUPS, SSD_STATE)
    cm = cm.reshape(b, s, SSD_GROUPS, SSD_STATE)
    dt = jax.nn.softplus(dt_raw + dt_bias.astype(jnp.float32))
    a = -jnp.exp(a_log.astype(jnp.float32))
    da = dt * a
    xdt = xs * dt[..., None]
    causal = jnp.tril(jnp.ones((SSD_CHUNK, SSD_CHUNK), bool))[None, :, :, None]

    def step(state, inp):
        xc, bc, cc, dac = inp
        bb, cl = xc.shape[:2]
        cum = jnp.cumsum(dac, axis=1)
        diff = cum[:, :, None, :] - cum[:, None, :, :]
        lmat = masked_exp(causal, diff)
        cb = jnp.repeat(jnp.einsum('btgn,bsgn->btsg', cc, bc), hpg, axis=-1)
        y = jnp.einsum('btsh,bshp->bthp', cb * lmat, xc)
        state_g = state.reshape(bb, SSD_GROUPS, hpg, SSD_HEADDIM, SSD_STATE)
        y_off = jnp.einsum('btgn,bgjpn->btgjp', cc, state_g).reshape(bb, cl, SSD_HEADS, SSD_HEADDIM)
        y = y + y_off * jnp.exp(cum)[..., None]
        decay_s = jnp.exp(cum[:, -1:] - cum)
        xw = (xc * decay_s[..., None]).reshape(bb, cl, SSD_GROUPS, hpg, SSD_HEADDIM)
        upd = jnp.einsum('bsgn,bsgjp->bgjpn', bc, xw).reshape(bb, SSD_HEADS, SSD_HEADDIM, SSD_STATE)
        new = state * jnp.exp(cum[:, -1])[:, :, None, None] + upd
        return new, y

    init = jnp.zeros((b, SSD_HEADS, SSD_HEADDIM, SSD_STATE), jnp.float32)
    _, y = lax.scan(step, init, (to_chunks(xdt, SSD_CHUNK), to_chunks(bm, SSD_CHUNK),
                                 to_chunks(cm, SSD_CHUNK), to_chunks(da, SSD_CHUNK)))
    y = from_chunks(y) + d_skip.astype(jnp.float32)[:, None] * xs
    yz = (y.reshape(b, s, SSD_INNER) * jax.nn.silu(z)).reshape(b, s, SSD_GROUPS, SSD_INNER // SSD_GROUPS)
    yz = rms_norm(yz, norm_w.reshape(SSD_GROUPS, SSD_INNER // SSD_GROUPS))
    return yz.reshape(b, s, SSD_INNER)


def hybrid_mixer(h, positions, lb, w_in, conv_w, conv_b, dt_bias, a_log, d_skip,
                 ssd_norm_w, hg_norm_w, w_ret_o, w_hg_o, w_ssd_o, w_out):
    u = h @ w_in
    rq, rk, rv, rg, hq, hf, hi, hg, sz, sxbc, sdt, gl = split_last(u, IN_SIZES)
    y_ret = retention(rq, rk, rv, rg, positions) @ w_ret_o
    y_hg = hgrn2(hq, hf, hi, hg, lb, hg_norm_w) @ w_hg_o
    y_ssd = ssd_mixer(sz, sxbc, sdt, conv_w, conv_b, dt_bias, a_log, d_skip, ssd_norm_w) @ w_ssd_o
    g_ret, g_hg, g_ssd = split_last(jax.nn.sigmoid(gl), (D_MODEL, D_MODEL, D_MODEL))
    return (g_ret * y_ret + g_hg * y_hg + g_ssd * y_ssd) @ w_out


def hier_moe(h, w_group, b_group, w_expert, b_expert, w1, w3, w2):
    t, d = h.shape
    g_logits = h @ w_group + b_group.astype(jnp.float32)
    g_prob = jax.nn.softmax(g_logits, axis=-1)
    g_top = jnp.argmax(g_logits, axis=-1)
    p_g = jnp.take_along_axis(g_prob, g_top[:, None], axis=1)
    e_logits = (h @ w_expert + b_expert.astype(jnp.float32)).reshape(t, MOE_GROUPS, MOE_PER_GROUP)
    e_logits = jnp.take_along_axis(e_logits, g_top[:, None, None], axis=1)[:, 0]
    top_l, top_i = lax.top_k(e_logits, MOE_TOPK)
    weights = jax.nn.softmax(top_l, axis=-1) * p_g
    expert = g_top[:, None].astype(jnp.int32) * MOE_PER_GROUP + top_i.astype(jnp.int32)
    n_assign = t * MOE_TOPK
    flat_e = expert.reshape(n_assign)
    flat_w = weights.reshape(n_assign)
    flat_tok = jnp.repeat(jnp.arange(t, dtype=jnp.int32), MOE_TOPK)
    order = jnp.argsort(flat_e)
    se, st, sw = flat_e[order], flat_tok[order], flat_w[order]
    counts = jnp.zeros((MOE_EXPERTS,), jnp.int32).at[flat_e].add(1)
    starts = jnp.cumsum(counts) - counts
    padded = ((counts + MOE_BLOCK - 1) // MOE_BLOCK) * MOE_BLOCK
    pad_ends = jnp.cumsum(padded)
    pad_starts = pad_ends - padded
    dest = pad_starts[se] + (jnp.arange(n_assign, dtype=jnp.int32) - starts[se])
    n_blocks = -(-n_assign // MOE_BLOCK) + MOE_EXPERTS
    buf_tok = jnp.full((n_blocks * MOE_BLOCK,), t, jnp.int32).at[dest].set(st)
    buf_w = jnp.zeros((n_blocks * MOE_BLOCK,), jnp.float32).at[dest].set(sw)
    block_start = jnp.arange(n_blocks, dtype=jnp.int32) * MOE_BLOCK
    block_e = jnp.clip(jnp.searchsorted(pad_ends, block_start, side='right'), 0, MOE_EXPERTS - 1)
    h_pad = jnp.concatenate([h, jnp.zeros((1, d), h.dtype)], axis=0)
    xb = h_pad[buf_tok].reshape(n_blocks, MOE_BLOCK, d)

    def expert_block(args):
        xblk, e = args
        return (jax.nn.silu(xblk @ w1[e]) * (xblk @ w3[e])) @ w2[e]

    yb = lax.map(expert_block, (xb, block_e)).reshape(n_blocks * MOE_BLOCK, d)
    yb = yb.astype(jnp.float32) * buf_w[:, None]
    return jnp.zeros((t + 1, d), jnp.float32).at[buf_tok].add(yb)[:t]


def setup_inputs(seed: int = 0) -> dict:
    key = jax.random.key(seed)
    ks = jax.random.split(key, 32)
    L, D = DEPTH, D_MODEL
    f32 = jnp.float32

    def nrm(k, shape, fan_in):
        return jax.random.normal(k, shape, f32) * (fan_in ** -0.5)

    def gain(k, shape):
        return 1.0 + 0.02 * jax.random.normal(k, shape, f32)

    def small(k, shape, scale):
        return scale * jax.random.normal(k, shape, f32)

    x = jax.random.normal(ks[0], (BATCH, SEQ, D), f32)
    c = jax.random.normal(ks[1], (BATCH, D), f32)
    positions = (jax.random.randint(ks[2], (BATCH, 1), 0, 4096, dtype=jnp.int32)
                 + jnp.arange(SEQ, dtype=jnp.int32)[None, :])
    dt0 = jnp.exp(jax.random.uniform(ks[9], (L, SSD_HEADS), f32,
                                     minval=math.log(1e-3), maxval=math.log(1e-1)))
    return {
        'x': x,
        'c': c,
        'positions': positions,
        'w_ada': nrm(ks[3], (L, D, 6 * D), D),
        'b_ada': small(ks[4], (L, 6 * D), 0.02),
        'norm_mix_w': gain(ks[5], (L, D)),
        'w_in': nrm(ks[6], (L, D, IN_DIM), D),
        'ssd_conv_w': nrm(ks[7], (L, SSD_CONV, SSD_CONV_DIM), SSD_CONV),
        'ssd_conv_b': small(ks[8], (L, SSD_CONV_DIM), 0.02),
        'ssd_dt_bias': dt0 + jnp.log(-jnp.expm1(-dt0)),
        'ssd_a_log': jnp.log(jax.random.uniform(ks[10], (L, SSD_HEADS), f32, minval=1.0, maxval=16.0)),
        'ssd_d': gain(ks[11], (L, SSD_HEADS)),
        'ssd_norm_w': gain(ks[12], (L, SSD_INNER)),
        'hg_lb': small(ks[13], (L, HG_DIM), 0.1),
        'hg_norm_w': gain(ks[14], (L, HG_EXPAND)),
        'w_ret_o': nrm(ks[15], (L, RET_VW, D), RET_VW),
        'w_hg_o': nrm(ks[16], (L, HG_DIM, D), HG_DIM),
        'w_ssd_o': nrm(ks[17], (L, SSD_INNER, D), SSD_INNER),
        'w_out': nrm(ks[18], (L, D, D), D),
        'norm_ffn_w': gain(ks[19], (L, D)),
        'moe_w_group': nrm(ks[20], (L, D, MOE_GROUPS), D),
        'moe_b_group': small(ks[21], (L, MOE_GROUPS), 0.01),
        'moe_w_expert': nrm(ks[22], (L, D, MOE_EXPERTS), D),
        'moe_b_expert': small(ks[23], (L, MOE_EXPERTS), 0.01),
        'moe_w1': nrm(ks[24], (L, MOE_EXPERTS, D, MOE_FF), D),
        'moe_w3': nrm(ks[25], (L, MOE_EXPERTS, D, MOE_FF), D),
        'moe_w2': nrm(ks[26], (L, MOE_EXPERTS, MOE_FF, D), MOE_FF),
        'final_norm_w': gain(ks[27], (D,)),
    }


def reference(x, c, positions, w_ada, b_ada, norm_mix_w, w_in, ssd_conv_w, ssd_conv_b,
              ssd_dt_bias, ssd_a_log, ssd_d, ssd_norm_w, hg_lb, hg_norm_w, w_ret_o, w_hg_o,
              w_ssd_o, w_out, norm_ffn_w, moe_w_group, moe_b_group, moe_w_expert, moe_b_expert,
              moe_w1, moe_w3, moe_w2, final_norm_w):
    b, s, d = x.shape
    c_act = jax.nn.silu(c.astype(jnp.float32))
    lb_soft = jax.nn.softmax(hg_lb.astype(jnp.float32), axis=0)
    lb_all = jnp.cumsum(lb_soft, axis=0) - lb_soft[0]
    for l in range(DEPTH):
        mod = c_act @ w_ada[l] + b_ada[l].astype(jnp.float32)
        sh_a, sc_a, gt_a, sh_f, sc_f, gt_f = [m[:, None, :] for m in split_last(mod, (D_MODEL,) * 6)]
        h = rms_norm(x, norm_mix_w[l]) * (1.0 + sc_a) + sh_a
        y = hybrid_mixer(h, positions, lb_all[l], w_in[l], ssd_conv_w[l], ssd_conv_b[l],
                         ssd_dt_bias[l], ssd_a_log[l], ssd_d[l], ssd_norm_w[l], hg_norm_w[l],
                         w_ret_o[l], w_hg_o[l], w_ssd_o[l], w_out[l])
        x = x + (gt_a * y).astype(x.dtype)
        h = rms_norm(x, norm_ffn_w[l]) * (1.0 + sc_f) + sh_f
        y = hier_moe(h.reshape(b * s, d), moe_w_group[l], moe_b_group[l], moe_w_expert[l],
                     moe_b_expert[l], moe_w1[l], moe_w3[l], moe_w2[l]).reshape(b, s, d)
        x = x + (gt_f * y).astype(x.dtype)
    return rms_norm(x, final_norm_w).astype(x.dtype)
```

```python
import functools

import jax
import jax.numpy as jnp
from jax import lax
from jax.experimental import pallas as pl
from jax.experimental.pallas import tpu as pltpu

F32 = jnp.float32
BF16 = jnp.bfloat16

EPS = 1e-6
ROPE_BASE = 10000.0
RET_HEADS = 4
RET_DK = 128
RET_DV = 256
HG_HEADS = 8
HG_DK = 128
SSD_HEADS = 32
SSD_P = 64
SSD_GROUPS = 4
SSD_N = 128
SSD_HPG = SSD_HEADS // SSD_GROUPS
SSD_CONV = 4
MOE_GROUPS = 4
MOE_PER_GROUP = 8
MOE_EXPERTS = 32
MOE_BLOCK = 256
CHUNK = 128
LANES = 128
SUBLANES = 8
VMEM_LIMIT = 56 * 1024 * 1024
HG_LEVELS = (64, 32, 16, 8, 4, 2, 1)


def _params(n_axes, vmem=VMEM_LIMIT):
    return pltpu.CompilerParams(dimension_semantics=("arbitrary",) * n_axes,
                                vmem_limit_bytes=vmem)


def _const_spec(shape):
    nd = len(shape)
    return pl.BlockSpec(shape, lambda *_: (0,) * nd, pipeline_mode=pl.Buffered(1))


def _sigmoid(x):
    return 1.0 / (1.0 + jnp.exp(-x))


def _silu(x):
    return x * _sigmoid(x)


def _dot(a, b):
    return jnp.dot(a, b, preferred_element_type=F32)


def _dot_nt(a, b):
    return lax.dot_general(a, b, (((1,), (1,)), ((), ())), preferred_element_type=F32)


def _dot_tn(a, b):
    return lax.dot_general(a, b, (((0,), (0,)), ((), ())), preferred_element_type=F32)


def _split3(x):
    hi = x.astype(BF16)
    r1 = x - hi.astype(F32)
    mid = r1.astype(BF16)
    lo = (r1 - mid.astype(F32)).astype(BF16)
    return hi, mid, lo


def _dot3(m_bf16, x_f32):
    hi, mid, lo = _split3(x_f32)
    return _dot(m_bf16, hi) + _dot(m_bf16, mid) + _dot(m_bf16, lo)


def _ada_kernel(c_ref, w_ref, b_ref, o_ref):
    c = c_ref[...]
    o_ref[...] = _dot(_silu(c).astype(BF16), w_ref[...].astype(BF16)) + b_ref[...]


def _adaln(c, w_ada, b_ada):
    L, D, N = w_ada.shape
    B = c.shape[0]
    cp = jnp.zeros((SUBLANES, D), F32).at[:B].set(c.astype(F32))
    tn = 1536
    out = pl.pallas_call(
        _ada_kernel,
        grid=(L, N // tn),
        in_specs=[pl.BlockSpec((SUBLANES, D), lambda l, n: (0, 0)),
                  pl.BlockSpec((None, D, tn), lambda l, n: (l, 0, n)),
                  pl.BlockSpec((None, 1, tn), lambda l, n: (l, 0, n))],
        out_specs=pl.BlockSpec((None, SUBLANES, tn), lambda l, n: (l, 0, n)),
        out_shape=jax.ShapeDtypeStruct((L, SUBLANES, N), F32),
        compiler_params=_params(2),
        name="adaln",
    )(cp, w_ada, b_ada.reshape(L, 1, N))
    return out[:, :B].reshape(L, B, 6, D)


def _rope_kernel(pos_ref, invf_ref, sign_ref, cos_ref, sin_ref):
    ang = pos_ref[...].astype(F32) * invf_ref[...]
    cos_ref[...] = jnp.cos(ang)
    sin_ref[...] = jnp.sin(ang) * sign_ref[...]


def _rope_tables(positions, ts):
    B, S = positions.shape
    half = RET_DK // 2
    inv_freq = ROPE_BASE ** (-jnp.linspace(0.0, 1.0, half, dtype=F32))
    invf = jnp.concatenate([inv_freq, inv_freq]).reshape(1, RET_DK)
    sign = jnp.concatenate([-jnp.ones((half,), F32), jnp.ones((half,), F32)]).reshape(1, RET_DK)
    spec = pl.BlockSpec((None, ts, RET_DK), lambda b, s: (b, s, 0))
    return pl.pallas_call(
        _rope_kernel,
        grid=(B, S // ts),
        in_specs=[pl.BlockSpec((None, ts, 1), lambda b, s: (b, s, 0)),
                  pl.BlockSpec((1, RET_DK), lambda b, s: (0, 0)),
                  pl.BlockSpec((1, RET_DK), lambda b, s: (0, 0))],
        out_specs=[spec, spec],
        out_shape=[jax.ShapeDtypeStruct((B, S, RET_DK), F32)] * 2,
        compiler_params=_params(2),
        name="rope_tables",
    )(positions.reshape(B, S, 1), invf, sign)


def _rms_mod(x, w, scale, shift):
    y = x * lax.rsqrt(jnp.mean(x * x, axis=-1, keepdims=True) + EPS)
    return (y * w) * (1.0 + scale) + shift


def _prenorm_kernel(x_ref, w_ref, mod_ref, o_ref):
    o_ref[...] = _rms_mod(x_ref[...], w_ref[...], mod_ref[1:2, :], mod_ref[0:1, :]).astype(o_ref.dtype)


def _prenorm(x, w, mod_l, ts):
    B, S, D = x.shape
    return pl.pallas_call(
        _prenorm_kernel,
        grid=(B, S // ts),
        in_specs=[pl.BlockSpec((None, ts, D), lambda b, s: (b, s, 0)),
                  pl.BlockSpec((1, D), lambda b, s: (0, 0)),
                  pl.BlockSpec((None, 6, D), lambda b, s: (b, 0, 0))],
        out_specs=pl.BlockSpec((None, ts, D), lambda b, s: (b, s, 0)),
        out_shape=jax.ShapeDtypeStruct((B, S, D), BF16),
        compiler_params=_params(2),
        name="prenorm",
    )(x, w.reshape(1, D), mod_l)


def _final_norm_kernel(x_ref, w_ref, o_ref):
    x = x_ref[...]
    y = x * lax.rsqrt(jnp.mean(x * x, axis=-1, keepdims=True) + EPS)
    o_ref[...] = (y * w_ref[...]).astype(o_ref.dtype)


def _final_norm(x, w, ts):
    B, S, D = x.shape
    return pl.pallas_call(
        _final_norm_kernel,
        grid=(B, S // ts),
        in_specs=[pl.BlockSpec((None, ts, D), lambda b, s: (b, s, 0)),
                  pl.BlockSpec((1, D), lambda b, s: (0, 0))],
        out_specs=pl.BlockSpec((None, ts, D), lambda b, s: (b, s, 0)),
        out_shape=jax.ShapeDtypeStruct((B, S, D), x.dtype),
        compiler_params=_params(2),
        name="final_norm",
    )(x, w.reshape(1, D))


def _ret_kernel(h_ref, cos_ref, sin_ref, wq_ref, wk_ref, wv_ref, wg_ref, wo_ref,
                di_ref, dq_ref, dk_ref, dc_ref, o_ref,
                st_ref, q_s, k_s, v_s, g_s):
    ts = h_ref.shape[0]

    @pl.when(pl.program_id(1) == 0)
    def _():
        st_ref[...] = jnp.zeros_like(st_ref)

    h = h_ref[...]
    cos = cos_ref[...]
    sin = sin_ref[...]
    q = _dot(h, wq_ref[...])
    k = _dot(h, wk_ref[...])
    scale = RET_DK ** -0.5
    for hh in range(RET_HEADS):
        sl = slice(hh * RET_DK, (hh + 1) * RET_DK)
        qh = q[:, sl]
        kh = k[:, sl]
        q_s[:, sl] = (qh * cos + pltpu.roll(qh, RET_DK // 2, 1) * sin).astype(BF16)
        k_s[:, sl] = (kh * cos + pltpu.roll(kh, RET_DK // 2, 1) * sin) * scale
    v_s[...] = _dot(h, wv_ref[...]).astype(BF16)
    g_s[...] = _silu(_dot(h, wg_ref[...]))

    def chunk(c, carry):
        rows = pl.ds(pl.multiple_of(c * CHUNK, CHUNK), CHUNK)
        for hh in range(RET_HEADS):
            ksl = slice(hh * RET_DK, (hh + 1) * RET_DK)
            vsl = slice(hh * RET_DV, (hh + 1) * RET_DV)
            qc = q_s[rows, ksl]
            kc = k_s[rows, ksl]
            vc = v_s[rows, vsl]
            st = st_ref[hh]
            scores = _dot_nt(qc, kc.astype(BF16)) * di_ref[hh]
            o = _dot(scores.astype(BF16), vc) + _dot(qc, st.astype(BF16)) * dq_ref[hh]
            st_ref[hh] = st * dc_ref[hh] + _dot_tn((kc * dk_ref[hh]).astype(BF16), vc)
            mu = jnp.mean(o, axis=-1, keepdims=True)
            d = o - mu
            var = jnp.mean(d * d, axis=-1, keepdims=True)
            g_s[rows, vsl] = g_s[rows, vsl] * (d * lax.rsqrt(var + EPS))
        return carry

    lax.fori_loop(0, ts // CHUNK, chunk, 0)
    o_ref[...] = _dot(g_s[...].astype(BF16), wo_ref[...]).astype(o_ref.dtype)


def _retention(h, cos, sin, wq, wk, wv, wg, wo, ts):
    B, S, D = h.shape
    H, C = RET_HEADS, CHUNK
    log_gamma = jnp.log(1.0 - 2.0 ** (-5.0 - jnp.arange(H, dtype=F32)))
    idx = jnp.arange(C, dtype=F32)
    rel = idx[:, None] - idx[None, :]
    di = jnp.where(rel >= 0.0, jnp.exp(log_gamma[:, None, None] * jnp.maximum(rel, 0.0)), 0.0)
    dq = jnp.broadcast_to(jnp.exp(log_gamma[:, None] * (idx[None, :] + 1.0))[:, :, None], (H, C, RET_DV))
    dk = jnp.broadcast_to(jnp.exp(log_gamma[:, None] * (C - 1.0 - idx[None, :]))[:, :, None], (H, C, RET_DK))
    dc = jnp.broadcast_to(jnp.exp(log_gamma * C)[:, None, None], (H, 1, RET_DV))
    qk, vw = H * RET_DK, H * RET_DV
    tile = lambda w: pl.BlockSpec((None, ts, w), lambda b, s: (b, s, 0))
    return pl.pallas_call(
        _ret_kernel,
        grid=(B, S // ts),
        in_specs=[tile(D), tile(RET_DK), tile(RET_DK),
                  _const_spec((D, qk)), _const_spec((D, qk)), _const_spec((D, vw)), _const_spec((D, vw)),
                  _const_spec((vw, D)),
                  _const_spec((H, C, C)), _const_spec((H, C, RET_DV)), _const_spec((H, C, RET_DK)),
                  _const_spec((H, 1, RET_DV))],
        out_specs=tile(D),
        out_shape=jax.ShapeDtypeStruct((B, S, D), BF16),
        scratch_shapes=[pltpu.VMEM((H, RET_DK, RET_DV), F32),
                        pltpu.VMEM((ts, qk), BF16), pltpu.VMEM((ts, qk), F32),
                        pltpu.VMEM((ts, vw), BF16), pltpu.VMEM((ts, vw), F32)],
        compiler_params=_params(2),
        name="retention",
    )(h, cos, sin, wq, wk, wv, wg, wo, di, dq, dk, dc)


def _hg_level_matrices():
    C = CHUNK
    r = jnp.arange(C)
    tril = (r[:, None] >= r[None, :])
    mats = [tril]
    for b in (4, 2):
        mid = (r // (2 * b)) * (2 * b) + b
        odd = (r & b) != 0
        tau = r[None, :]
        m_odd = odd[:, None] & (tau >= mid[:, None]) & (tau <= r[:, None])
        m_even = (~odd)[:, None] & (tau > r[:, None]) & (tau < mid[:, None])
        mats.append(m_odd | m_even)
    stack = jnp.concatenate(mats, axis=0).astype(BF16)
    masks = []
    for b in HG_LEVELS:
        masks.append((r[:, None] // (2 * b)) == (r[None, :] // (2 * b)))
    masks.append(r[:, None] == r[None, :])
    return stack, jnp.stack(masks).astype(F32)


def _hg_kernel(h_ref, wq_ref, wf_ref, wi_ref, wg_ref, wo_ref, lb_ref, nw_ref, cm_ref, mk_ref, o_ref,
               st_ref, q_s, k_s, v_s, lf_s, g_s, G_s, a_s):
    ts = h_ref.shape[0]
    C = CHUNK
    W = HG_HEADS * HG_DK

    @pl.when(pl.program_id(1) == 0)
    def _():
        st_ref[...] = jnp.zeros_like(st_ref)

    h = h_ref[...]
    lb = lb_ref[...]
    q_s[...] = _silu(_dot(h, wq_ref[...]))
    hf = _dot(h, wf_ref[...])
    lf_s[...] = jnp.log(lb + (1.0 - lb) * _sigmoid(hf))
    k_s[...] = (1.0 - lb) * _sigmoid(-hf)
    v_s[...] = _dot(h, wi_ref[...]).astype(BF16)
    g_s[...] = _silu(_dot(h, wg_ref[...]))
    odd_row = {b: (lax.broadcasted_iota(jnp.int32, (C, 1), 0) & b) != 0 for b in HG_LEVELS}

    def chunk(c, carry):
        rows = pl.ds(pl.multiple_of(c * C, C), C)
        lf = lf_s[rows, :]
        q = q_s[rows, :]
        k = k_s[rows, :]
        sums = _dot3(cm_ref[...], lf)
        G = sums[0:C]
        G_s[...] = G
        for li, b in enumerate(HG_LEVELS):
            if b >= 8:
                n = C // (2 * b)
                ref = jnp.concatenate(
                    [jnp.broadcast_to(G_s[j * 2 * b + b - 1:j * 2 * b + b, :], (2 * b, W)) for j in range(n)],
                    axis=0)
                x = jnp.exp(-jnp.abs(G - ref))
                qt = jnp.where(odd_row[b], q * x, 0.0)
                kt = jnp.where(odd_row[b], 0.0, k * x)
            elif b > 1:
                e = sums[C:2 * C] if b == 4 else sums[2 * C:3 * C]
                x = jnp.exp(e)
                qt = jnp.where(odd_row[b], q * x, 0.0)
                kt = jnp.where(odd_row[b], 0.0, k * x)
            else:
                qt = jnp.where(odd_row[b], q * jnp.exp(lf), 0.0)
                kt = jnp.where(odd_row[b], 0.0, k)
            qt = qt.astype(BF16)
            kt = kt.astype(BF16)
            for hh in range(HG_HEADS):
                sl = slice(hh * HG_DK, (hh + 1) * HG_DK)
                a = _dot_nt(qt[:, sl], kt[:, sl])
                if li == 0:
                    a_s[hh] = a
                else:
                    a_s[hh] = a_s[hh] + a * mk_ref[li]
        g_last = G_s[C - 1:C, :]
        qg = (q * jnp.exp(G)).astype(BF16)
        kh = (k * jnp.exp(g_last - G)).astype(BF16)
        qk = q * k
        dec = jnp.exp(g_last)
        eye = mk_ref[len(HG_LEVELS)]
        for hh in range(HG_HEADS):
            sl = slice(hh * HG_DK, (hh + 1) * HG_DK)
            vh = v_s[rows, sl]
            diag = jnp.sum(qk[:, sl], axis=-1, keepdims=True)
            a = (a_s[hh] + diag * eye).astype(BF16)
            st = st_ref[hh]
            o = _dot(a, vh) + _dot_nt(qg[:, sl], st.astype(BF16))
            st_ref[hh] = st * dec[:, sl] + _dot_tn(vh, kh[:, sl])
            on = o * lax.rsqrt(jnp.mean(o * o, axis=-1, keepdims=True) + EPS) * nw_ref[...]
            g_s[rows, sl] = on * g_s[rows, sl]
        return carry

    lax.fori_loop(0, ts // C, chunk, 0)
    o_ref[...] = _dot(g_s[...].astype(BF16), wo_ref[...]).astype(o_ref.dtype)


def _hgrn2(h, wq, wf, wi, wg, wo, lb, norm_w, ts):
    B, S, D = h.shape
    W = HG_HEADS * HG_DK
    C = CHUNK
    cm, mk = _hg_level_matrices()
    tile = lambda w: pl.BlockSpec((None, ts, w), lambda b, s: (b, s, 0))
    return pl.pallas_call(
        _hg_kernel,
        grid=(B, S // ts),
        in_specs=[tile(D)] + [_const_spec((D, W))] * 4 + [_const_spec((W, D)), _const_spec((1, W)),
                  _const_spec((1, HG_DK)), _const_spec(cm.shape), _const_spec(mk.shape)],
        out_specs=tile(D),
        out_shape=jax.ShapeDtypeStruct((B, S, D), BF16),
        scratch_shapes=[pltpu.VMEM((HG_HEADS, HG_DK, HG_DK), F32),
                        pltpu.VMEM((ts, W), F32), pltpu.VMEM((ts, W), F32), pltpu.VMEM((ts, W), BF16),
                        pltpu.VMEM((ts, W), F32), pltpu.VMEM((ts, W), F32),
                        pltpu.VMEM((C, W), F32), pltpu.VMEM((HG_HEADS, C, C), F32)],
        compiler_params=_params(2),
        name="hgrn2",
    )(h, wq, wf, wi, wg, wo, lb.reshape(1, W), norm_w.reshape(1, HG_DK), cm, mk)


def _softplus(x):
    return jnp.maximum(x, 0.0) + jnp.log(1.0 + jnp.exp(-jnp.abs(x)))


def _ssd_kernel(h_ref, wz_ref, wx_ref, wdt_ref, wo_ref, cw_ref, cb_ref, dtb_ref, a_ref, dl_ref, nw_ref,
                tril_ref, ex_ref, o_ref,
                st_ref, xb_s, xc_s, z_s, dt_s, y_s):
    ts = h_ref.shape[0]
    C = CHUNK
    inner = SSD_HEADS * SSD_P
    gw = SSD_GROUPS * SSD_N
    gp = SSD_HPG * SSD_P
    halo = SUBLANES

    @pl.when(pl.program_id(1) == 0)
    def _():
        st_ref[...] = jnp.zeros_like(st_ref)
        xb_s[0:halo, :] = jnp.zeros((halo, xb_s.shape[1]), F32)

    h = h_ref[...]
    z_s[...] = _silu(_dot(h, wz_ref[...]))
    xb_s[halo:halo + ts, :] = _dot(h, wx_ref[...])
    dt_s[...] = _softplus(_dot(h, wdt_ref[...]) + dtb_ref[...])
    conv = cb_ref[...]
    for j in range(SSD_CONV):
        off = halo - (SSD_CONV - 1) + j
        conv = conv + cw_ref[j:j + 1, :] * xb_s[off:off + ts, :]
    xc_s[...] = _silu(conv)
    xb_s[0:halo, :] = xb_s[ts:ts + halo, :]

    causal = lax.broadcasted_iota(jnp.int32, (C, C), 0) >= lax.broadcasted_iota(jnp.int32, (C, C), 1)

    def chunk(c, carry):
        rows = pl.ds(pl.multiple_of(c * C, C), C)
        dt = dt_s[rows, :]
        da = dt * a_ref[...]
        cum = jnp.dot(tril_ref[...], da, preferred_element_type=F32, precision=lax.Precision.HIGHEST)
        cum_t = cum.T
        dt_t = dt.T
        ecum = jnp.exp(cum)
        cum_last = cum[C - 1:C, :]
        wfac = dt * jnp.exp(cum_last - cum)
        w_hi = wfac.astype(BF16)
        w_lo = (wfac - w_hi.astype(F32)).astype(BF16)
        wexp = _dot(w_hi, ex_ref[...]) + _dot(w_lo, ex_ref[...])
        dl = jnp.broadcast_to(jnp.exp(cum_last), (SUBLANES, LANES))
        d_hi = dl.astype(BF16)
        d_lo = (dl - d_hi.astype(F32)).astype(BF16)
        dexp = (_dot(d_hi, ex_ref[...]) + _dot(d_lo, ex_ref[...]))[0:1, :]
        for g in range(SSD_GROUPS):
            bg = xc_s[rows, inner + g * SSD_N:inner + (g + 1) * SSD_N]
            cg = xc_s[rows, inner + gw + g * SSD_N:inner + gw + (g + 1) * SSD_N]
            bgb = bg.astype(BF16)
            cb = _dot_nt(cg.astype(BF16), bgb)
            st = st_ref[g]
            stb = st.astype(BF16)
            xg = xc_s[rows, g * gp:(g + 1) * gp]
            xgb = xg.astype(BF16)
            for j in range(SSD_HPG):
                hd = g * SSD_HPG + j
                col = cum[:, hd:hd + 1]
                diff = col - cum_t[hd:hd + 1, :]
                lmat = jnp.where(causal, jnp.exp(jnp.where(causal, diff, 0.0)), 0.0)
                w = cb * lmat * dt_t[hd:hd + 1, :]
                lhs = jnp.concatenate([w.astype(BF16), (cg * ecum[:, hd:hd + 1]).astype(BF16)], axis=1)
                psl = slice(j * SSD_P, (j + 1) * SSD_P)
                rhs = jnp.concatenate([xgb[:, psl], stb[:, psl]], axis=0)
                y_s[rows, hd * SSD_P:(hd + 1) * SSD_P] = _dot(lhs, rhs)
            gsl = slice(g * gp, (g + 1) * gp)
            st_ref[g] = st * dexp[:, gsl] + _dot_tn(bgb, (xg * wexp[:, gsl]).astype(BF16))
        xs = xc_s[rows, 0:inner]
        yz = (y_s[rows, :] + dl_ref[...] * xs) * z_s[rows, :]
        gi = inner // SSD_GROUPS
        for g in range(SSD_GROUPS):
            sl = slice(g * gi, (g + 1) * gi)
            seg = yz[:, sl]
            nrm = seg * lax.rsqrt(jnp.mean(seg * seg, axis=-1, keepdims=True) + EPS)
            y_s[rows, sl] = nrm * nw_ref[:, sl]
        return carry

    lax.fori_loop(0, ts // C, chunk, 0)
    o_ref[...] = _dot(y_s[...].astype(BF16), wo_ref[...]).astype(o_ref.dtype)


def _ssd(h, wz, wx, wdt, wo, conv_w, conv_b, dt_bias, a_log, d_skip, norm_w, ts):
    B, S, D = h.shape
    C = CHUNK
    inner = SSD_HEADS * SSD_P
    cdim = inner + 2 * SSD_GROUPS * SSD_N
    pad = LANES - SSD_HEADS
    wdt_p = jnp.pad(wdt, ((0, 0), (0, pad)))
    dtb = jnp.pad(dt_bias.astype(F32), (0, pad)).reshape(1, LANES)
    a = jnp.pad(-jnp.exp(a_log.astype(F32)), (0, pad)).reshape(1, LANES)
    dl = jnp.repeat(d_skip.astype(F32), SSD_P).reshape(1, inner)
    r = jnp.arange(C)
    tril = (r[:, None] >= r[None, :]).astype(F32)
    ex = (jnp.arange(LANES)[:, None] == (jnp.arange(inner)[None, :] // SSD_P)).astype(BF16)
    tile = lambda w: pl.BlockSpec((None, ts, w), lambda b, s: (b, s, 0))
    return pl.pallas_call(
        _ssd_kernel,
        grid=(B, S // ts),
        in_specs=[tile(D), _const_spec((D, inner)), _const_spec((D, cdim)), _const_spec((D, LANES)),
                  _const_spec((inner, D)), _const_spec((SSD_CONV, cdim)), _const_spec((1, cdim)),
                  _const_spec((1, LANES)), _const_spec((1, LANES)), _const_spec((1, inner)),
                  _const_spec((1, inner)), _const_spec((C, C)), _const_spec((LANES, inner))],
        out_specs=tile(D),
        out_shape=jax.ShapeDtypeStruct((B, S, D), BF16),
        scratch_shapes=[pltpu.VMEM((SSD_GROUPS, SSD_N, SSD_HPG * SSD_P), F32),
                        pltpu.VMEM((ts + SUBLANES, cdim), F32), pltpu.VMEM((ts, cdim), F32),
                        pltpu.VMEM((ts, inner), F32), pltpu.VMEM((ts, LANES), F32),
                        pltpu.VMEM((ts, inner), F32)],
        compiler_params=_params(2),
        name="ssd",
    )(h, wz, wx, wdt_p, wo, conv_w.astype(F32), conv_b.astype(F32).reshape(1, cdim), dtb, a, dl,
      norm_w.astype(F32).reshape(1, inner), tril, ex)


def _merge_kernel(x_ref, h_ref, yr_ref, yh_ref, ys_ref, wgl_ref, wout_ref, mod_ref, nw_ref,
                  wr_hi_ref, wr_lo_ref, br_ref, x1_ref, h2_ref, rf_ref, ri_ref):
    D = x_ref.shape[1]
    ts = x_ref.shape[0]
    h = h_ref[...]
    gates = _sigmoid(_dot(h, wgl_ref[...]))
    m = (gates[:, 0:D] * yr_ref[...].astype(F32) + gates[:, D:2 * D] * yh_ref[...].astype(F32)
         + gates[:, 2 * D:3 * D] * ys_ref[...].astype(F32))
    y = _dot(m.astype(BF16), wout_ref[...])
    x1 = x_ref[...] + mod_ref[2:3, :] * y
    x1_ref[...] = x1
    h2 = _rms_mod(x1, nw_ref[...], mod_ref[4:5, :], mod_ref[3:4, :])
    h2_ref[...] = h2
    h_hi = h2.astype(BF16)
    h_lo = (h2 - h_hi.astype(F32)).astype(BF16)
    lg = (_dot_nt(wr_hi_ref[...], h_hi) + _dot_nt(wr_hi_ref[...], h_lo) + _dot_nt(wr_lo_ref[...], h_hi)
          + br_ref[...])
    gl = lg[0:MOE_GROUPS]
    gmax = jnp.max(gl, axis=0, keepdims=True)
    gi = lax.broadcasted_iota(jnp.int32, gl.shape, 0)
    gtop = jnp.min(jnp.where(gl == gmax, gi, MOE_GROUPS), axis=0, keepdims=True)
    p_g = 1.0 / jnp.sum(jnp.exp(gl - gmax), axis=0, keepdims=True)
    el = jnp.zeros((MOE_PER_GROUP, ts), F32)
    for g in range(MOE_GROUPS):
        lo = SUBLANES + g * MOE_PER_GROUP
        el = jnp.where(gtop == g, lg[lo:lo + MOE_PER_GROUP], el)
    ei = lax.broadcasted_iota(jnp.int32, el.shape, 0)
    m1 = jnp.max(el, axis=0, keepdims=True)
    i1 = jnp.min(jnp.where(el == m1, ei, MOE_PER_GROUP), axis=0, keepdims=True)
    el2 = jnp.where(ei == i1, -jnp.inf, el)
    m2 = jnp.max(el2, axis=0, keepdims=True)
    i2 = jnp.min(jnp.where(el2 == m2, ei, MOE_PER_GROUP), axis=0, keepdims=True)
    e21 = jnp.exp(m2 - m1)
    den = 1.0 + e21
    w1 = (1.0 / den) * p_g
    w2 = (e21 / den) * p_g
    zf = jnp.zeros((SUBLANES - 2, ts), F32)
    rf_ref[...] = jnp.concatenate([w1, w2, zf], axis=0)
    e1 = gtop * MOE_PER_GROUP + i1
    e2 = gtop * MOE_PER_GROUP + i2
    ri_ref[...] = jnp.concatenate([e1, e2, jnp.zeros((SUBLANES - 2, ts), jnp.int32)], axis=0)


def _merge(x, h, y_ret, y_hg, y_ssd, wgl, wout, mod_l, norm_w, wr_hi, wr_lo, br, ts):
    B, S, D = x.shape
    ns = S // ts
    R = wr_hi.shape[0]
    tile = pl.BlockSpec((None, ts, D), lambda b, s: (b, s, 0))
    rt = pl.BlockSpec((SUBLANES, ts), lambda b, s: (0, b * ns + s))
    return pl.pallas_call(
        _merge_kernel,
        grid=(B, ns),
        in_specs=[tile, tile, tile, tile, tile, _const_spec((D, 3 * D)), _const_spec((D, D)),
                  pl.BlockSpec((None, 6, D), lambda b, s: (b, 0, 0)), _const_spec((1, D)),
                  _const_spec((R, D)), _const_spec((R, D)), _const_spec((R, 1))],
        out_specs=[tile, tile, rt, rt],
        out_shape=[jax.ShapeDtypeStruct((B, S, D), F32), jax.ShapeDtypeStruct((B, S, D), F32),
                   jax.ShapeDtypeStruct((SUBLANES, B * S), F32), jax.ShapeDtypeStruct((SUBLANES, B * S), jnp.int32)],
        compiler_params=_params(2),
        name="merge_router",
    )(x, h, y_ret, y_hg, y_ssd, wgl, wout, mod_l, norm_w.reshape(1, D), wr_hi, wr_lo, br)


def _row_copy(src, src_row, dst, dst_row, sem):
    return pltpu.make_async_copy(src.at[pl.ds(src_row, 1), :], dst.at[pl.ds(dst_row, 1), :], sem)


def _expert_kernel(be_ref, nv_ref, tok_ref, dst_ref, wt_ref, h2_hbm, w1_ref, w3_ref, w2_ref, y2_hbm,
                   xbuf, ybuf, gsem, ssem):
    i = pl.program_id(0)
    nv = nv_ref[i]

    @pl.when(i == 0)
    def _():
        xbuf[...] = jnp.zeros_like(xbuf)

    @pl.when(nv > 0)
    def _():
        def g_start(r, c):
            _row_copy(h2_hbm, tok_ref[0, 0, r], xbuf, r, gsem).start()
            return c

        def g_wait(r, c):
            _row_copy(h2_hbm, 0, xbuf, r, gsem).wait()
            return c

        lax.fori_loop(0, nv, g_start, 0)
        lax.fori_loop(0, nv, g_wait, 0)
        x = xbuf[...].astype(BF16)
        a = _dot(x, w1_ref[...])
        b = _dot(x, w3_ref[...])
        y = _dot((_silu(a) * b).astype(BF16), w2_ref[...])
        ybuf[...] = y * wt_ref[...]

        def s_start(r, c):
            _row_copy(ybuf, r, y2_hbm, dst_ref[0, 0, r], ssem).start()
            return c

        def s_wait(r, c):
            _row_copy(ybuf, r, y2_hbm, 0, ssem).wait()
            return c

        lax.fori_loop(0, nv, s_start, 0)
        lax.fori_loop(0, nv, s_wait, 0)


def _experts(h2, block_e, nvalid, buf_tok, buf_dst, buf_w, w1, w3, w2):
    T, D = h2.shape
    nb = block_e.shape[0]
    ff = w1.shape[2]
    BLK = MOE_BLOCK
    grid_spec = pltpu.PrefetchScalarGridSpec(
        num_scalar_prefetch=2,
        grid=(nb,),
        in_specs=[pl.BlockSpec((1, 1, BLK), lambda i, be, nv: (i, 0, 0), memory_space=pltpu.SMEM),
                  pl.BlockSpec((1, 1, BLK), lambda i, be, nv: (i, 0, 0), memory_space=pltpu.SMEM),
                  pl.BlockSpec((None, BLK, 1), lambda i, be, nv: (i, 0, 0)),
                  pl.BlockSpec(memory_space=pl.ANY),
                  pl.BlockSpec((None, D, ff), lambda i, be, nv: (be[i], 0, 0)),
                  pl.BlockSpec((None, D, ff), lambda i, be, nv: (be[i], 0, 0)),
                  pl.BlockSpec((None, ff, D), lambda i, be, nv: (be[i], 0, 0))],
        out_specs=pl.BlockSpec(memory_space=pl.ANY),
        scratch_shapes=[pltpu.VMEM((BLK, D), F32), pltpu.VMEM((BLK, D), F32),
                        pltpu.SemaphoreType.DMA(()), pltpu.SemaphoreType.DMA(())])
    return pl.pallas_call(
        _expert_kernel,
        grid_spec=grid_spec,
        out_shape=jax.ShapeDtypeStruct((2 * T, D), F32),
        compiler_params=pltpu.CompilerParams(dimension_semantics=("arbitrary",), vmem_limit_bytes=VMEM_LIMIT,
                                             has_side_effects=True),
        name="experts",
    )(block_e, nvalid, buf_tok.reshape(nb, 1, BLK), buf_dst.reshape(nb, 1, BLK), buf_w.reshape(nb, BLK, 1),
      h2, w1, w3, w2)


def _dispatch_plan(route_i, route_w, T):
    BLK = MOE_BLOCK
    n_assign = 2 * T
    nb = -(-n_assign // BLK) + MOE_EXPERTS
    flat_e = jnp.concatenate([route_i[0], route_i[1]])
    flat_w = jnp.concatenate([route_w[0], route_w[1]])
    tok = jnp.tile(jnp.arange(T, dtype=jnp.int32), 2)
    onehot = (flat_e[:, None] == jnp.arange(MOE_EXPERTS, dtype=jnp.int32)[None, :]).astype(jnp.int32)
    incl = jnp.cumsum(onehot, axis=0)
    rank = jnp.sum((incl - onehot) * onehot, axis=1)
    counts = incl[-1]
    padded = ((counts + BLK - 1) // BLK) * BLK
    pad_ends = jnp.cumsum(padded)
    pad_starts = pad_ends - padded
    dest = pad_starts[flat_e] + rank
    buf_tok = jnp.zeros((nb * BLK,), jnp.int32).at[dest].set(tok)
    buf_dst = jnp.zeros((nb * BLK,), jnp.int32).at[dest].set(jnp.arange(n_assign, dtype=jnp.int32))
    buf_w = jnp.zeros((nb * BLK,), F32).at[dest].set(flat_w)
    block_start = jnp.arange(nb, dtype=jnp.int32) * BLK
    block_e = jnp.clip(jnp.searchsorted(pad_ends, block_start, side='right'), 0, MOE_EXPERTS - 1).astype(jnp.int32)
    nvalid = jnp.clip(counts[block_e] - (block_start - pad_starts[block_e]), 0, BLK).astype(jnp.int32)
    return block_e, nvalid, buf_tok, buf_dst, buf_w


def _combine_kernel(x_ref, y0_ref, y1_ref, mod_ref, o_ref):
    o_ref[...] = x_ref[...] + mod_ref[5:6, :] * (y0_ref[...] + y1_ref[...])


def _combine(x1, y2, mod_l, ts):
    B, S, D = x1.shape
    y4 = y2.reshape(2, B, S, D)
    tile = pl.BlockSpec((None, ts, D), lambda b, s: (b, s, 0))
    return pl.pallas_call(
        _combine_kernel,
        grid=(B, S // ts),
        in_specs=[tile,
                  pl.BlockSpec((None, None, ts, D), lambda b, s: (0, b, s, 0)),
                  pl.BlockSpec((None, None, ts, D), lambda b, s: (1, b, s, 0)),
                  pl.BlockSpec((None, 6, D), lambda b, s: (b, 0, 0))],
        out_specs=tile,
        out_shape=jax.ShapeDtypeStruct((B, S, D), F32),
        compiler_params=_params(2),
        name="combine",
    )(x1, y4, y4, mod_l)


def _seq_tile(S, want):
    ts = min(want, S)
    assert S % ts == 0 and ts % CHUNK == 0
    return ts


def kernel(x, c, positions, w_ada, b_ada, norm_mix_w, w_in, ssd_conv_w, ssd_conv_b, ssd_dt_bias, ssd_a_log,
           ssd_d, ssd_norm_w, hg_lb, hg_norm_w, w_ret_o, w_hg_o, w_ssd_o, w_out, norm_ffn_w, moe_w_group,
           moe_b_group, moe_w_expert, moe_b_expert, moe_w1, moe_w3, moe_w2, final_norm_w):
    B, S, D = x.shape
    L = w_ada.shape[0]
    T = B * S
    ts_el = _seq_tile(S, 512)
    ts_mix = _seq_tile(S, 256)

    mod = _adaln(c, w_ada, b_ada)
    cos, sin = _rope_tables(positions, ts_el)
    lb_soft = jax.nn.softmax(hg_lb.astype(F32), axis=0)
    lb_all = jnp.cumsum(lb_soft, axis=0) - lb_soft[0]

    qk = RET_HEADS * RET_DK
    vw = RET_HEADS * RET_DV
    hgw = HG_HEADS * HG_DK
    inner = SSD_HEADS * SSD_P
    cdim = inner + 2 * SSD_GROUPS * SSD_N
    sizes = (qk, qk, vw, vw, hgw, hgw, hgw, hgw, inner, cdim, SSD_HEADS, 3 * D)
    offs = [0]
    for n in sizes:
        offs.append(offs[-1] + n)

    xcur = x.astype(F32)
    for l in range(L):
        wl = w_in[l].astype(BF16)
        cols = [wl[:, offs[i]:offs[i + 1]] for i in range(len(sizes))]
        h = _prenorm(xcur, norm_mix_w[l].astype(F32), mod[l], ts_el)
        y_ret = _retention(h, cos, sin, cols[0], cols[1], cols[2], cols[3], w_ret_o[l].astype(BF16), ts_mix)
        y_hg = _hgrn2(h, cols[4], cols[5], cols[6], cols[7], w_hg_o[l].astype(BF16), lb_all[l],
                      hg_norm_w[l].astype(F32), ts_mix)
        y_ssd = _ssd(h, cols[8], cols[9], cols[10], w_ssd_o[l].astype(BF16), ssd_conv_w[l], ssd_conv_b[l],
                     ssd_dt_bias[l], ssd_a_log[l], ssd_d[l], ssd_norm_w[l], ts_mix)
        wr = jnp.zeros((SUBLANES + MOE_EXPERTS, D), F32)
        wr = wr.at[0:MOE_GROUPS].set(moe_w_group[l].astype(F32).T).at[SUBLANES:].set(moe_w_expert[l].astype(F32).T)
        br = jnp.zeros((SUBLANES + MOE_EXPERTS, 1), F32)
        br = br.at[0:MOE_GROUPS, 0].set(moe_b_group[l].astype(F32)).at[SUBLANES:, 0].set(moe_b_expert[l].astype(F32))
        wr_hi = wr.astype(BF16)
        wr_lo = (wr - wr_hi.astype(F32)).astype(BF16)
        x1, h2, route_w, route_i = _merge(xcur, h, y_ret, y_hg, y_ssd, cols[11], w_out[l].astype(BF16), mod[l],
                                          norm_ffn_w[l].astype(F32), wr_hi, wr_lo, br, ts_mix)
        plan = _dispatch_plan(route_i, route_w, T)
        y2 = _experts(h2.reshape(T, D), *plan, moe_w1[l].astype(BF16), moe_w3[l].astype(BF16),
                      moe_w2[l].astype(BF16))
        xcur = _combine(x1, y2, mod[l], ts_el)
    return _final_norm(xcur, final_norm_w.astype(F32), ts_el).astype(x.dtype)
```

```python
import functools

import jax
import jax.numpy as jnp
from jax import lax
from jax.experimental import pallas as pl
from jax.experimental.pallas import tpu as pltpu

F32 = jnp.float32
BF16 = jnp.bfloat16

EPS = 1e-6
ROPE_BASE = 10000.0
RET_HEADS = 4
RET_DK = 128
RET_DV = 256
HG_HEADS = 8
HG_DK = 128
SSD_HEADS = 32
SSD_P = 64
SSD_GROUPS = 4
SSD_N = 128
SSD_HPG = SSD_HEADS // SSD_GROUPS
SSD_CONV = 4
MOE_GROUPS = 4
MOE_PER_GROUP = 8
MOE_EXPERTS = 32
MOE_BLOCK = 256
CHUNK = 128
LANES = 128
SUBLANES = 8
VMEM_LIMIT = 56 * 1024 * 1024
HG_LEVELS = (64, 32, 16, 8, 4, 2, 1)


def _params(n_axes, vmem=VMEM_LIMIT):
    return pltpu.CompilerParams(dimension_semantics=("arbitrary",) * n_axes,
                                vmem_limit_bytes=vmem)


def _const_spec(shape):
    nd = len(shape)
    return pl.BlockSpec(shape, lambda *_: (0,) * nd, pipeline_mode=pl.Buffered(1))


def _sigmoid(x):
    return 1.0 / (1.0 + jnp.exp(-x))


def _silu(x):
    return x * _sigmoid(x)


def _dot(a, b):
    return jnp.dot(a, b, preferred_element_type=F32)


def _dot_nt(a, b):
    return lax.dot_general(a, b, (((1,), (1,)), ((), ())), preferred_element_type=F32)


def _dot_tn(a, b):
    return lax.dot_general(a, b, (((0,), (0,)), ((), ())), preferred_element_type=F32)


def _split3(x):
    hi = x.astype(BF16)
    r1 = x - hi.astype(F32)
    mid = r1.astype(BF16)
    lo = (r1 - mid.astype(F32)).astype(BF16)
    return hi, mid, lo


def _dot3(m_bf16, x_f32):
    hi, mid, lo = _split3(x_f32)
    return _dot(m_bf16, hi) + _dot(m_bf16, mid) + _dot(m_bf16, lo)


def _ada_kernel(c_ref, w_ref, b_ref, o_ref):
    c = c_ref[...]
    o_ref[...] = _dot(_silu(c).astype(BF16), w_ref[...].astype(BF16)) + b_ref[...]


def _adaln(c, w_ada, b_ada):
    L, D, N = w_ada.shape
    B = c.shape[0]
    cp = jnp.zeros((SUBLANES, D), F32).at[:B].set(c.astype(F32))
    tn = 1536
    out = pl.pallas_call(
        _ada_kernel,
        grid=(L, N // tn),
        in_specs=[pl.BlockSpec((SUBLANES, D), lambda l, n: (0, 0)),
                  pl.BlockSpec((None, D, tn), lambda l, n: (l, 0, n)),
                  pl.BlockSpec((None, 1, tn), lambda l, n: (l, 0, n))],
        out_specs=pl.BlockSpec((None, SUBLANES, tn), lambda l, n: (l, 0, n)),
        out_shape=jax.ShapeDtypeStruct((L, SUBLANES, N), F32),
        compiler_params=_params(2),
        name="adaln",
    )(cp, w_ada, b_ada.reshape(L, 1, N))
    return out[:, :B].reshape(L, B, 6, D)


def _rope_kernel(pos_ref, invf_ref, sign_ref, cos_ref, sin_ref):
    ang = pos_ref[...].astype(F32) * invf_ref[...]
    cos_ref[...] = jnp.cos(ang)
    sin_ref[...] = jnp.sin(ang) * sign_ref[...]


def _rope_tables(positions, ts):
    B, S = positions.shape
    half = RET_DK // 2
    inv_freq = ROPE_BASE ** (-jnp.linspace(0.0, 1.0, half, dtype=F32))
    invf = jnp.concatenate([inv_freq, inv_freq]).reshape(1, RET_DK)
    sign = jnp.concatenate([-jnp.ones((half,), F32), jnp.ones((half,), F32)]).reshape(1, RET_DK)
    spec = pl.BlockSpec((None, ts, RET_DK), lambda b, s: (b, s, 0))
    return pl.pallas_call(
        _rope_kernel,
        grid=(B, S // ts),
        in_specs=[pl.BlockSpec((None, ts, 1), lambda b, s: (b, s, 0)),
                  pl.BlockSpec((1, RET_DK), lambda b, s: (0, 0)),
                  pl.BlockSpec((1, RET_DK), lambda b, s: (0, 0))],
        out_specs=[spec, spec],
        out_shape=[jax.ShapeDtypeStruct((B, S, RET_DK), F32)] * 2,
        compiler_params=_params(2),
        name="rope_tables",
    )(positions.reshape(B, S, 1), invf, sign)


def _rms_mod(x, w, scale, shift):
    y = x * lax.rsqrt(jnp.mean(x * x, axis=-1, keepdims=True) + EPS)
    return (y * w) * (1.0 + scale) + shift


def _prenorm_kernel(x_ref, w_ref, mod_ref, o_ref):
    o_ref[...] = _rms_mod(x_ref[...], w_ref[...], mod_ref[1:2, :], mod_ref[0:1, :]).astype(o_ref.dtype)


def _prenorm(x, w, mod_l, ts):
    B, S, D = x.shape
    return pl.pallas_call(
        _prenorm_kernel,
        grid=(B, S // ts),
        in_specs=[pl.BlockSpec((None, ts, D), lambda b, s: (b, s, 0)),
                  pl.BlockSpec((1, D), lambda b, s: (0, 0)),
                  pl.BlockSpec((None, 6, D), lambda b, s: (b, 0, 0))],
        out_specs=pl.BlockSpec((None, ts, D), lambda b, s: (b, s, 0)),
        out_shape=jax.ShapeDtypeStruct((B, S, D), BF16),
        compiler_params=_params(2),
        name="prenorm",
    )(x, w.reshape(1, D), mod_l)


def _ret_kernel(h_ref, cos_ref, sin_ref, wq_ref, wk_ref, wv_ref, wg_ref, wo_ref,
                di_ref, dq_ref, dk_ref, dc_ref, o_ref,
                st_ref, q_s, k_s, v_s, g_s):
    ts = h_ref.shape[0]

    @pl.when(pl.program_id(1) == 0)
    def _():
        st_ref[...] = jnp.zeros_like(st_ref)

    h = h_ref[...]
    cos = cos_ref[...]
    sin = sin_ref[...]
    q = _dot(h, wq_ref[...])
    k = _dot(h, wk_ref[...])
    scale = RET_DK ** -0.5
    for hh in range(RET_HEADS):
        sl = slice(hh * RET_DK, (hh + 1) * RET_DK)
        qh = q[:, sl]
        kh = k[:, sl]
        q_s[:, sl] = (qh * cos + pltpu.roll(qh, RET_DK // 2, 1) * sin).astype(BF16)
        k_s[:, sl] = (kh * cos + pltpu.roll(kh, RET_DK // 2, 1) * sin) * scale
    v_s[...] = _dot(h, wv_ref[...]).astype(BF16)
    g_s[...] = _silu(_dot(h, wg_ref[...]))

    def chunk(c, carry):
        rows = pl.ds(pl.multiple_of(c * CHUNK, CHUNK), CHUNK)
        for hh in range(RET_HEADS):
            ksl = slice(hh * RET_DK, (hh + 1) * RET_DK)
            vsl = slice(hh * RET_DV, (hh + 1) * RET_DV)
            qc = q_s[rows, ksl]
            kc = k_s[rows, ksl]
            vc = v_s[rows, vsl]
            st = st_ref[hh]
            scores = _dot_nt(qc, kc.astype(BF16)) * di_ref[hh]
            o = _dot(scores.astype(BF16), vc) + _dot(qc, st.astype(BF16)) * dq_ref[hh]
            st_ref[hh] = st * dc_ref[hh] + _dot_tn((kc * dk_ref[hh]).astype(BF16), vc)
            mu = jnp.mean(o, axis=-1, keepdims=True)
            d = o - mu
            var = jnp.mean(d * d, axis=-1, keepdims=True)
            g_s[rows, vsl] = g_s[rows, vsl] * (d * lax.rsqrt(var + EPS))
        return carry

    lax.fori_loop(0, ts // CHUNK, chunk, 0)
    o_ref[...] = _dot(g_s[...].astype(BF16), wo_ref[...]).astype(o_ref.dtype)


def _retention(h, cos, sin, wq, wk, wv, wg, wo, ts):
    B, S, D = h.shape
    H, C = RET_HEADS, CHUNK
    log_gamma = jnp.log(1.0 - 2.0 ** (-5.0 - jnp.arange(H, dtype=F32)))
    idx = jnp.arange(C, dtype=F32)
    rel = idx[:, None] - idx[None, :]
    di = jnp.where(rel >= 0.0, jnp.exp(log_gamma[:, None, None] * jnp.maximum(rel, 0.0)), 0.0)
    dq = jnp.broadcast_to(jnp.exp(log_gamma[:, None] * (idx[None, :] + 1.0))[:, :, None], (H, C, RET_DV))
    dk = jnp.broadcast_to(jnp.exp(log_gamma[:, None] * (C - 1.0 - idx[None, :]))[:, :, None], (H, C, RET_DK))
    dc = jnp.broadcast_to(jnp.exp(log_gamma * C)[:, None, None], (H, 1, RET_DV))
    qk, vw = H * RET_DK, H * RET_DV
    tile = lambda w: pl.BlockSpec((None, ts, w), lambda b, s: (b, s, 0))
    return pl.pallas_call(
        _ret_kernel,
        grid=(B, S // ts),
        in_specs=[tile(D), tile(RET_DK), tile(RET_DK),
                  _const_spec((D, qk)), _const_spec((D, qk)), _const_spec((D, vw)), _const_spec((D, vw)),
                  _const_spec((vw, D)),
                  _const_spec((H, C, C)), _const_spec((H, C, RET_DV)), _const_spec((H, C, RET_DK)),
                  _const_spec((H, 1, RET_DV))],
        out_specs=tile(D),
        out_shape=jax.ShapeDtypeStruct((B, S, D), BF16),
        scratch_shapes=[pltpu.VMEM((H, RET_DK, RET_DV), F32),
                        pltpu.VMEM((ts, qk), BF16), pltpu.VMEM((ts, qk), F32),
                        pltpu.VMEM((ts, vw), BF16), pltpu.VMEM((ts, vw), F32)],
        compiler_params=_params(2),
        name="retention",
    )(h, cos, sin, wq, wk, wv, wg, wo, di, dq, dk, dc)


def _hg_level_matrices():
    C = CHUNK
    r = jnp.arange(C)
    tril = (r[:, None] >= r[None, :])
    mats = [tril]
    for b in (4, 2):
        mid = (r // (2 * b)) * (2 * b) + b
        odd = (r & b) != 0
        tau = r[None, :]
        m_odd = odd[:, None] & (tau >= mid[:, None]) & (tau <= r[:, None])
        m_even = (~odd)[:, None] & (tau > r[:, None]) & (tau < mid[:, None])
        mats.append(m_odd | m_even)
    stack = jnp.concatenate(mats, axis=0).astype(BF16)
    masks = []
    for b in HG_LEVELS:
        masks.append((r[:, None] // (2 * b)) == (r[None, :] // (2 * b)))
    masks.append(r[:, None] == r[None, :])
    return stack, jnp.stack(masks).astype(F32)


def _hg_kernel(h_ref, wq_ref, wf_ref, wi_ref, wg_ref, wo_ref, lb_ref, nw_ref, cm_ref, mk_ref, o_ref,
               st_ref, q_s, k_s, v_s, lf_s, g_s, G_s, a_s):
    ts = h_ref.shape[0]
    C = CHUNK
    W = HG_HEADS * HG_DK

    @pl.when(pl.program_id(1) == 0)
    def _():
        st_ref[...] = jnp.zeros_like(st_ref)

    h = h_ref[...]
    lb = lb_ref[...]
    q_s[...] = _silu(_dot(h, wq_ref[...]))
    hf = _dot(h, wf_ref[...])
    lf_s[...] = jnp.log(lb + (1.0 - lb) * _sigmoid(hf))
    k_s[...] = (1.0 - lb) * _sigmoid(-hf)
    v_s[...] = _dot(h, wi_ref[...]).astype(BF16)
    g_s[...] = _silu(_dot(h, wg_ref[...]))
    odd_row = {b: (lax.broadcasted_iota(jnp.int32, (C, 1), 0) & b) != 0 for b in HG_LEVELS}

    def chunk(c, carry):
        rows = pl.ds(pl.multiple_of(c * C, C), C)
        lf = lf_s[rows, :]
        q = q_s[rows, :]
        k = k_s[rows, :]
        sums = _dot3(cm_ref[...], lf)
        G = sums[0:C]
        G_s[...] = G
        for li, b in enumerate(HG_LEVELS):
            if b >= 8:
                n = C // (2 * b)
                ref = jnp.concatenate(
                    [jnp.broadcast_to(G_s[j * 2 * b + b - 1:j * 2 * b + b, :], (2 * b, W)) for j in range(n)],
                    axis=0)
                x = jnp.exp(-jnp.abs(G - ref))
                qt = jnp.where(odd_row[b], q * x, 0.0)
                kt = jnp.where(odd_row[b], 0.0, k * x)
            elif b > 1:
                e = sums[C:2 * C] if b == 4 else sums[2 * C:3 * C]
                x = jnp.exp(e)
                qt = jnp.where(odd_row[b], q * x, 0.0)
                kt = jnp.where(odd_row[b], 0.0, k * x)
            else:
                qt = jnp.where(odd_row[b], q * jnp.exp(lf), 0.0)
                kt = jnp.where(odd_row[b], 0.0, k)
            qt = qt.astype(BF16)
            kt = kt.astype(BF16)
            for hh in range(HG_HEADS):
                sl = slice(hh * HG_DK, (hh + 1) * HG_DK)
                a = _dot_nt(qt[:, sl], kt[:, sl])
                if li == 0:
                    a_s[hh] = a
                else:
                    a_s[hh] = a_s[hh] + a * mk_ref[li]
        g_last = G_s[C - 1:C, :]
        qg = (q * jnp.exp(G)).astype(BF16)
        kh = (k * jnp.exp(g_last - G)).astype(BF16)
        qk = q * k
        dec = jnp.exp(g_last)
        eye = mk_ref[len(HG_LEVELS)]
        for hh in range(HG_HEADS):
            sl = slice(hh * HG_DK, (hh + 1) * HG_DK)
            vh = v_s[rows, sl]
            diag = jnp.sum(qk[:, sl], axis=-1, keepdims=True)
            a = (a_s[hh] + diag * eye).astype(BF16)
            st = st_ref[hh]
            o = _dot(a, vh) + _dot_nt(qg[:, sl], st.astype(BF16))
            st_ref[hh] = st * dec[:, sl] + _dot_tn(vh, kh[:, sl])
            on = o * lax.rsqrt(jnp.mean(o * o, axis=-1, keepdims=True) + EPS) * nw_ref[...]
            g_s[rows, sl] = on * g_s[rows, sl]
        return carry

    lax.fori_loop(0, ts // C, chunk, 0)
    o_ref[...] = _dot(g_s[...].astype(BF16), wo_ref[...]).astype(o_ref.dtype)


def _hgrn2(h, wq, wf, wi, wg, wo, lb, norm_w, ts):
    B, S, D = h.shape
    W = HG_HEADS * HG_DK
    C = CHUNK
    cm, mk = _hg_level_matrices()
    tile = lambda w: pl.BlockSpec((None, ts, w), lambda b, s: (b, s, 0))
    return pl.pallas_call(
        _hg_kernel,
        grid=(B, S // ts),
        in_specs=[tile(D)] + [_const_spec((D, W))] * 4 + [_const_spec((W, D)), _const_spec((1, W)),
                  _const_spec((1, HG_DK)), _const_spec(cm.shape), _const_spec(mk.shape)],
        out_specs=tile(D),
        out_shape=jax.ShapeDtypeStruct((B, S, D), BF16),
        scratch_shapes=[pltpu.VMEM((HG_HEADS, HG_DK, HG_DK), F32),
                        pltpu.VMEM((ts, W), F32), pltpu.VMEM((ts, W), F32), pltpu.VMEM((ts, W), BF16),
                        pltpu.VMEM((ts, W), F32), pltpu.VMEM((ts, W), F32),
                        pltpu.VMEM((C, W), F32), pltpu.VMEM((HG_HEADS, C, C), F32)],
        compiler_params=_params(2),
        name="hgrn2",
    )(h, wq, wf, wi, wg, wo, lb.reshape(1, W), norm_w.reshape(1, HG_DK), cm, mk)


def _softplus(x):
    return jnp.maximum(x, 0.0) + jnp.log(1.0 + jnp.exp(-jnp.abs(x)))


def _ssd_kernel(h_ref, wz_ref, wx_ref, wdt_ref, wo_ref, cw_ref, cb_ref, dtb_ref, a_ref, dl_ref, nw_ref,
                tril_ref, ex_ref, o_ref,
                st_ref, xb_s, xc_s, z_s, dt_s, y_s):
    ts = h_ref.shape[0]
    C = CHUNK
    inner = SSD_HEADS * SSD_P
    gw = SSD_GROUPS * SSD_N
    gp = SSD_HPG * SSD_P
    halo = SUBLANES

    @pl.when(pl.program_id(1) == 0)
    def _():
        st_ref[...] = jnp.zeros_like(st_ref)
        xb_s[0:halo, :] = jnp.zeros((halo, xb_s.shape[1]), F32)

    h = h_ref[...]
    z_s[...] = _silu(_dot(h, wz_ref[...]))
    xb_s[halo:halo + ts, :] = _dot(h, wx_ref[...])
    dt_s[...] = _softplus(_dot(h, wdt_ref[...]) + dtb_ref[...])
    conv = cb_ref[...]
    for j in range(SSD_CONV):
        off = halo - (SSD_CONV - 1) + j
        conv = conv + cw_ref[j:j + 1, :] * xb_s[off:off + ts, :]
    xc_s[...] = _silu(conv)
    xb_s[0:halo, :] = xb_s[ts:ts + halo, :]

    causal = lax.broadcasted_iota(jnp.int32, (C, C), 0) >= lax.broadcasted_iota(jnp.int32, (C, C), 1)

    def chunk(c, carry):
        rows = pl.ds(pl.multiple_of(c * C, C), C)
        dt = dt_s[rows, :]
        da = dt * a_ref[...]
        cum = jnp.dot(tril_ref[...], da, preferred_element_type=F32, precision=lax.Precision.HIGHEST)
        cum_t = cum.T
        dt_t = dt.T
        ecum = jnp.exp(cum)
        cum_last = cum[C - 1:C, :]
        wfac = dt * jnp.exp(cum_last - cum)
        w_hi = wfac.astype(BF16)
        w_lo = (wfac - w_hi.astype(F32)).astype(BF16)
        wexp = _dot(w_hi, ex_ref[...]) + _dot(w_lo, ex_ref[...])
        dl = jnp.broadcast_to(jnp.exp(cum_last), (SUBLANES, LANES))
        d_hi = dl.astype(BF16)
        d_lo = (dl - d_hi.astype(F32)).astype(BF16)
        dexp = (_dot(d_hi, ex_ref[...]) + _dot(d_lo, ex_ref[...]))[0:1, :]
        for g in range(SSD_GROUPS):
            bg = xc_s[rows, inner + g * SSD_N:inner + (g + 1) * SSD_N]
            cg = xc_s[rows, inner + gw + g * SSD_N:inner + gw + (g + 1) * SSD_N]
            bgb = bg.astype(BF16)
            cb = _dot_nt(cg.astype(BF16), bgb)
            st = st_ref[g]
            stb = st.astype(BF16)
            xg = xc_s[rows, g * gp:(g + 1) * gp]
            xgb = xg.astype(BF16)
            for j in range(SSD_HPG):
                hd = g * SSD_HPG + j
                col = cum[:, hd:hd + 1]
                diff = col - cum_t[hd:hd + 1, :]
                lmat = jnp.where(causal, jnp.exp(jnp.where(causal, diff, 0.0)), 0.0)
                w = cb * lmat * dt_t[hd:hd + 1, :]
                lhs = jnp.concatenate([w.astype(BF16), (cg * ecum[:, hd:hd + 1]).astype(BF16)], axis=1)
                psl = slice(j * SSD_P, (j + 1) * SSD_P)
                rhs = jnp.concatenate([xgb[:, psl], stb[:, psl]], axis=0)
                y_s[rows, hd * SSD_P:(hd + 1) * SSD_P] = _dot(lhs, rhs)
            gsl = slice(g * gp, (g + 1) * gp)
            st_ref[g] = st * dexp[:, gsl] + _dot_tn(bgb, (xg * wexp[:, gsl]).astype(BF16))
        xs = xc_s[rows, 0:inner]
        yz = (y_s[rows, :] + dl_ref[...] * xs) * z_s[rows, :]
        gi = inner // SSD_GROUPS
        for g in range(SSD_GROUPS):
            sl = slice(g * gi, (g + 1) * gi)
            seg = yz[:, sl]
            nrm = seg * lax.rsqrt(jnp.mean(seg * seg, axis=-1, keepdims=True) + EPS)
            y_s[rows, sl] = nrm * nw_ref[:, sl]
        return carry

    lax.fori_loop(0, ts // C, chunk, 0)
    o_ref[...] = _dot(y_s[...].astype(BF16), wo_ref[...]).astype(o_ref.dtype)


def _ssd(h, wz, wx, wdt, wo, conv_w, conv_b, dt_bias, a_log, d_skip, norm_w, ts):
    B, S, D = h.shape
    C = CHUNK
    inner = SSD_HEADS * SSD_P
    cdim = inner + 2 * SSD_GROUPS * SSD_N
    pad = LANES - SSD_HEADS
    wdt_p = jnp.pad(wdt, ((0, 0), (0, pad)))
    dtb = jnp.pad(dt_bias.astype(F32), (0, pad)).reshape(1, LANES)
    a = jnp.pad(-jnp.exp(a_log.astype(F32)), (0, pad)).reshape(1, LANES)
    dl = jnp.repeat(d_skip.astype(F32), SSD_P).reshape(1, inner)
    r = jnp.arange(C)
    tril = (r[:, None] >= r[None, :]).astype(F32)
    ex = (jnp.arange(LANES)[:, None] == (jnp.arange(inner)[None, :] // SSD_P)).astype(BF16)
    tile = lambda w: pl.BlockSpec((None, ts, w), lambda b, s: (b, s, 0))
    return pl.pallas_call(
        _ssd_kernel,
        grid=(B, S // ts),
        in_specs=[tile(D), _const_spec((D, inner)), _const_spec((D, cdim)), _const_spec((D, LANES)),
                  _const_spec((inner, D)), _const_spec((SSD_CONV, cdim)), _const_spec((1, cdim)),
                  _const_spec((1, LANES)), _const_spec((1, LANES)), _const_spec((1, inner)),
                  _const_spec((1, inner)), _const_spec((C, C)), _const_spec((LANES, inner))],
        out_specs=tile(D),
        out_shape=jax.ShapeDtypeStruct((B, S, D), BF16),
        scratch_shapes=[pltpu.VMEM((SSD_GROUPS, SSD_N, SSD_HPG * SSD_P), F32),
                        pltpu.VMEM((ts + SUBLANES, cdim), F32), pltpu.VMEM((ts, cdim), F32),
                        pltpu.VMEM((ts, inner), F32), pltpu.VMEM((ts, LANES), F32),
                        pltpu.VMEM((ts, inner), F32)],
        compiler_params=_params(2),
        name="ssd",
    )(h, wz, wx, wdt_p, wo, conv_w.astype(F32), conv_b.astype(F32).reshape(1, cdim), dtb, a, dl,
      norm_w.astype(F32).reshape(1, inner), tril, ex)


def _merge_kernel(x_ref, h_ref, yr_ref, yh_ref, ys_ref, wgl_ref, wout_ref, mod_ref, nw_ref,
                  wr_hi_ref, wr_lo_ref, br_ref, ut_ref, x1_ref, h2_ref, rf_ref, ri_ref, cnt_ref, run_ref):
    D = x_ref.shape[1]
    ts = x_ref.shape[0]

    @pl.when((pl.program_id(0) == 0) & (pl.program_id(1) == 0))
    def _():
        run_ref[...] = jnp.zeros_like(run_ref)

    h = h_ref[...]
    gates = _sigmoid(_dot(h, wgl_ref[...]))
    m = (gates[:, 0:D] * yr_ref[...].astype(F32) + gates[:, D:2 * D] * yh_ref[...].astype(F32)
         + gates[:, 2 * D:3 * D] * ys_ref[...].astype(F32))
    y = _dot(m.astype(BF16), wout_ref[...])
    x1 = x_ref[...] + mod_ref[2:3, :] * y
    x1_ref[...] = x1
    h2 = _rms_mod(x1, nw_ref[...], mod_ref[4:5, :], mod_ref[3:4, :])
    h2_ref[...] = h2
    h_hi = h2.astype(BF16)
    h_lo = (h2 - h_hi.astype(F32)).astype(BF16)
    lg = (_dot_nt(wr_hi_ref[...], h_hi) + _dot_nt(wr_hi_ref[...], h_lo) + _dot_nt(wr_lo_ref[...], h_hi)
          + br_ref[...])
    gl = lg[0:MOE_GROUPS]
    gmax = jnp.max(gl, axis=0, keepdims=True)
    gi = lax.broadcasted_iota(jnp.int32, gl.shape, 0)
    gtop = jnp.min(jnp.where(gl == gmax, gi, MOE_GROUPS), axis=0, keepdims=True)
    p_g = 1.0 / jnp.sum(jnp.exp(gl - gmax), axis=0, keepdims=True)
    el = jnp.zeros((MOE_PER_GROUP, ts), F32)
    for g in range(MOE_GROUPS):
        lo = SUBLANES + g * MOE_PER_GROUP
        el = jnp.where(gtop == g, lg[lo:lo + MOE_PER_GROUP], el)
    ei = lax.broadcasted_iota(jnp.int32, el.shape, 0)
    m1 = jnp.max(el, axis=0, keepdims=True)
    i1 = jnp.min(jnp.where(el == m1, ei, MOE_PER_GROUP), axis=0, keepdims=True)
    el2 = jnp.where(ei == i1, -jnp.inf, el)
    m2 = jnp.max(el2, axis=0, keepdims=True)
    i2 = jnp.min(jnp.where(el2 == m2, ei, MOE_PER_GROUP), axis=0, keepdims=True)
    e21 = jnp.exp(m2 - m1)
    den = 1.0 + e21
    w1 = (1.0 / den) * p_g
    w2 = (e21 / den) * p_g
    zf = jnp.zeros((SUBLANES - 2, ts), F32)
    rf_ref[...] = jnp.concatenate([w1, w2, zf], axis=0)
    e1 = gtop * MOE_PER_GROUP + i1
    e2 = gtop * MOE_PER_GROUP + i2
    ex = lax.broadcasted_iota(jnp.int32, (MOE_EXPERTS, ts), 0)
    oh1 = jnp.where(ex == e1, 1.0, 0.0)
    oh2 = jnp.where(ex == e2, 1.0, 0.0)
    p1 = _dot(oh1.astype(BF16), ut_ref[...])
    p2 = _dot(oh2.astype(BF16), ut_ref[...])
    c1 = p1[:, ts - 1:ts]
    c2 = p2[:, ts - 1:ts]
    run = run_ref[...]
    base = run[:, 0:1]
    r1 = jnp.sum(oh1 * (p1 - 1.0 + base), axis=0, keepdims=True)
    r2 = jnp.sum(oh2 * (p2 - 1.0 + base + c1), axis=0, keepdims=True)
    run = run + (c1 + c2)
    run_ref[...] = run
    cnt_ref[...] = run
    ri_ref[...] = jnp.concatenate([e1, e2, r1.astype(jnp.int32), r2.astype(jnp.int32),
                                   jnp.zeros((SUBLANES - 4, ts), jnp.int32)], axis=0)


def _merge(x, h, y_ret, y_hg, y_ssd, wgl, wout, mod_l, norm_w, wr_hi, wr_lo, br, ts):
    B, S, D = x.shape
    ns = S // ts
    R = wr_hi.shape[0]
    tile = pl.BlockSpec((None, ts, D), lambda b, s: (b, s, 0))
    rt = pl.BlockSpec((SUBLANES, ts), lambda b, s: (0, b * ns + s))
    r = jnp.arange(ts)
    ut = (r[:, None] <= r[None, :]).astype(BF16)
    return pl.pallas_call(
        _merge_kernel,
        grid=(B, ns),
        in_specs=[tile, tile, tile, tile, tile, _const_spec((D, 3 * D)), _const_spec((D, D)),
                  pl.BlockSpec((None, 6, D), lambda b, s: (b, 0, 0)), _const_spec((1, D)),
                  _const_spec((R, D)), _const_spec((R, D)), _const_spec((R, 1)), _const_spec((ts, ts))],
        out_specs=[tile, tile, rt, rt, pl.BlockSpec((MOE_EXPERTS, LANES), lambda b, s: (0, 0))],
        out_shape=[jax.ShapeDtypeStruct((B, S, D), F32), jax.ShapeDtypeStruct((B, S, D), F32),
                   jax.ShapeDtypeStruct((SUBLANES, B * S), F32), jax.ShapeDtypeStruct((SUBLANES, B * S), jnp.int32),
                   jax.ShapeDtypeStruct((MOE_EXPERTS, LANES), F32)],
        scratch_shapes=[pltpu.VMEM((MOE_EXPERTS, LANES), F32)],
        compiler_params=_params(2),
        name="merge_router",
    )(x, h, y_ret, y_hg, y_ssd, wgl, wout, mod_l, norm_w.reshape(1, D), wr_hi, wr_lo, br, ut)


def _row_copy(src, src_row, dst, dst_row, sem):
    return pltpu.make_async_copy(src.at[pl.ds(src_row, 1), :], dst.at[pl.ds(dst_row, 1), :], sem)


DISPATCH_TILE = 1024
DMA_UNROLL = 8


def _dispatch_kernel(pend_ref, plen_ref, nact_ref, dest_ref, h2_hbm, xs_hbm, zbuf, sem, zsem):
    i = pl.program_id(0)
    n = pl.num_programs(0)
    td = dest_ref.shape[2]
    t0 = i * td
    BLK = MOE_BLOCK
    nb = xs_hbm.shape[0] // BLK

    def zero_fill(wait):
        def go(start, size):
            cp = pltpu.make_async_copy(zbuf.at[pl.ds(0, size), :], xs_hbm.at[pl.ds(start, size), :], zsem)
            if wait:
                cp.wait()
            else:
                cp.start()

        for e in range(MOE_EXPERTS):
            ln = plen_ref[e]
            end = pend_ref[e]
            size = BLK // 2
            while size >= SUBLANES:
                @pl.when((ln & size) != 0)
                def _(size=size, ln=ln, end=end):
                    go(pl.multiple_of(end - (ln & ~(size - 1)), size), size)
                size //= 2
            for j in range(SUBLANES - 1):
                @pl.when(j < (ln & (SUBLANES - 1)))
                def _(j=j, ln=ln, end=end):
                    go(end - ln + j, 1)
        for j in range(MOE_EXPERTS):
            @pl.when(nact_ref[0] + j < nb)
            def _(j=j):
                go(pl.multiple_of((nact_ref[0] + j) * BLK, BLK), BLK)

    @pl.when(i == 0)
    def _():
        zbuf[...] = jnp.zeros_like(zbuf)
        zero_fill(False)

    def issue(j, c):
        for u in range(DMA_UNROLL):
            r = j * DMA_UNROLL + u
            for k in range(2):
                _row_copy(h2_hbm, t0 + r, xs_hbm, dest_ref[0, k, r], sem).start()
        return c

    lax.fori_loop(0, td // DMA_UNROLL, issue, 0)

    def wait_tile():
        for k in range(2):
            pltpu.make_async_copy(h2_hbm.at[pl.ds(0, td), :], xs_hbm.at[pl.ds(0, td), :], sem).wait()

    @pl.when(i > 0)
    def _():
        wait_tile()

    @pl.when(i == n - 1)
    def _():
        wait_tile()

    @pl.when(i == 0)
    def _():
        zero_fill(True)


def _dispatch(h2, dest, pad_end, pad_len, nactive, nrows):
    T, D = h2.shape
    td = min(DISPATCH_TILE, T)
    nt = T // td
    dest3 = dest.reshape(2, nt, td).transpose(1, 0, 2)
    grid_spec = pltpu.PrefetchScalarGridSpec(
        num_scalar_prefetch=3,
        grid=(nt,),
        in_specs=[pl.BlockSpec((1, 2, td), lambda i, *_: (i, 0, 0), memory_space=pltpu.SMEM),
                  pl.BlockSpec(memory_space=pl.ANY)],
        out_specs=pl.BlockSpec(memory_space=pl.ANY),
        scratch_shapes=[pltpu.VMEM((MOE_BLOCK, D), F32), pltpu.SemaphoreType.DMA(()),
                        pltpu.SemaphoreType.DMA(())])
    return pl.pallas_call(
        _dispatch_kernel,
        grid_spec=grid_spec,
        out_shape=jax.ShapeDtypeStruct((nrows, D), F32),
        compiler_params=_params(1),
        name="dispatch",
    )(pad_end, pad_len, nactive, dest3, h2)


def _expert_kernel(be_ref, na_ref, x_ref, w1_ref, w3_ref, w2_ref, y_ref):
    i = pl.program_id(0)

    @pl.when(i < na_ref[0])
    def _():
        x = x_ref[...].astype(BF16)
        a = _dot(x, w1_ref[...])
        b = _dot(x, w3_ref[...])
        y_ref[...] = _dot((_silu(a) * b).astype(BF16), w2_ref[...])

    @pl.when(i >= na_ref[0])
    def _():
        y_ref[...] = jnp.zeros_like(y_ref)


def _experts(xs, block_e, nactive, w1, w3, w2):
    nrows, D = xs.shape
    BLK = MOE_BLOCK
    nb = nrows // BLK
    ff = w1.shape[2]
    wsel = lambda i, be, na: (be[i], 0, 0)
    grid_spec = pltpu.PrefetchScalarGridSpec(
        num_scalar_prefetch=2,
        grid=(nb,),
        in_specs=[pl.BlockSpec((BLK, D), lambda i, be, na: (jnp.minimum(i, na[0] - 1), 0)),
                  pl.BlockSpec((None, D, ff), wsel), pl.BlockSpec((None, D, ff), wsel),
                  pl.BlockSpec((None, ff, D), wsel)],
        out_specs=pl.BlockSpec((BLK, D), lambda i, be, na: (i, 0)))
    return pl.pallas_call(
        _expert_kernel,
        grid_spec=grid_spec,
        out_shape=jax.ShapeDtypeStruct((nrows, D), F32),
        compiler_params=_params(1),
        name="experts",
    )(block_e, nactive, xs, w1, w3, w2)


def _dispatch_plan(route_i, counts_f, T):
    BLK = MOE_BLOCK
    nb = -(-(2 * T) // BLK) + MOE_EXPERTS
    counts = counts_f[:, 0].astype(jnp.int32)
    padded = ((counts + BLK - 1) // BLK) * BLK
    pad_ends = jnp.cumsum(padded)
    pad_starts = pad_ends - padded
    e = route_i[0:2]
    rank = route_i[2:4]
    sel = e[:, :, None] == jnp.arange(MOE_EXPERTS, dtype=jnp.int32)[None, None, :]
    dest = rank + jnp.sum(jnp.where(sel, pad_starts[None, None, :], 0), axis=-1)
    nactive = (pad_ends[-1] // BLK).astype(jnp.int32)
    blk_id = jnp.minimum(jnp.arange(nb, dtype=jnp.int32), nactive - 1)
    block_e = jnp.sum((blk_id[:, None] * BLK >= pad_ends[None, :]).astype(jnp.int32), axis=1)
    block_e = jnp.minimum(block_e, MOE_EXPERTS - 1)
    return dest, block_e, pad_ends, padded - counts, nactive.reshape(1), nb * BLK


COMBINE_TILE = 256


def _combine_kernel(final, dcur_ref, dnxt_ref, x_ref, rw_ref, mod_ref, nw_ref, modn_ref, ys_hbm, *rest):
    if final:
        o_ref, gbuf, sem = rest
    else:
        x2_ref, h_ref, gbuf, sem = rest
    i = pl.program_id(0)
    n = pl.num_programs(0)
    ts = x_ref.shape[0]
    slot = i % 2

    def issue(dref, sl):
        def body(j, c):
            for u in range(DMA_UNROLL):
                r = j * DMA_UNROLL + u
                for k in range(2):
                    _row_copy(ys_hbm, dref[0, k, r], gbuf.at[sl, k], r, sem.at[sl]).start()
            return c
        lax.fori_loop(0, ts // DMA_UNROLL, body, 0)

    @pl.when(i == 0)
    def _():
        issue(dcur_ref, 0)

    @pl.when(i + 1 < n)
    def _():
        issue(dnxt_ref, 1 - slot)

    for k in range(2):
        pltpu.make_async_copy(ys_hbm.at[pl.ds(0, ts), :], gbuf.at[slot, k], sem.at[slot]).wait()
    y = rw_ref[:, 0:1] * gbuf[slot, 0] + rw_ref[:, 1:2] * gbuf[slot, 1]
    x2 = x_ref[...] + mod_ref[5:6, :] * y
    if final:
        o_ref[...] = (x2 * lax.rsqrt(jnp.mean(x2 * x2, axis=-1, keepdims=True) + EPS) * nw_ref[...]).astype(o_ref.dtype)
    else:
        x2_ref[...] = x2
        h_ref[...] = _rms_mod(x2, nw_ref[...], modn_ref[1:2, :], modn_ref[0:1, :]).astype(h_ref.dtype)


def _combine(x1, ys, dest, rw, mod_l, norm_w, mod_next, final, out_dtype):
    B, S, D = x1.shape
    T = B * S
    ts = min(COMBINE_TILE, S)
    nsb = S // ts
    nt = T // ts
    dest3 = dest.reshape(2, nt, ts).transpose(1, 0, 2)
    tile = pl.BlockSpec((ts, D), lambda i: (i, 0))
    modspec = pl.BlockSpec((None, 6, D), lambda i: (i // nsb, 0, 0))
    if final:
        out_specs = tile
        out_shape = jax.ShapeDtypeStruct((T, D), out_dtype)
    else:
        out_specs = [tile, tile]
        out_shape = [jax.ShapeDtypeStruct((T, D), F32), jax.ShapeDtypeStruct((T, D), BF16)]
    out = pl.pallas_call(
        functools.partial(_combine_kernel, final),
        grid=(nt,),
        in_specs=[pl.BlockSpec((1, 2, ts), lambda i: (i, 0, 0), memory_space=pltpu.SMEM),
                  pl.BlockSpec((1, 2, ts), lambda i: (jnp.minimum(i + 1, nt - 1), 0, 0), memory_space=pltpu.SMEM),
                  tile, pl.BlockSpec((ts, SUBLANES), lambda i: (i, 0)), modspec,
                  pl.BlockSpec((1, D), lambda i: (0, 0)), modspec,
                  pl.BlockSpec(memory_space=pl.ANY)],
        out_specs=out_specs,
        out_shape=out_shape,
        scratch_shapes=[pltpu.VMEM((2, 2, ts, D), F32), pltpu.SemaphoreType.DMA((2,))],
        compiler_params=_params(1),
        name="combine_final" if final else "combine_prenorm",
    )(dest3, dest3, x1.reshape(T, D), rw, mod_l, norm_w.reshape(1, D), mod_next, ys)
    if final:
        return out.reshape(B, S, D)
    return out[0].reshape(B, S, D), out[1].reshape(B, S, D)


def _seq_tile(S, want):
    ts = min(want, S)
    assert S % ts == 0 and ts % CHUNK == 0
    return ts


def kernel(x, c, positions, w_ada, b_ada, norm_mix_w, w_in, ssd_conv_w, ssd_conv_b, ssd_dt_bias, ssd_a_log,
           ssd_d, ssd_norm_w, hg_lb, hg_norm_w, w_ret_o, w_hg_o, w_ssd_o, w_out, norm_ffn_w, moe_w_group,
           moe_b_group, moe_w_expert, moe_b_expert, moe_w1, moe_w3, moe_w2, final_norm_w):
    B, S, D = x.shape
    L = w_ada.shape[0]
    T = B * S
    ts_el = _seq_tile(S, 512)
    ts_mix = _seq_tile(S, 256)

    mod = _adaln(c, w_ada, b_ada)
    cos, sin = _rope_tables(positions, ts_el)
    lb_soft = jax.nn.softmax(hg_lb.astype(F32), axis=0)
    lb_all = jnp.cumsum(lb_soft, axis=0) - lb_soft[0]

    qk = RET_HEADS * RET_DK
    vw = RET_HEADS * RET_DV
    hgw = HG_HEADS * HG_DK
    inner = SSD_HEADS * SSD_P
    cdim = inner + 2 * SSD_GROUPS * SSD_N
    sizes = (qk, qk, vw, vw, hgw, hgw, hgw, hgw, inner, cdim, SSD_HEADS, 3 * D)
    offs = [0]
    for n in sizes:
        offs.append(offs[-1] + n)

    xcur = x.astype(F32)
    h = _prenorm(xcur, norm_mix_w[0].astype(F32), mod[0], ts_el)
    for l in range(L):
        wl = w_in[l].astype(BF16)
        cols = [wl[:, offs[i]:offs[i + 1]] for i in range(len(sizes))]
        y_ret = _retention(h, cos, sin, cols[0], cols[1], cols[2], cols[3], w_ret_o[l].astype(BF16), ts_mix)
        y_hg = _hgrn2(h, cols[4], cols[5], cols[6], cols[7], w_hg_o[l].astype(BF16), lb_all[l],
                      hg_norm_w[l].astype(F32), ts_mix)
        y_ssd = _ssd(h, cols[8], cols[9], cols[10], w_ssd_o[l].astype(BF16), ssd_conv_w[l], ssd_conv_b[l],
                     ssd_dt_bias[l], ssd_a_log[l], ssd_d[l], ssd_norm_w[l], ts_mix)
        wr = jnp.zeros((SUBLANES + MOE_EXPERTS, D), F32)
        wr = wr.at[0:MOE_GROUPS].set(moe_w_group[l].astype(F32).T).at[SUBLANES:].set(moe_w_expert[l].astype(F32).T)
        br = jnp.zeros((SUBLANES + MOE_EXPERTS, 1), F32)
        br = br.at[0:MOE_GROUPS, 0].set(moe_b_group[l].astype(F32)).at[SUBLANES:, 0].set(moe_b_expert[l].astype(F32))
        wr_hi = wr.astype(BF16)
        wr_lo = (wr - wr_hi.astype(F32)).astype(BF16)
        x1, h2, route_w, route_i, counts = _merge(xcur, h, y_ret, y_hg, y_ssd, cols[11], w_out[l].astype(BF16),
                                                  mod[l], norm_ffn_w[l].astype(F32), wr_hi, wr_lo, br, ts_mix)
        dest, block_e, pad_end, pad_len, nactive, nrows = _dispatch_plan(route_i, counts, T)
        xs = _dispatch(h2.reshape(T, D), dest, pad_end, pad_len, nactive, nrows)
        ys = _experts(xs, block_e, nactive, moe_w1[l].astype(BF16), moe_w3[l].astype(BF16),
                      moe_w2[l].astype(BF16))
        rw = route_w.T
        if l + 1 < L:
            xcur, h = _combine(x1, ys, dest, rw, mod[l], norm_mix_w[l + 1].astype(F32), mod[l + 1], False, x.dtype)
        else:
            out = _combine(x1, ys, dest, rw, mod[l], final_norm_w.astype(F32), mod[l], True, x.dtype)
    return out
```

```python
import functools

import jax
import jax.numpy as jnp
from jax import lax
from jax.experimental import pallas as pl
from jax.experimental.pallas import tpu as pltpu

F32 = jnp.float32
BF16 = jnp.bfloat16

EPS = 1e-6
ROPE_BASE = 10000.0
RET_HEADS = 4
RET_DK = 128
RET_DV = 256
HG_HEADS = 8
HG_DK = 128
SSD_HEADS = 32
SSD_P = 64
SSD_GROUPS = 4
SSD_N = 128
SSD_HPG = SSD_HEADS // SSD_GROUPS
SSD_CONV = 4
MOE_GROUPS = 4
MOE_PER_GROUP = 8
MOE_EXPERTS = 32
MOE_BLOCK = 256
CHUNK = 128
LANES = 128
SUBLANES = 8
VMEM_LIMIT = 56 * 1024 * 1024
HG_LEVELS = (64, 32, 16, 8, 4, 2, 1)
LOG2E = 1.4426950408889634
NEG_BIG = -1e30


def _params(n_axes, vmem=VMEM_LIMIT):
    return pltpu.CompilerParams(dimension_semantics=("arbitrary",) * n_axes,
                                vmem_limit_bytes=vmem)


def _const_spec(shape):
    nd = len(shape)
    return pl.BlockSpec(shape, lambda *_: (0,) * nd, pipeline_mode=pl.Buffered(1))


def _sigmoid(x):
    return 1.0 / (1.0 + jnp.exp(-x))


def _silu(x):
    return x * _sigmoid(x)


def _dot(a, b):
    return jnp.dot(a, b, preferred_element_type=F32)


def _dot_nt(a, b):
    return lax.dot_general(a, b, (((1,), (1,)), ((), ())), preferred_element_type=F32)


def _dot_tn(a, b):
    return lax.dot_general(a, b, (((0,), (0,)), ((), ())), preferred_element_type=F32)


def _split3(x):
    hi = x.astype(BF16)
    r1 = x - hi.astype(F32)
    mid = r1.astype(BF16)
    lo = (r1 - mid.astype(F32)).astype(BF16)
    return hi, mid, lo


def _dot3(m_bf16, x_f32):
    hi, mid, lo = _split3(x_f32)
    return _dot(m_bf16, hi) + _dot(m_bf16, mid) + _dot(m_bf16, lo)


def _ada_kernel(c_ref, w_ref, b_ref, o_ref):
    c = c_ref[...]
    o_ref[...] = _dot(_silu(c).astype(BF16), w_ref[...].astype(BF16)) + b_ref[...]


def _adaln(c, w_ada, b_ada):
    L, D, N = w_ada.shape
    B = c.shape[0]
    cp = jnp.zeros((SUBLANES, D), F32).at[:B].set(c.astype(F32))
    tn = 1536
    out = pl.pallas_call(
        _ada_kernel,
        grid=(L, N // tn),
        in_specs=[pl.BlockSpec((SUBLANES, D), lambda l, n: (0, 0)),
                  pl.BlockSpec((None, D, tn), lambda l, n: (l, 0, n)),
                  pl.BlockSpec((None, 1, tn), lambda l, n: (l, 0, n))],
        out_specs=pl.BlockSpec((None, SUBLANES, tn), lambda l, n: (l, 0, n)),
        out_shape=jax.ShapeDtypeStruct((L, SUBLANES, N), F32),
        compiler_params=_params(2),
        name="adaln",
    )(cp, w_ada, b_ada.reshape(L, 1, N))
    return out[:, :B].reshape(L, B, 6, D)


def _rope_kernel(pos_ref, invf_ref, sign_ref, cos_ref, sin_ref):
    ang = pos_ref[...].astype(F32) * invf_ref[...]
    cos_ref[...] = jnp.cos(ang)
    sin_ref[...] = jnp.sin(ang) * sign_ref[...]


def _rope_tables(positions, ts):
    B, S = positions.shape
    half = RET_DK // 2
    inv_freq = ROPE_BASE ** (-jnp.linspace(0.0, 1.0, half, dtype=F32))
    invf = jnp.concatenate([inv_freq, inv_freq]).reshape(1, RET_DK)
    sign = jnp.concatenate([-jnp.ones((half,), F32), jnp.ones((half,), F32)]).reshape(1, RET_DK)
    spec = pl.BlockSpec((None, ts, RET_DK), lambda b, s: (b, s, 0))
    return pl.pallas_call(
        _rope_kernel,
        grid=(B, S // ts),
        in_specs=[pl.BlockSpec((None, ts, 1), lambda b, s: (b, s, 0)),
                  pl.BlockSpec((1, RET_DK), lambda b, s: (0, 0)),
                  pl.BlockSpec((1, RET_DK), lambda b, s: (0, 0))],
        out_specs=[spec, spec],
        out_shape=[jax.ShapeDtypeStruct((B, S, RET_DK), F32)] * 2,
        compiler_params=_params(2),
        name="rope_tables",
    )(positions.reshape(B, S, 1), invf, sign)


def _rms_mod(x, w, scale, shift):
    y = x * lax.rsqrt(jnp.mean(x * x, axis=-1, keepdims=True) + EPS)
    return (y * w) * (1.0 + scale) + shift


def _prenorm_kernel(x_ref, w_ref, mod_ref, o_ref):
    o_ref[...] = _rms_mod(x_ref[...], w_ref[...], mod_ref[1:2, :], mod_ref[0:1, :]).astype(o_ref.dtype)


def _prenorm(x, w, mod_l, ts):
    B, S, D = x.shape
    return pl.pallas_call(
        _prenorm_kernel,
        grid=(B, S // ts),
        in_specs=[pl.BlockSpec((None, ts, D), lambda b, s: (b, s, 0)),
                  pl.BlockSpec((1, D), lambda b, s: (0, 0)),
                  pl.BlockSpec((None, 6, D), lambda b, s: (b, 0, 0))],
        out_specs=pl.BlockSpec((None, ts, D), lambda b, s: (b, s, 0)),
        out_shape=jax.ShapeDtypeStruct((B, S, D), BF16),
        compiler_params=_params(2),
        name="prenorm",
    )(x, w.reshape(1, D), mod_l)


def _ret_kernel(h_ref, cos_ref, sin_ref, wq_ref, wk_ref, wv_ref, wg_ref, wo_ref,
                di_ref, dq_ref, dk_ref, dc_ref, o_ref,
                st_ref, q_s, k_s, v_s, g_s):
    nb, ts, D = h_ref.shape
    R = nb * ts

    @pl.when(pl.program_id(1) == 0)
    def _():
        st_ref[...] = jnp.zeros_like(st_ref)

    h = h_ref[...].reshape(R, D)
    cos = cos_ref[...].reshape(R, RET_DK)
    sin = sin_ref[...].reshape(R, RET_DK)
    q = _dot(h, wq_ref[...])
    k = _dot(h, wk_ref[...])
    scale = RET_DK ** -0.5
    for hh in range(RET_HEADS):
        sl = slice(hh * RET_DK, (hh + 1) * RET_DK)
        qh = q[:, sl]
        kh = k[:, sl]
        q_s[:, sl] = (qh * cos + pltpu.roll(qh, RET_DK // 2, 1) * sin).astype(BF16)
        k_s[:, sl] = (kh * cos + pltpu.roll(kh, RET_DK // 2, 1) * sin) * scale
    v_s[...] = _dot(h, wv_ref[...]).astype(BF16)
    g_s[...] = _silu(_dot(h, wg_ref[...]))

    def chunk(c, carry):
        for bi in range(nb):
            rows = pl.ds(pl.multiple_of(bi * ts + c * CHUNK, CHUNK), CHUNK)
            for hh in range(RET_HEADS):
                ksl = slice(hh * RET_DK, (hh + 1) * RET_DK)
                vsl = slice(hh * RET_DV, (hh + 1) * RET_DV)
                qc = q_s[rows, ksl]
                kc = k_s[rows, ksl]
                vc = v_s[rows, vsl]
                st = st_ref[bi * RET_HEADS + hh]
                scores = _dot_nt(qc, kc.astype(BF16)) * di_ref[hh]
                o = _dot(scores.astype(BF16), vc) + _dot(qc, st.astype(BF16)) * dq_ref[hh]
                st_ref[bi * RET_HEADS + hh] = st * dc_ref[hh] + _dot_tn((kc * dk_ref[hh]).astype(BF16), vc)
                mu = jnp.mean(o, axis=-1, keepdims=True)
                d = o - mu
                var = jnp.mean(d * d, axis=-1, keepdims=True)
                g_s[rows, vsl] = g_s[rows, vsl] * (d * lax.rsqrt(var + EPS))
        return carry

    lax.fori_loop(0, ts // CHUNK, chunk, 0)
    o_ref[...] = _dot(g_s[...].astype(BF16), wo_ref[...]).astype(o_ref.dtype).reshape(nb, ts, D)


def _retention(h, cos, sin, wq, wk, wv, wg, wo, ts, nb):
    B, S, D = h.shape
    H, C = RET_HEADS, CHUNK
    log_gamma = jnp.log(1.0 - 2.0 ** (-5.0 - jnp.arange(H, dtype=F32)))
    idx = jnp.arange(C, dtype=F32)
    rel = idx[:, None] - idx[None, :]
    di = jnp.where(rel >= 0.0, jnp.exp(log_gamma[:, None, None] * jnp.maximum(rel, 0.0)), 0.0)
    dq = jnp.broadcast_to(jnp.exp(log_gamma[:, None] * (idx[None, :] + 1.0))[:, :, None], (H, C, RET_DV))
    dk = jnp.broadcast_to(jnp.exp(log_gamma[:, None] * (C - 1.0 - idx[None, :]))[:, :, None], (H, C, RET_DK))
    dc = jnp.broadcast_to(jnp.exp(log_gamma * C)[:, None, None], (H, 1, RET_DV))
    qk, vw = H * RET_DK, H * RET_DV
    tile = lambda w: pl.BlockSpec((nb, ts, w), lambda b, s: (b, s, 0))
    R = nb * ts
    return pl.pallas_call(
        _ret_kernel,
        grid=(B // nb, S // ts),
        in_specs=[tile(D), tile(RET_DK), tile(RET_DK),
                  _const_spec((D, qk)), _const_spec((D, qk)), _const_spec((D, vw)), _const_spec((D, vw)),
                  _const_spec((vw, D)),
                  _const_spec((H, C, C)), _const_spec((H, C, RET_DV)), _const_spec((H, C, RET_DK)),
                  _const_spec((H, 1, RET_DV))],
        out_specs=tile(D),
        out_shape=jax.ShapeDtypeStruct((B, S, D), BF16),
        scratch_shapes=[pltpu.VMEM((nb * H, RET_DK, RET_DV), F32),
                        pltpu.VMEM((R, qk), BF16), pltpu.VMEM((R, qk), F32),
                        pltpu.VMEM((R, vw), BF16), pltpu.VMEM((R, vw), F32)],
        compiler_params=_params(2),
        name="retention",
    )(h, cos, sin, wq, wk, wv, wg, wo, di, dq, dk, dc)


def _hg_level_matrices():
    C = CHUNK
    r = jnp.arange(C)
    tril = (r[:, None] >= r[None, :])
    mats = [tril]
    for b in (4, 2):
        mid = (r // (2 * b)) * (2 * b) + b
        odd = (r & b) != 0
        tau = r[None, :]
        m_odd = odd[:, None] & (tau >= mid[:, None]) & (tau <= r[:, None])
        m_even = (~odd)[:, None] & (tau > r[:, None]) & (tau < mid[:, None])
        mats.append(m_odd | m_even)
    stack = jnp.concatenate(mats, axis=0).astype(BF16)
    masks = []
    for b in HG_LEVELS:
        same = (r[:, None] // (2 * b)) == (r[None, :] // (2 * b))
        masks.append(same & ((r[:, None] & b) != 0) & ((r[None, :] & b) == 0))
    masks.append(r[:, None] == r[None, :])
    return stack, jnp.stack(masks).astype(F32)


def _hg_kernel(h_ref, wq_ref, wf_ref, wi_ref, wg_ref, wo_ref, lb_ref, nw_ref, cm_ref, mk_ref, o_ref,
               st_ref, q_s, k_s, v_s, lf_s, g_s, G_s, a_s):
    nb, ts, D = h_ref.shape
    C = CHUNK
    W = HG_HEADS * HG_DK

    @pl.when(pl.program_id(1) == 0)
    def _():
        st_ref[...] = jnp.zeros_like(st_ref)

    h = h_ref[...].reshape(nb * ts, D)
    lb = lb_ref[...]
    q_s[...] = _silu(_dot(h, wq_ref[...]))
    hf = _dot(h, wf_ref[...])
    lf_s[...] = jnp.log(lb + (1.0 - lb) * _sigmoid(hf)) * LOG2E
    k_s[...] = (1.0 - lb) * _sigmoid(-hf)
    v_s[...] = _dot(h, wi_ref[...]).astype(BF16)
    g_s[...] = _silu(_dot(h, wg_ref[...]))
    odd_row = {b: (lax.broadcasted_iota(jnp.int32, (C, 1), 0) & b) != 0 for b in HG_LEVELS}

    def one_chunk(bi, c):
        rows = pl.ds(pl.multiple_of(bi * ts + c * C, C), C)
        G_b = G_s.at[bi]
        lf = lf_s[rows, :]
        q = q_s[rows, :]
        k = k_s[rows, :]
        sums = _dot3(cm_ref[...], lf)
        G = sums[0:C]
        G_b[...] = G
        for li, b in enumerate(HG_LEVELS):
            if b >= SUBLANES:
                pieces = []
                for j in range(C // b):
                    ref = G_b[(j // 2) * 2 * b + b - 1:(j // 2) * 2 * b + b, :]
                    blk = slice(j * b, (j + 1) * b)
                    if j % 2 == 1:
                        pieces.append(q[blk] * jnp.exp2(G[blk] - ref))
                    else:
                        pieces.append(k[blk] * jnp.exp2(ref - G[blk]))
                z = jnp.concatenate(pieces, axis=0)
            elif b > 1:
                e = sums[C:2 * C] if b == 4 else sums[2 * C:3 * C]
                z = jnp.where(odd_row[b], q, k) * jnp.exp2(e)
            else:
                z = jnp.where(odd_row[b], q * jnp.exp2(lf), k)
            z = z.astype(BF16)
            for hh in range(HG_HEADS):
                sl = slice(hh * HG_DK, (hh + 1) * HG_DK)
                a = _dot_nt(z[:, sl], z[:, sl]) * mk_ref[li]
                if li == 0:
                    a_s[bi * HG_HEADS + hh] = a
                else:
                    a_s[bi * HG_HEADS + hh] = a_s[bi * HG_HEADS + hh] + a
        g_last = G_b[C - 1:C, :]
        qg = (q * jnp.exp2(G)).astype(BF16)
        kh = (k * jnp.exp2(g_last - G)).astype(BF16)
        qk = q * k
        dec = jnp.exp2(g_last)
        eye = mk_ref[len(HG_LEVELS)]
        for hh in range(HG_HEADS):
            sl = slice(hh * HG_DK, (hh + 1) * HG_DK)
            vh = v_s[rows, sl]
            diag = jnp.sum(qk[:, sl], axis=-1, keepdims=True)
            a = (a_s[bi * HG_HEADS + hh] + diag * eye).astype(BF16)
            st = st_ref[bi * HG_HEADS + hh]
            o = _dot(a, vh) + _dot_nt(qg[:, sl], st.astype(BF16))
            st_ref[bi * HG_HEADS + hh] = st * dec[:, sl] + _dot_tn(vh, kh[:, sl])
            on = o * lax.rsqrt(jnp.mean(o * o, axis=-1, keepdims=True) + EPS) * nw_ref[...]
            g_s[rows, sl] = on * g_s[rows, sl]

    def chunk(c, carry):
        for bi in range(nb):
            one_chunk(bi, c)
        return carry

    lax.fori_loop(0, ts // C, chunk, 0)
    o_ref[...] = _dot(g_s[...].astype(BF16), wo_ref[...]).astype(o_ref.dtype).reshape(nb, ts, D)


def _hgrn2(h, wq, wf, wi, wg, wo, lb, norm_w, ts, nb):
    B, S, D = h.shape
    W = HG_HEADS * HG_DK
    C = CHUNK
    R = nb * ts
    cm, mk = _hg_level_matrices()
    tile = lambda w: pl.BlockSpec((nb, ts, w), lambda b, s: (b, s, 0))
    return pl.pallas_call(
        _hg_kernel,
        grid=(B // nb, S // ts),
        in_specs=[tile(D)] + [_const_spec((D, W))] * 4 + [_const_spec((W, D)), _const_spec((1, W)),
                  _const_spec((1, HG_DK)), _const_spec(cm.shape), _const_spec(mk.shape)],
        out_specs=tile(D),
        out_shape=jax.ShapeDtypeStruct((B, S, D), BF16),
        scratch_shapes=[pltpu.VMEM((nb * HG_HEADS, HG_DK, HG_DK), F32),
                        pltpu.VMEM((R, W), F32), pltpu.VMEM((R, W), F32), pltpu.VMEM((R, W), BF16),
                        pltpu.VMEM((R, W), F32), pltpu.VMEM((R, W), F32),
                        pltpu.VMEM((nb, C, W), F32), pltpu.VMEM((nb * HG_HEADS, C, C), F32)],
        compiler_params=_params(2),
        name="hgrn2",
    )(h, wq, wf, wi, wg, wo, lb.reshape(1, W), norm_w.reshape(1, HG_DK), cm, mk)


def _softplus(x):
    return jnp.maximum(x, 0.0) + jnp.log(1.0 + jnp.exp(-jnp.abs(x)))


def _ssd_kernel(h_ref, wz_ref, wx_ref, wdt_ref, wo_ref, cw_ref, cb_ref, dtb_ref, a_ref, dl_ref, nw_ref,
                tril_ref, negm_ref, ex_ref, o_ref,
                st_ref, xb_s, xc_s, z_s, dt_s, y_s):
    nb, ts, D = h_ref.shape
    C = CHUNK
    inner = SSD_HEADS * SSD_P
    gw = SSD_GROUPS * SSD_N
    gp = SSD_HPG * SSD_P
    halo = SUBLANES

    @pl.when(pl.program_id(1) == 0)
    def _():
        st_ref[...] = jnp.zeros_like(st_ref)
        xb_s[:, 0:halo, :] = jnp.zeros((nb, halo, xb_s.shape[2]), F32)

    h = h_ref[...].reshape(nb * ts, D)
    z_s[...] = _silu(_dot(h, wz_ref[...]))
    dt_s[...] = _softplus(_dot(h, wdt_ref[...]) + dtb_ref[...])
    for bi in range(nb):
        xb = xb_s.at[bi]
        xb[halo:halo + ts, :] = _dot(h[bi * ts:(bi + 1) * ts], wx_ref[...])
        xall = xb[...]
        acc = cw_ref[0:1, :] * xall
        for j in range(1, SSD_CONV):
            acc = cw_ref[j:j + 1, :] * xall + pltpu.roll(acc, 1, 0)
        xc_s[bi * ts:(bi + 1) * ts, :] = _silu(acc[halo:halo + ts] + cb_ref[...])
        xb[0:halo, :] = xb[ts:ts + halo, :]

    low_half = lax.broadcasted_iota(jnp.int32, (C, LANES), 1) < SSD_P

    def one_chunk(bi, c):
        rows = pl.ds(pl.multiple_of(bi * ts + c * C, C), C)
        dt = dt_s[rows, :]
        da = dt * a_ref[...]
        cum = jnp.dot(tril_ref[...], da, preferred_element_type=F32, precision=lax.Precision.HIGHEST) * LOG2E
        cum_t = cum.T
        row_t = (cum - jnp.log(dt) * LOG2E).T
        cum_last = cum[C - 1:C, :]
        wfac = dt * jnp.exp2(cum_last - cum)
        w_hi = wfac.astype(BF16)
        w_lo = (wfac - w_hi.astype(F32)).astype(BF16)
        wexp = _dot(w_hi, ex_ref[...]) + _dot(w_lo, ex_ref[...])
        dl = jnp.broadcast_to(jnp.exp2(cum_last), (SUBLANES, LANES))
        d_hi = dl.astype(BF16)
        d_lo = (dl - d_hi.astype(F32)).astype(BF16)
        dexp = (_dot(d_hi, ex_ref[...]) + _dot(d_lo, ex_ref[...]))[0:1, :]
        for g in range(SSD_GROUPS):
            bg = xc_s[rows, inner + g * SSD_N:inner + (g + 1) * SSD_N]
            cg = xc_s[rows, inner + gw + g * SSD_N:inner + gw + (g + 1) * SSD_N]
            bgb = bg.astype(BF16)
            cb = _dot_nt(cg.astype(BF16), bgb)
            st = st_ref[bi * SSD_GROUPS + g]
            stb = st.astype(BF16)
            xg = xc_s[rows, g * gp:(g + 1) * gp]
            xgb = xg.astype(BF16)
            for jp in range(SSD_HPG // 2):
                lsl = slice(jp * LANES, (jp + 1) * LANES)
                rhs = jnp.concatenate([xgb[:, lsl], stb[:, lsl]], axis=0)
                ys = []
                for hd in (g * SSD_HPG + 2 * jp, g * SSD_HPG + 2 * jp + 1):
                    colb = jnp.broadcast_to(cum[:, hd:hd + 1], (C, C))
                    w = cb * jnp.exp2(colb - row_t[hd:hd + 1, :] + negm_ref[...])
                    ce = cg * jnp.exp2(colb)
                    ys.append(_dot(jnp.concatenate([w.astype(BF16), ce.astype(BF16)], axis=1), rhs))
                y_s[rows, g * gp + jp * LANES:g * gp + (jp + 1) * LANES] = jnp.where(low_half, ys[0], ys[1])
            gsl = slice(g * gp, (g + 1) * gp)
            st_ref[bi * SSD_GROUPS + g] = st * dexp[:, gsl] + _dot_tn(bgb, (xg * wexp[:, gsl]).astype(BF16))
        xs = xc_s[rows, 0:inner]
        yz = (y_s[rows, :] + dl_ref[...] * xs) * z_s[rows, :]
        gi = inner // SSD_GROUPS
        for g in range(SSD_GROUPS):
            sl = slice(g * gi, (g + 1) * gi)
            seg = yz[:, sl]
            nrm = seg * lax.rsqrt(jnp.mean(seg * seg, axis=-1, keepdims=True) + EPS)
            y_s[rows, sl] = nrm * nw_ref[:, sl]

    def chunk(c, carry):
        for bi in range(nb):
            one_chunk(bi, c)
        return carry

    lax.fori_loop(0, ts // C, chunk, 0)
    o_ref[...] = _dot(y_s[...].astype(BF16), wo_ref[...]).astype(o_ref.dtype).reshape(nb, ts, D)


def _ssd(h, wz, wx, wdt, wo, conv_w, conv_b, dt_bias, a_log, d_skip, norm_w, ts, nb):
    B, S, D = h.shape
    C = CHUNK
    inner = SSD_HEADS * SSD_P
    cdim = inner + 2 * SSD_GROUPS * SSD_N
    pad = LANES - SSD_HEADS
    wdt_p = jnp.pad(wdt, ((0, 0), (0, pad)))
    dtb = jnp.pad(dt_bias.astype(F32), (0, pad)).reshape(1, LANES)
    a = jnp.pad(-jnp.exp(a_log.astype(F32)), (0, pad)).reshape(1, LANES)
    dl = jnp.repeat(d_skip.astype(F32), SSD_P).reshape(1, inner)
    r = jnp.arange(C)
    tril = (r[:, None] >= r[None, :]).astype(F32)
    negm = jnp.where(r[:, None] >= r[None, :], 0.0, NEG_BIG).astype(F32)
    ex = (jnp.arange(LANES)[:, None] == (jnp.arange(inner)[None, :] // SSD_P)).astype(BF16)
    tile = lambda w: pl.BlockSpec((nb, ts, w), lambda b, s: (b, s, 0))
    R = nb * ts
    return pl.pallas_call(
        _ssd_kernel,
        grid=(B // nb, S // ts),
        in_specs=[tile(D), _const_spec((D, inner)), _const_spec((D, cdim)), _const_spec((D, LANES)),
                  _const_spec((inner, D)), _const_spec((SSD_CONV, cdim)), _const_spec((1, cdim)),
                  _const_spec((1, LANES)), _const_spec((1, LANES)), _const_spec((1, inner)),
                  _const_spec((1, inner)), _const_spec((C, C)), _const_spec((C, C)),
                  _const_spec((LANES, inner))],
        out_specs=tile(D),
        out_shape=jax.ShapeDtypeStruct((B, S, D), BF16),
        scratch_shapes=[pltpu.VMEM((nb * SSD_GROUPS, SSD_N, SSD_HPG * SSD_P), F32),
                        pltpu.VMEM((nb, ts + SUBLANES, cdim), F32), pltpu.VMEM((R, cdim), F32),
                        pltpu.VMEM((R, inner), F32), pltpu.VMEM((R, LANES), F32),
                        pltpu.VMEM((R, inner), F32)],
        compiler_params=_params(2),
        name="ssd",
    )(h, wz, wx, wdt_p, wo, conv_w.astype(F32), conv_b.astype(F32).reshape(1, cdim), dtb, a, dl,
      norm_w.astype(F32).reshape(1, inner), tril, negm, ex)


def _merge_kernel(x_ref, h_ref, yr_ref, yh_ref, ys_ref, wgl_ref, wout_ref, mod_ref, nw_ref,
                  wr_hi_ref, wr_lo_ref, br_ref, ut_ref, x1_ref, h2_ref, rf_ref, ri_ref, cnt_ref, run_ref):
    D = x_ref.shape[1]
    ts = x_ref.shape[0]

    @pl.when((pl.program_id(0) == 0) & (pl.program_id(1) == 0))
    def _():
        run_ref[...] = jnp.zeros_like(run_ref)

    h = h_ref[...]
    gates = _sigmoid(_dot(h, wgl_ref[...]))
    m = (gates[:, 0:D] * yr_ref[...].astype(F32) + gates[:, D:2 * D] * yh_ref[...].astype(F32)
         + gates[:, 2 * D:3 * D] * ys_ref[...].astype(F32))
    y = _dot(m.astype(BF16), wout_ref[...])
    x1 = x_ref[...] + mod_ref[2:3, :] * y
    x1_ref[...] = x1
    h2 = _rms_mod(x1, nw_ref[...], mod_ref[4:5, :], mod_ref[3:4, :])
    h2_ref[...] = h2
    h_hi = h2.astype(BF16)
    h_lo = (h2 - h_hi.astype(F32)).astype(BF16)
    lg = (_dot_nt(wr_hi_ref[...], h_hi) + _dot_nt(wr_hi_ref[...], h_lo) + _dot_nt(wr_lo_ref[...], h_hi)
          + br_ref[...])
    gl = lg[0:MOE_GROUPS]
    gmax = jnp.max(gl, axis=0, keepdims=True)
    gi = lax.broadcasted_iota(jnp.int32, gl.shape, 0)
    gtop = jnp.min(jnp.where(gl == gmax, gi, MOE_GROUPS), axis=0, keepdims=True)
    p_g = 1.0 / jnp.sum(jnp.exp(gl - gmax), axis=0, keepdims=True)
    el = jnp.zeros((MOE_PER_GROUP, ts), F32)
    for g in range(MOE_GROUPS):
        lo = SUBLANES + g * MOE_PER_GROUP
        el = jnp.where(gtop == g, lg[lo:lo + MOE_PER_GROUP], el)
    ei = lax.broadcasted_iota(jnp.int32, el.shape, 0)
    m1 = jnp.max(el, axis=0, keepdims=True)
    i1 = jnp.min(jnp.where(el == m1, ei, MOE_PER_GROUP), axis=0, keepdims=True)
    el2 = jnp.where(ei == i1, -jnp.inf, el)
    m2 = jnp.max(el2, axis=0, keepdims=True)
    i2 = jnp.min(jnp.where(el2 == m2, ei, MOE_PER_GROUP), axis=0, keepdims=True)
    e21 = jnp.exp(m2 - m1)
    den = 1.0 + e21
    w1 = (1.0 / den) * p_g
    w2 = (e21 / den) * p_g
    zf = jnp.zeros((SUBLANES - 2, ts), F32)
    rf_ref[...] = jnp.concatenate([w1, w2, zf], axis=0)
    e1 = gtop * MOE_PER_GROUP + i1
    e2 = gtop * MOE_PER_GROUP + i2
    ex = lax.broadcasted_iota(jnp.int32, (MOE_EXPERTS, ts), 0)
    oh1 = jnp.where(ex == e1, 1.0, 0.0)
    oh2 = jnp.where(ex == e2, 1.0, 0.0)
    p1 = _dot(oh1.astype(BF16), ut_ref[...])
    p2 = _dot(oh2.astype(BF16), ut_ref[...])
    c1 = p1[:, ts - 1:ts]
    c2 = p2[:, ts - 1:ts]
    run = run_ref[...]
    base = run[:, 0:1]
    r1 = jnp.sum(oh1 * (p1 - 1.0 + base), axis=0, keepdims=True)
    r2 = jnp.sum(oh2 * (p2 - 1.0 + base + c1), axis=0, keepdims=True)
    run = run + (c1 + c2)
    run_ref[...] = run
    cnt_ref[...] = run
    ri_ref[...] = jnp.concatenate([e1, e2, r1.astype(jnp.int32), r2.astype(jnp.int32),
                                   jnp.zeros((SUBLANES - 4, ts), jnp.int32)], axis=0)


def _merge(x, h, y_ret, y_hg, y_ssd, wgl, wout, mod_l, norm_w, wr_hi, wr_lo, br, ts):
    B, S, D = x.shape
    ns = S // ts
    R = wr_hi.shape[0]
    tile = pl.BlockSpec((None, ts, D), lambda b, s: (b, s, 0))
    rt = pl.BlockSpec((SUBLANES, ts), lambda b, s: (0, b * ns + s))
    r = jnp.arange(ts)
    ut = (r[:, None] <= r[None, :]).astype(BF16)
    return pl.pallas_call(
        _merge_kernel,
        grid=(B, ns),
        in_specs=[tile, tile, tile, tile, tile, _const_spec((D, 3 * D)), _const_spec((D, D)),
                  pl.BlockSpec((None, 6, D), lambda b, s: (b, 0, 0)), _const_spec((1, D)),
                  _const_spec((R, D)), _const_spec((R, D)), _const_spec((R, 1)), _const_spec((ts, ts))],
        out_specs=[tile, tile, rt, rt, pl.BlockSpec((MOE_EXPERTS, LANES), lambda b, s: (0, 0))],
        out_shape=[jax.ShapeDtypeStruct((B, S, D), F32), jax.ShapeDtypeStruct((B, S, D), F32),
                   jax.ShapeDtypeStruct((SUBLANES, B * S), F32), jax.ShapeDtypeStruct((SUBLANES, B * S), jnp.int32),
                   jax.ShapeDtypeStruct((MOE_EXPERTS, LANES), F32)],
        scratch_shapes=[pltpu.VMEM((MOE_EXPERTS, LANES), F32)],
        compiler_params=_params(2),
        name="merge_router",
    )(x, h, y_ret, y_hg, y_ssd, wgl, wout, mod_l, norm_w.reshape(1, D), wr_hi, wr_lo, br, ut)


def _row_copy(src, src_row, dst, dst_row, sem):
    return pltpu.make_async_copy(src.at[pl.ds(src_row, 1), :], dst.at[pl.ds(dst_row, 1), :], sem)


DISPATCH_TILE = 1024
DMA_UNROLL = 8


def _dispatch_kernel(pend_ref, plen_ref, nact_ref, dest_ref, h2_ref, xs_hbm, zbuf, sem, zsem):
    i = pl.program_id(0)
    td = dest_ref.shape[2]
    BLK = MOE_BLOCK
    nb = xs_hbm.shape[0] // BLK

    def zero_fill(wait):
        def go(start, size):
            cp = pltpu.make_async_copy(zbuf.at[pl.ds(0, size), :], xs_hbm.at[pl.ds(start, size), :], zsem)
            if wait:
                cp.wait()
            else:
                cp.start()

        for e in range(MOE_EXPERTS):
            ln = plen_ref[e]
            end = pend_ref[e]
            size = BLK // 2
            while size >= SUBLANES:
                @pl.when((ln & size) != 0)
                def _(size=size, ln=ln, end=end):
                    go(pl.multiple_of(end - (ln & ~(size - 1)), size), size)
                size //= 2
            for j in range(SUBLANES - 1):
                @pl.when(j < (ln & (SUBLANES - 1)))
                def _(j=j, ln=ln, end=end):
                    go(end - ln + j, 1)
        for j in range(MOE_EXPERTS):
            @pl.when(nact_ref[0] + j < nb)
            def _(j=j):
                go(pl.multiple_of((nact_ref[0] + j) * BLK, BLK), BLK)

    @pl.when(i == 0)
    def _():
        zbuf[...] = jnp.zeros_like(zbuf)
        zero_fill(False)

    def issue(j, c):
        for u in range(DMA_UNROLL):
            r = j * DMA_UNROLL + u
            for k in range(2):
                _row_copy(h2_ref, r, xs_hbm, dest_ref[0, k, r], sem).start(priority=k)
        return c

    lax.fori_loop(0, td // DMA_UNROLL, issue, 0)
    for k in range(2):
        pltpu.make_async_copy(h2_ref, xs_hbm.at[pl.ds(0, td), :], sem).wait()

    @pl.when(i == 0)
    def _():
        zero_fill(True)


def _dispatch(h2, dest, pad_end, pad_len, nactive, nrows):
    T, D = h2.shape
    td = min(DISPATCH_TILE, T)
    nt = T // td
    dest3 = dest.reshape(2, nt, td).transpose(1, 0, 2)
    grid_spec = pltpu.PrefetchScalarGridSpec(
        num_scalar_prefetch=3,
        grid=(nt,),
        in_specs=[pl.BlockSpec((1, 2, td), lambda i, *_: (i, 0, 0), memory_space=pltpu.SMEM),
                  pl.BlockSpec((td, D), lambda i, *_: (i, 0))],
        out_specs=pl.BlockSpec(memory_space=pl.ANY),
        scratch_shapes=[pltpu.VMEM((MOE_BLOCK, D), F32), pltpu.SemaphoreType.DMA(()),
                        pltpu.SemaphoreType.DMA(())])
    return pl.pallas_call(
        _dispatch_kernel,
        grid_spec=grid_spec,
        out_shape=jax.ShapeDtypeStruct((nrows, D), F32),
        compiler_params=_params(1),
        name="dispatch",
    )(pad_end, pad_len, nactive, dest3, h2)


def _expert_kernel(be_ref, na_ref, x_ref, w1_ref, w3_ref, w2_ref, y_ref, w1_s, w3_s, w2_s):
    i = pl.program_id(0)

    @pl.when((i == 0) | (be_ref[i] != be_ref[jnp.maximum(i - 1, 0)]))
    def _():
        w1_s[...] = w1_ref[...].astype(BF16)
        w3_s[...] = w3_ref[...].astype(BF16)
        w2_s[...] = w2_ref[...].astype(BF16)

    @pl.when(i < na_ref[0])
    def _():
        x = x_ref[...].astype(BF16)
        a = _dot(x, w1_s[...])
        b = _dot(x, w3_s[...])
        y_ref[...] = _dot((_silu(a) * b).astype(BF16), w2_s[...])

    @pl.when(i >= na_ref[0])
    def _():
        y_ref[...] = jnp.zeros_like(y_ref)


def _experts(xs, block_e, nactive, w1, w3, w2, layer):
    nrows, D = xs.shape
    BLK = MOE_BLOCK
    nb = nrows // BLK
    ff = w1.shape[3]
    wsel = lambda i, be, na: (layer, be[i], 0, 0)
    grid_spec = pltpu.PrefetchScalarGridSpec(
        num_scalar_prefetch=2,
        grid=(nb,),
        in_specs=[pl.BlockSpec((BLK, D), lambda i, be, na: (jnp.minimum(i, na[0] - 1), 0)),
                  pl.BlockSpec((None, None, D, ff), wsel), pl.BlockSpec((None, None, D, ff), wsel),
                  pl.BlockSpec((None, None, ff, D), wsel)],
        out_specs=pl.BlockSpec((BLK, D), lambda i, be, na: (i, 0)),
        scratch_shapes=[pltpu.VMEM((D, ff), BF16), pltpu.VMEM((D, ff), BF16), pltpu.VMEM((ff, D), BF16)])
    return pl.pallas_call(
        _expert_kernel,
        grid_spec=grid_spec,
        out_shape=jax.ShapeDtypeStruct((nrows, D), F32),
        compiler_params=_params(1),
        name="experts",
    )(block_e, nactive, xs, w1, w3, w2)


def _dispatch_plan(route_i, counts_f, T):
    BLK = MOE_BLOCK
    nb = -(-(2 * T) // BLK) + MOE_EXPERTS
    counts = counts_f[:, 0].astype(jnp.int32)
    padded = ((counts + BLK - 1) // BLK) * BLK
    pad_ends = jnp.cumsum(padded)
    pad_starts = pad_ends - padded
    e = route_i[0:2]
    rank = route_i[2:4]
    sel = e[:, :, None] == jnp.arange(MOE_EXPERTS, dtype=jnp.int32)[None, None, :]
    dest = rank + jnp.sum(jnp.where(sel, pad_starts[None, None, :], 0), axis=-1)
    nactive = (pad_ends[-1] // BLK).astype(jnp.int32)
    blk_id = jnp.minimum(jnp.arange(nb, dtype=jnp.int32), nactive - 1)
    block_e = jnp.sum((blk_id[:, None] * BLK >= pad_ends[None, :]).astype(jnp.int32), axis=1)
    block_e = jnp.minimum(block_e, MOE_EXPERTS - 1)
    return dest, block_e, pad_ends, padded - counts, nactive.reshape(1), nb * BLK


COMBINE_TILE = 256


def _combine_kernel(final, dcur_ref, dnxt_ref, x_ref, rw_ref, mod_ref, nw_ref, modn_ref, ys_hbm, *rest):
    if final:
        o_ref, gbuf, sem = rest
    else:
        x2_ref, h_ref, gbuf, sem = rest
    i = pl.program_id(0)
    n = pl.num_programs(0)
    ts = x_ref.shape[0]
    slot = i % 2

    def issue(dref, sl):
        def body(j, c):
            for u in range(DMA_UNROLL):
                r = j * DMA_UNROLL + u
                for k in range(2):
                    _row_copy(ys_hbm, dref[0, k, r], gbuf.at[sl, k], r, sem.at[sl]).start(priority=k)
            return c
        lax.fori_loop(0, ts // DMA_UNROLL, body, 0)

    @pl.when(i == 0)
    def _():
        issue(dcur_ref, 0)

    @pl.when(i + 1 < n)
    def _():
        issue(dnxt_ref, 1 - slot)

    for k in range(2):
        pltpu.make_async_copy(ys_hbm.at[pl.ds(0, ts), :], gbuf.at[slot, k], sem.at[slot]).wait()
    y = rw_ref[:, 0:1] * gbuf[slot, 0] + rw_ref[:, 1:2] * gbuf[slot, 1]
    x2 = x_ref[...] + mod_ref[5:6, :] * y
    if final:
        o_ref[...] = (x2 * lax.rsqrt(jnp.mean(x2 * x2, axis=-1, keepdims=True) + EPS) * nw_ref[...]).astype(o_ref.dtype)
    else:
        x2_ref[...] = x2
        h_ref[...] = _rms_mod(x2, nw_ref[...], modn_ref[1:2, :], modn_ref[0:1, :]).astype(h_ref.dtype)


def _combine(x1, ys, dest, rw, mod_l, norm_w, mod_next, final, out_dtype):
    B, S, D = x1.shape
    T = B * S
    ts = min(COMBINE_TILE, S)
    nsb = S // ts
    nt = T // ts
    dest3 = dest.reshape(2, nt, ts).transpose(1, 0, 2)
    tile = pl.BlockSpec((ts, D), lambda i: (i, 0))
    modspec = pl.BlockSpec((None, 6, D), lambda i: (i // nsb, 0, 0))
    if final:
        out_specs = tile
        out_shape = jax.ShapeDtypeStruct((T, D), out_dtype)
    else:
        out_specs = [tile, tile]
        out_shape = [jax.ShapeDtypeStruct((T, D), F32), jax.ShapeDtypeStruct((T, D), BF16)]
    out = pl.pallas_call(
        functools.partial(_combine_kernel, final),
        grid=(nt,),
        in_specs=[pl.BlockSpec((1, 2, ts), lambda i: (i, 0, 0), memory_space=pltpu.SMEM),
                  pl.BlockSpec((1, 2, ts), lambda i: (jnp.minimum(i + 1, nt - 1), 0, 0), memory_space=pltpu.SMEM),
                  tile, pl.BlockSpec((ts, SUBLANES), lambda i: (i, 0)), modspec,
                  pl.BlockSpec((1, D), lambda i: (0, 0)), modspec,
                  pl.BlockSpec(memory_space=pl.ANY)],
        out_specs=out_specs,
        out_shape=out_shape,
        scratch_shapes=[pltpu.VMEM((2, 2, ts, D), F32), pltpu.SemaphoreType.DMA((2,))],
        compiler_params=_params(1),
        name="combine_final" if final else "combine_prenorm",
    )(dest3, dest3, x1.reshape(T, D), rw, mod_l, norm_w.reshape(1, D), mod_next, ys)
    if final:
        return out.reshape(B, S, D)
    return out[0].reshape(B, S, D), out[1].reshape(B, S, D)


def _seq_tile(S, want):
    ts = min(want, S)
    assert S % ts == 0 and ts % CHUNK == 0
    return ts


def kernel(x, c, positions, w_ada, b_ada, norm_mix_w, w_in, ssd_conv_w, ssd_conv_b, ssd_dt_bias, ssd_a_log,
           ssd_d, ssd_norm_w, hg_lb, hg_norm_w, w_ret_o, w_hg_o, w_ssd_o, w_out, norm_ffn_w, moe_w_group,
           moe_b_group, moe_w_expert, moe_b_expert, moe_w1, moe_w3, moe_w2, final_norm_w):
    B, S, D = x.shape
    L = w_ada.shape[0]
    T = B * S
    ts_el = _seq_tile(S, 512)
    ts_mix = _seq_tile(S, 256)
    nb_mix = 2 if B % 2 == 0 else 1

    mod = _adaln(c, w_ada, b_ada)
    cos, sin = _rope_tables(positions, ts_el)
    lb_soft = jax.nn.softmax(hg_lb.astype(F32), axis=0)
    lb_all = jnp.cumsum(lb_soft, axis=0) - lb_soft[0]

    qk = RET_HEADS * RET_DK
    vw = RET_HEADS * RET_DV
    hgw = HG_HEADS * HG_DK
    inner = SSD_HEADS * SSD_P
    cdim = inner + 2 * SSD_GROUPS * SSD_N
    sizes = (qk, qk, vw, vw, hgw, hgw, hgw, hgw, inner, cdim, SSD_HEADS, 3 * D)
    offs = [0]
    for n in sizes:
        offs.append(offs[-1] + n)

    xcur = x.astype(F32)
    h = _prenorm(xcur, norm_mix_w[0].astype(F32), mod[0], ts_el)
    for l in range(L):
        wl = w_in[l].astype(BF16)
        cols = [wl[:, offs[i]:offs[i + 1]] for i in range(len(sizes))]
        y_ret = _retention(h, cos, sin, cols[0], cols[1], cols[2], cols[3], w_ret_o[l].astype(BF16), ts_mix,
                           nb_mix)
        y_hg = _hgrn2(h, cols[4], cols[5], cols[6], cols[7], w_hg_o[l].astype(BF16), lb_all[l],
                      hg_norm_w[l].astype(F32), ts_mix, nb_mix)
        y_ssd = _ssd(h, cols[8], cols[9], cols[10], w_ssd_o[l].astype(BF16), ssd_conv_w[l], ssd_conv_b[l],
                     ssd_dt_bias[l], ssd_a_log[l], ssd_d[l], ssd_norm_w[l], ts_mix, nb_mix)
        wr = jnp.zeros((SUBLANES + MOE_EXPERTS, D), F32)
        wr = wr.at[0:MOE_GROUPS].set(moe_w_group[l].astype(F32).T).at[SUBLANES:].set(moe_w_expert[l].astype(F32).T)
        br = jnp.zeros((SUBLANES + MOE_EXPERTS, 1), F32)
        br = br.at[0:MOE_GROUPS, 0].set(moe_b_group[l].astype(F32)).at[SUBLANES:, 0].set(moe_b_expert[l].astype(F32))
        wr_hi = wr.astype(BF16)
        wr_lo = (wr - wr_hi.astype(F32)).astype(BF16)
        x1, h2, route_w, route_i, counts = _merge(xcur, h, y_ret, y_hg, y_ssd, cols[11], w_out[l].astype(BF16),
                                                  mod[l], norm_ffn_w[l].astype(F32), wr_hi, wr_lo, br, ts_mix)
        dest, block_e, pad_end, pad_len, nactive, nrows = _dispatch_plan(route_i, counts, T)
        xs = _dispatch(h2.reshape(T, D), dest, pad_end, pad_len, nactive, nrows)
        ys = _experts(xs, block_e, nactive, moe_w1, moe_w3, moe_w2, l)
        rw = route_w.T
        if l + 1 < L:
            xcur, h = _combine(x1, ys, dest, rw, mod[l], norm_mix_w[l + 1].astype(F32), mod[l + 1], False, x.dtype)
        else:
            out = _combine(x1, ys, dest, rw, mod[l], final_norm_w.astype(F32), mod[l], True, x.dtype)
    return out
```

```python
import functools

import jax
import jax.numpy as jnp
from jax import lax
from jax.experimental import pallas as pl
from jax.experimental.pallas import tpu as pltpu

F32 = jnp.float32
BF16 = jnp.bfloat16

EPS = 1e-6
ROPE_BASE = 10000.0
RET_HEADS = 4
RET_DK = 128
RET_DV = 256
HG_HEADS = 8
HG_DK = 128
SSD_HEADS = 32
SSD_P = 64
SSD_GROUPS = 4
SSD_N = 128
SSD_HPG = SSD_HEADS // SSD_GROUPS
SSD_CONV = 4
MOE_GROUPS = 4
MOE_PER_GROUP = 8
MOE_EXPERTS = 32
MOE_BLOCK = 512
CHUNK = 128
LANES = 128
SUBLANES = 8
VMEM_LIMIT = 56 * 1024 * 1024
HG_LEVELS = (64, 32, 16, 8, 4, 2, 1)
LOG2E = 1.4426950408889634
NEG_BIG = -1e30


def _params(n_axes, vmem=VMEM_LIMIT):
    return pltpu.CompilerParams(dimension_semantics=("arbitrary",) * n_axes,
                                vmem_limit_bytes=vmem)


def _const_spec(shape):
    nd = len(shape)
    return pl.BlockSpec(shape, lambda *_: (0,) * nd, pipeline_mode=pl.Buffered(1))


def _sigmoid(x):
    return 1.0 / (1.0 + jnp.exp(-x))


def _silu(x):
    return x * _sigmoid(x)


def _dot(a, b):
    return jnp.dot(a, b, preferred_element_type=F32)


def _dot_nt(a, b):
    return lax.dot_general(a, b, (((1,), (1,)), ((), ())), preferred_element_type=F32)


def _dot_tn(a, b):
    return lax.dot_general(a, b, (((0,), (0,)), ((), ())), preferred_element_type=F32)


def _split3(x):
    hi = x.astype(BF16)
    r1 = x - hi.astype(F32)
    mid = r1.astype(BF16)
    lo = (r1 - mid.astype(F32)).astype(BF16)
    return hi, mid, lo


def _dot2(m_bf16, x_f32):
    hi = x_f32.astype(BF16)
    lo = (x_f32 - hi.astype(F32)).astype(BF16)
    return _dot(m_bf16, hi) + _dot(m_bf16, lo)


def _dot3(m_bf16, x_f32):
    hi, mid, lo = _split3(x_f32)
    return _dot(m_bf16, hi) + _dot(m_bf16, mid) + _dot(m_bf16, lo)


def _ada_kernel(c_ref, w_ref, b_ref, o_ref):
    c = c_ref[...]
    o_ref[...] = _dot(_silu(c).astype(BF16), w_ref[...].astype(BF16)) + b_ref[...]


def _adaln(c, w_ada, b_ada):
    L, D, N = w_ada.shape
    B = c.shape[0]
    cp = jnp.zeros((SUBLANES, D), F32).at[:B].set(c.astype(F32))
    tn = 1536
    out = pl.pallas_call(
        _ada_kernel,
        grid=(L, N // tn),
        in_specs=[pl.BlockSpec((SUBLANES, D), lambda l, n: (0, 0)),
                  pl.BlockSpec((None, D, tn), lambda l, n: (l, 0, n)),
                  pl.BlockSpec((None, 1, tn), lambda l, n: (l, 0, n))],
        out_specs=pl.BlockSpec((None, SUBLANES, tn), lambda l, n: (l, 0, n)),
        out_shape=jax.ShapeDtypeStruct((L, SUBLANES, N), F32),
        compiler_params=_params(2),
        name="adaln",
    )(cp, w_ada, b_ada.reshape(L, 1, N))
    return out[:, :B].reshape(L, B, 6, D)


def _rope_kernel(pos_ref, invf_ref, sign_ref, cos_ref, sin_ref):
    ang = pos_ref[...].astype(F32) * invf_ref[...]
    cos_ref[...] = jnp.cos(ang)
    sin_ref[...] = jnp.sin(ang) * sign_ref[...]


def _rope_tables(positions, ts):
    B, S = positions.shape
    half = RET_DK // 2
    inv_freq = ROPE_BASE ** (-jnp.linspace(0.0, 1.0, half, dtype=F32))
    invf = jnp.concatenate([inv_freq, inv_freq]).reshape(1, RET_DK)
    sign = jnp.concatenate([-jnp.ones((half,), F32), jnp.ones((half,), F32)]).reshape(1, RET_DK)
    spec = pl.BlockSpec((None, ts, RET_DK), lambda b, s: (b, s, 0))
    return pl.pallas_call(
        _rope_kernel,
        grid=(B, S // ts),
        in_specs=[pl.BlockSpec((None, ts, 1), lambda b, s: (b, s, 0)),
                  pl.BlockSpec((1, RET_DK), lambda b, s: (0, 0)),
                  pl.BlockSpec((1, RET_DK), lambda b, s: (0, 0))],
        out_specs=[spec, spec],
        out_shape=[jax.ShapeDtypeStruct((B, S, RET_DK), F32)] * 2,
        compiler_params=_params(2),
        name="rope_tables",
    )(positions.reshape(B, S, 1), invf, sign)


def _rms_mod(x, w, scale, shift):
    y = x * lax.rsqrt(jnp.mean(x * x, axis=-1, keepdims=True) + EPS)
    return (y * w) * (1.0 + scale) + shift


def _prenorm_kernel(x_ref, w_ref, mod_ref, o_ref):
    o_ref[...] = _rms_mod(x_ref[...], w_ref[...], mod_ref[1:2, :], mod_ref[0:1, :]).astype(o_ref.dtype)


def _prenorm(x, w, mod_l, ts):
    B, S, D = x.shape
    return pl.pallas_call(
        _prenorm_kernel,
        grid=(B, S // ts),
        in_specs=[pl.BlockSpec((None, ts, D), lambda b, s: (b, s, 0)),
                  pl.BlockSpec((1, D), lambda b, s: (0, 0)),
                  pl.BlockSpec((None, 6, D), lambda b, s: (b, 0, 0))],
        out_specs=pl.BlockSpec((None, ts, D), lambda b, s: (b, s, 0)),
        out_shape=jax.ShapeDtypeStruct((B, S, D), BF16),
        compiler_params=_params(2),
        name="prenorm",
    )(x, w.reshape(1, D), mod_l)


def _ret_kernel(h_ref, cos_ref, sin_ref, wq_ref, wk_ref, wv_ref, wg_ref, wo_ref,
                di_ref, dq_ref, dk_ref, dc_ref, o_ref,
                st_ref, q_s, k_s, v_s, g_s):
    nb, ts, D = h_ref.shape
    R = nb * ts

    @pl.when(pl.program_id(1) == 0)
    def _():
        st_ref[...] = jnp.zeros_like(st_ref)

    h = h_ref[...].reshape(R, D)
    cos = cos_ref[...].reshape(R, RET_DK)
    sin = sin_ref[...].reshape(R, RET_DK)
    q = _dot(h, wq_ref[...])
    k = _dot(h, wk_ref[...])
    scale = RET_DK ** -0.5
    for hh in range(RET_HEADS):
        sl = slice(hh * RET_DK, (hh + 1) * RET_DK)
        qh = q[:, sl]
        kh = k[:, sl]
        q_s[:, sl] = (qh * cos + pltpu.roll(qh, RET_DK // 2, 1) * sin).astype(BF16)
        k_s[:, sl] = (kh * cos + pltpu.roll(kh, RET_DK // 2, 1) * sin) * scale
    v_s[...] = _dot(h, wv_ref[...]).astype(BF16)
    g_s[...] = _silu(_dot(h, wg_ref[...]))

    def chunk(c, carry):
        for bi in range(nb):
            rows = pl.ds(pl.multiple_of(bi * ts + c * CHUNK, CHUNK), CHUNK)
            for hh in range(RET_HEADS):
                ksl = slice(hh * RET_DK, (hh + 1) * RET_DK)
                vsl = slice(hh * RET_DV, (hh + 1) * RET_DV)
                qc = q_s[rows, ksl]
                kc = k_s[rows, ksl]
                vc = v_s[rows, vsl]
                st = st_ref[bi * RET_HEADS + hh]
                scores = _dot_nt(qc, kc.astype(BF16)) * di_ref[hh]
                o = _dot(scores.astype(BF16), vc) + _dot(qc, st.astype(BF16)) * dq_ref[hh]
                st_ref[bi * RET_HEADS + hh] = st * dc_ref[hh] + _dot_tn((kc * dk_ref[hh]).astype(BF16), vc)
                mu = jnp.mean(o, axis=-1, keepdims=True)
                d = o - mu
                var = jnp.mean(d * d, axis=-1, keepdims=True)
                g_s[rows, vsl] = g_s[rows, vsl] * (d * lax.rsqrt(var + EPS))
        return carry

    lax.fori_loop(0, ts // CHUNK, chunk, 0)
    o_ref[...] = _dot(g_s[...].astype(BF16), wo_ref[...]).astype(o_ref.dtype).reshape(nb, ts, D)


def _retention(h, cos, sin, wq, wk, wv, wg, wo, ts, nb):
    B, S, D = h.shape
    H, C = RET_HEADS, CHUNK
    log_gamma = jnp.log(1.0 - 2.0 ** (-5.0 - jnp.arange(H, dtype=F32)))
    idx = jnp.arange(C, dtype=F32)
    rel = idx[:, None] - idx[None, :]
    di = jnp.where(rel >= 0.0, jnp.exp(log_gamma[:, None, None] * jnp.maximum(rel, 0.0)), 0.0)
    dq = jnp.broadcast_to(jnp.exp(log_gamma[:, None] * (idx[None, :] + 1.0))[:, :, None], (H, C, RET_DV))
    dk = jnp.broadcast_to(jnp.exp(log_gamma[:, None] * (C - 1.0 - idx[None, :]))[:, :, None], (H, C, RET_DK))
    dc = jnp.broadcast_to(jnp.exp(log_gamma * C)[:, None, None], (H, 1, RET_DV))
    qk, vw = H * RET_DK, H * RET_DV
    tile = lambda w: pl.BlockSpec((nb, ts, w), lambda b, s: (b, s, 0))
    R = nb * ts
    return pl.pallas_call(
        _ret_kernel,
        grid=(B // nb, S // ts),
        in_specs=[tile(D), tile(RET_DK), tile(RET_DK),
                  _const_spec((D, qk)), _const_spec((D, qk)), _const_spec((D, vw)), _const_spec((D, vw)),
                  _const_spec((vw, D)),
                  _const_spec((H, C, C)), _const_spec((H, C, RET_DV)), _const_spec((H, C, RET_DK)),
                  _const_spec((H, 1, RET_DV))],
        out_specs=tile(D),
        out_shape=jax.ShapeDtypeStruct((B, S, D), BF16),
        scratch_shapes=[pltpu.VMEM((nb * H, RET_DK, RET_DV), F32),
                        pltpu.VMEM((R, qk), BF16), pltpu.VMEM((R, qk), F32),
                        pltpu.VMEM((R, vw), BF16), pltpu.VMEM((R, vw), F32)],
        compiler_params=_params(2),
        name="retention",
    )(h, cos, sin, wq, wk, wv, wg, wo, di, dq, dk, dc)


def _hg_level_matrices():
    C = CHUNK
    r = jnp.arange(C)
    stack = (r[:, None] >= r[None, :]).astype(BF16)
    masks = []
    for b in HG_LEVELS:
        same = (r[:, None] // (2 * b)) == (r[None, :] // (2 * b))
        masks.append(same & ((r[:, None] & b) != 0) & ((r[None, :] & b) == 0))
    masks.append(r[:, None] == r[None, :])
    return stack, jnp.stack(masks).astype(F32)


def _hg_kernel(h_ref, hn_ref, wq_ref, wf_ref, wi_ref, wg_ref, wo_ref, lb_ref, nw_ref, cm_ref, mk_ref, o_ref,
               st_ref, q_s, k_s, v_s, lf_s, g_s):
    nb, ts, D = h_ref.shape
    C = CHUNK
    W = HG_HEADS * HG_DK
    n = ts // C
    lb = lb_ref[...]

    def project(hc, slot):
        q_s[slot] = _silu(_dot(hc, wq_ref[...]))
        hf = _dot(hc, wf_ref[...])
        lf_s[slot] = jnp.log(lb + (1.0 - lb) * _sigmoid(hf)) * LOG2E
        k_s[slot] = (1.0 - lb) * _sigmoid(-hf)
        v_s[slot] = _dot(hc, wi_ref[...]).astype(BF16)
        g_s[slot] = _silu(_dot(hc, wg_ref[...]))

    @pl.when(pl.program_id(1) == 0)
    def _():
        st_ref[...] = jnp.zeros_like(st_ref)
        project(h_ref[:, 0:C, :].reshape(nb * C, D), 0)

    odd_row = {b: (lax.broadcasted_iota(jnp.int32, (C, 1), 0) & b) != 0 for b in HG_LEVELS}

    def one_chunk(bi, slot):
        rows = slice(bi * C, (bi + 1) * C)
        lf = lf_s[slot, rows, :]
        q = q_s[slot, rows, :]
        k = k_s[slot, rows, :]
        G = _dot2(cm_ref[...], lf)
        acc = [None] * HG_HEADS
        G3 = G.reshape(C // SUBLANES, SUBLANES, W)
        sub8 = lax.broadcasted_iota(jnp.int32, (C // SUBLANES, SUBLANES, 1), 1)

        def tile_row(i):
            return jnp.broadcast_to(G3[:, i:i + 1, :], G3.shape)
        for li, b in enumerate(HG_LEVELS):
            if b >= SUBLANES:
                pieces = []
                for j in range(C // b):
                    ref = G[(j // 2) * 2 * b + b - 1:(j // 2) * 2 * b + b, :]
                    blk = slice(j * b, (j + 1) * b)
                    if j % 2 == 1:
                        pieces.append(q[blk] * jnp.exp2(G[blk] - ref))
                    else:
                        pieces.append(k[blk] * jnp.exp2(ref - G[blk]))
                z = jnp.concatenate(pieces, axis=0)
            elif b > 1:
                ref = tile_row(3) if b == 4 else jnp.where(sub8 < 4, tile_row(1), tile_row(5))
                e = -jnp.abs(G3 - ref)
                z = jnp.where(odd_row[b], q, k) * jnp.exp2(e.reshape(C, W))
            else:
                z = jnp.where(odd_row[b], q * jnp.exp2(lf), k)
            z = z.astype(BF16)
            for hh in range(HG_HEADS):
                sl = slice(hh * HG_DK, (hh + 1) * HG_DK)
                a = _dot_nt(z[:, sl], z[:, sl]) * mk_ref[li]
                acc[hh] = a if li == 0 else acc[hh] + a
        g_last = G[C - 1:C, :]
        qg = (q * jnp.exp2(G)).astype(BF16)
        kh = (k * jnp.exp2(g_last - G)).astype(BF16)
        qk = q * k
        dec = jnp.exp2(g_last)
        eye = mk_ref[len(HG_LEVELS)]
        outs = []
        for hh in range(HG_HEADS):
            sl = slice(hh * HG_DK, (hh + 1) * HG_DK)
            vh = v_s[slot, rows, sl]
            diag = jnp.sum(qk[:, sl], axis=-1, keepdims=True)
            a = (acc[hh] + diag * eye).astype(BF16)
            st = st_ref[bi * HG_HEADS + hh]
            o = _dot(a, vh) + _dot_nt(qg[:, sl], st.astype(BF16))
            st_ref[bi * HG_HEADS + hh] = st * dec[:, sl] + _dot_tn(vh, kh[:, sl])
            on = o * lax.rsqrt(jnp.mean(o * o, axis=-1, keepdims=True) + EPS) * nw_ref[...]
            outs.append((on * g_s[slot, rows, sl]).astype(BF16))
        return jnp.concatenate(outs, axis=1)

    for j in range(n):
        nxt = h_ref[:, (j + 1) * C:(j + 2) * C, :] if j + 1 < n else hn_ref[...]
        project(nxt.reshape(nb * C, D), (j + 1) % 2)
        o = jnp.concatenate([one_chunk(bi, j % 2) for bi in range(nb)], axis=0)
        y = _dot(o, wo_ref[...]).astype(o_ref.dtype)
        for bi in range(nb):
            o_ref[bi, j * C:(j + 1) * C, :] = y[bi * C:(bi + 1) * C]


def _hgrn2(h, wq, wf, wi, wg, wo, lb, norm_w, ts, nb):
    B, S, D = h.shape
    W = HG_HEADS * HG_DK
    C = CHUNK
    n = ts // C
    assert n % 2 == 0
    R = nb * C
    cm, mk = _hg_level_matrices()
    tile = lambda w: pl.BlockSpec((nb, ts, w), lambda b, s: (b, s, 0))
    nxt = pl.BlockSpec((nb, C, D), lambda b, s: (b, jnp.minimum((s + 1) * n, S // C - 1), 0))
    return pl.pallas_call(
        _hg_kernel,
        grid=(B // nb, S // ts),
        in_specs=[tile(D), nxt] + [_const_spec((D, W))] * 4 + [_const_spec((W, D)), _const_spec((1, W)),
                  _const_spec((1, HG_DK)), _const_spec(cm.shape), _const_spec(mk.shape)],
        out_specs=tile(D),
        out_shape=jax.ShapeDtypeStruct((B, S, D), BF16),
        scratch_shapes=[pltpu.VMEM((nb * HG_HEADS, HG_DK, HG_DK), F32),
                        pltpu.VMEM((2, R, W), F32), pltpu.VMEM((2, R, W), F32), pltpu.VMEM((2, R, W), BF16),
                        pltpu.VMEM((2, R, W), F32), pltpu.VMEM((2, R, W), F32)],
        compiler_params=_params(2),
        name="hgrn2",
    )(h, h, wq, wf, wi, wg, wo, lb.reshape(1, W), norm_w.reshape(1, HG_DK), cm, mk)


def _softplus(x):
    return jnp.maximum(x, 0.0) + jnp.log(1.0 + jnp.exp(-jnp.abs(x)))


def _ssd_kernel(h_ref, hn_ref, wz_ref, wx_ref, wdt_ref, wo_ref, cw_ref, cb_ref, dtb_ref, a_ref, dl_ref, nw_ref,
                tril_ref, negm_ref, ex_ref, o_ref,
                st_ref, halo_s, xc_s, z_s, dt_s):
    nb, ts, D = h_ref.shape
    C = CHUNK
    n = ts // C
    inner = SSD_HEADS * SSD_P
    gw = SSD_GROUPS * SSD_N
    gp = SSD_HPG * SSD_P
    halo = SUBLANES

    def project(hc, slot):
        z_s[slot] = _silu(_dot(hc, wz_ref[...]))
        dt_s[slot] = _softplus(_dot(hc, wdt_ref[...]) + dtb_ref[...])
        u = _dot(hc, wx_ref[...])
        for bi in range(nb):
            ub = u[bi * C:(bi + 1) * C]
            xall = jnp.concatenate([halo_s[bi], ub], axis=0)
            acc = cw_ref[0:1, :] * xall
            for t in range(1, SSD_CONV):
                acc = cw_ref[t:t + 1, :] * xall + pltpu.roll(acc, 1, 0)
            xc_s[slot, bi * C:(bi + 1) * C, :] = _silu(acc[halo:halo + C] + cb_ref[...])
            halo_s[bi] = ub[C - halo:C]

    @pl.when(pl.program_id(1) == 0)
    def _():
        st_ref[...] = jnp.zeros_like(st_ref)
        halo_s[...] = jnp.zeros_like(halo_s)
        project(h_ref[:, 0:C, :].reshape(nb * C, D), 0)

    low_half = lax.broadcasted_iota(jnp.int32, (C, LANES), 1) < SSD_P

    def one_chunk(bi, slot):
        rows = slice(bi * C, (bi + 1) * C)
        dt = dt_s[slot, rows, :]
        da = dt * a_ref[...]
        cum = jnp.dot(tril_ref[...], da, preferred_element_type=F32, precision=lax.Precision.HIGHEST) * LOG2E
        cum_t = cum.T
        row_t = (cum - jnp.log(dt) * LOG2E).T
        cum_last = cum[C - 1:C, :]
        wfac = dt * jnp.exp2(cum_last - cum)
        w_hi = wfac.astype(BF16)
        w_lo = (wfac - w_hi.astype(F32)).astype(BF16)
        wexp = _dot(w_hi, ex_ref[...]) + _dot(w_lo, ex_ref[...])
        dl = jnp.broadcast_to(jnp.exp2(cum_last), (SUBLANES, LANES))
        d_hi = dl.astype(BF16)
        d_lo = (dl - d_hi.astype(F32)).astype(BF16)
        dexp = (_dot(d_hi, ex_ref[...]) + _dot(d_lo, ex_ref[...]))[0:1, :]
        ypairs = []
        for g in range(SSD_GROUPS):
            bg = xc_s[slot, rows, inner + g * SSD_N:inner + (g + 1) * SSD_N]
            cg = xc_s[slot, rows, inner + gw + g * SSD_N:inner + gw + (g + 1) * SSD_N]
            bgb = bg.astype(BF16)
            cb = _dot_nt(cg.astype(BF16), bgb)
            st = st_ref[bi * SSD_GROUPS + g]
            stb = st.astype(BF16)
            xg = xc_s[slot, rows, g * gp:(g + 1) * gp]
            xgb = xg.astype(BF16)
            for jp in range(SSD_HPG // 2):
                lsl = slice(jp * LANES, (jp + 1) * LANES)
                rhs = jnp.concatenate([xgb[:, lsl], stb[:, lsl]], axis=0)
                ys = []
                for hd in (g * SSD_HPG + 2 * jp, g * SSD_HPG + 2 * jp + 1):
                    colb = jnp.broadcast_to(cum[:, hd:hd + 1], (C, C))
                    w = cb * jnp.exp2(colb - row_t[hd:hd + 1, :] + negm_ref[...])
                    ce = cg * jnp.exp2(colb)
                    ys.append(_dot(jnp.concatenate([w.astype(BF16), ce.astype(BF16)], axis=1), rhs))
                ypairs.append(jnp.where(low_half, ys[0], ys[1]))
            gsl = slice(g * gp, (g + 1) * gp)
            st_ref[bi * SSD_GROUPS + g] = st * dexp[:, gsl] + _dot_tn(bgb, (xg * wexp[:, gsl]).astype(BF16))
        gi = inner // SSD_GROUPS
        ppg = gi // LANES
        outs = []
        for g in range(SSD_GROUPS):
            sl = slice(g * gi, (g + 1) * gi)
            y = jnp.concatenate(ypairs[g * ppg:(g + 1) * ppg], axis=1)
            seg = (y + dl_ref[:, sl] * xc_s[slot, rows, sl]) * z_s[slot, rows, sl]
            nrm = seg * lax.rsqrt(jnp.mean(seg * seg, axis=-1, keepdims=True) + EPS)
            outs.append((nrm * nw_ref[:, sl]).astype(BF16))
        return jnp.concatenate(outs, axis=1)

    for j in range(n):
        nxt = h_ref[:, (j + 1) * C:(j + 2) * C, :] if j + 1 < n else hn_ref[...]
        project(nxt.reshape(nb * C, D), (j + 1) % 2)
        o = jnp.concatenate([one_chunk(bi, j % 2) for bi in range(nb)], axis=0)
        y = _dot(o, wo_ref[...]).astype(o_ref.dtype)
        for bi in range(nb):
            o_ref[bi, j * C:(j + 1) * C, :] = y[bi * C:(bi + 1) * C]


def _ssd(h, wz, wx, wdt, wo, conv_w, conv_b, dt_bias, a_log, d_skip, norm_w, ts, nb):
    B, S, D = h.shape
    C = CHUNK
    n = ts // C
    assert n % 2 == 0
    inner = SSD_HEADS * SSD_P
    cdim = inner + 2 * SSD_GROUPS * SSD_N
    pad = LANES - SSD_HEADS
    wdt_p = jnp.pad(wdt, ((0, 0), (0, pad)))
    dtb = jnp.pad(dt_bias.astype(F32), (0, pad)).reshape(1, LANES)
    a = jnp.pad(-jnp.exp(a_log.astype(F32)), (0, pad)).reshape(1, LANES)
    dl = jnp.repeat(d_skip.astype(F32), SSD_P).reshape(1, inner)
    r = jnp.arange(C)
    tril = (r[:, None] >= r[None, :]).astype(F32)
    negm = jnp.where(r[:, None] >= r[None, :], 0.0, NEG_BIG).astype(F32)
    ex = (jnp.arange(LANES)[:, None] == (jnp.arange(inner)[None, :] // SSD_P)).astype(BF16)
    tile = lambda w: pl.BlockSpec((nb, ts, w), lambda b, s: (b, s, 0))
    nxt = pl.BlockSpec((nb, C, D), lambda b, s: (b, jnp.minimum((s + 1) * n, S // C - 1), 0))
    R = nb * C
    return pl.pallas_call(
        _ssd_kernel,
        grid=(B // nb, S // ts),
        in_specs=[tile(D), nxt, _const_spec((D, inner)), _const_spec((D, cdim)), _const_spec((D, LANES)),
                  _const_spec((inner, D)), _const_spec((SSD_CONV, cdim)), _const_spec((1, cdim)),
                  _const_spec((1, LANES)), _const_spec((1, LANES)), _const_spec((1, inner)),
                  _const_spec((1, inner)), _const_spec((C, C)), _const_spec((C, C)),
                  _const_spec((LANES, inner))],
        out_specs=tile(D),
        out_shape=jax.ShapeDtypeStruct((B, S, D), BF16),
        scratch_shapes=[pltpu.VMEM((nb * SSD_GROUPS, SSD_N, SSD_HPG * SSD_P), F32),
                        pltpu.VMEM((nb, SUBLANES, cdim), F32), pltpu.VMEM((2, R, cdim), F32),
                        pltpu.VMEM((2, R, inner), F32), pltpu.VMEM((2, R, LANES), F32)],
        compiler_params=_params(2),
        name="ssd",
    )(h, h, wz, wx, wdt_p, wo, conv_w.astype(F32), conv_b.astype(F32).reshape(1, cdim), dtb, a, dl,
      norm_w.astype(F32).reshape(1, inner), tril, negm, ex)


def _merge_kernel(x_ref, h_ref, yr_ref, yh_ref, ys_ref, wgl_ref, wout_ref, mod_ref, nw_ref,
                  wr_hi_ref, wr_lo_ref, br_ref, ut_ref, x1_ref, h2_ref, rf_ref, ri_ref, cnt_ref, run_ref):
    D = x_ref.shape[1]
    ts = x_ref.shape[0]

    @pl.when((pl.program_id(0) == 0) & (pl.program_id(1) == 0))
    def _():
        run_ref[...] = jnp.zeros_like(run_ref)

    h = h_ref[...]
    gates = _sigmoid(_dot(h, wgl_ref[...]))
    m = (gates[:, 0:D] * yr_ref[...].astype(F32) + gates[:, D:2 * D] * yh_ref[...].astype(F32)
         + gates[:, 2 * D:3 * D] * ys_ref[...].astype(F32))
    y = _dot(m.astype(BF16), wout_ref[...])
    x1 = x_ref[...] + mod_ref[2:3, :] * y
    x1_ref[...] = x1
    h2 = _rms_mod(x1, nw_ref[...], mod_ref[4:5, :], mod_ref[3:4, :])
    h2_ref[...] = h2
    h_hi = h2.astype(BF16)
    h_lo = (h2 - h_hi.astype(F32)).astype(BF16)
    lg = (_dot_nt(wr_hi_ref[...], h_hi) + _dot_nt(wr_hi_ref[...], h_lo) + _dot_nt(wr_lo_ref[...], h_hi)
          + br_ref[...])
    gl = lg[0:MOE_GROUPS]
    gmax = jnp.max(gl, axis=0, keepdims=True)
    gi = lax.broadcasted_iota(jnp.int32, gl.shape, 0)
    gtop = jnp.min(jnp.where(gl == gmax, gi, MOE_GROUPS), axis=0, keepdims=True)
    p_g = 1.0 / jnp.sum(jnp.exp(gl - gmax), axis=0, keepdims=True)
    el = jnp.zeros((MOE_PER_GROUP, ts), F32)
    for g in range(MOE_GROUPS):
        lo = SUBLANES + g * MOE_PER_GROUP
        el = jnp.where(gtop == g, lg[lo:lo + MOE_PER_GROUP], el)
    ei = lax.broadcasted_iota(jnp.int32, el.shape, 0)
    m1 = jnp.max(el, axis=0, keepdims=True)
    i1 = jnp.min(jnp.where(el == m1, ei, MOE_PER_GROUP), axis=0, keepdims=True)
    el2 = jnp.where(ei == i1, -jnp.inf, el)
    m2 = jnp.max(el2, axis=0, keepdims=True)
    i2 = jnp.min(jnp.where(el2 == m2, ei, MOE_PER_GROUP), axis=0, keepdims=True)
    e21 = jnp.exp(m2 - m1)
    den = 1.0 + e21
    w1 = (1.0 / den) * p_g
    w2 = (e21 / den) * p_g
    zf = jnp.zeros((SUBLANES - 2, ts), F32)
    rf_ref[...] = jnp.concatenate([w1, w2, zf], axis=0)
    e1 = gtop * MOE_PER_GROUP + i1
    e2 = gtop * MOE_PER_GROUP + i2
    ex = lax.broadcasted_iota(jnp.int32, (MOE_EXPERTS, ts), 0)
    oh1 = jnp.where(ex == e1, 1.0, 0.0)
    oh2 = jnp.where(ex == e2, 1.0, 0.0)
    p1 = _dot(oh1.astype(BF16), ut_ref[...])
    p2 = _dot(oh2.astype(BF16), ut_ref[...])
    c1 = p1[:, ts - 1:ts]
    c2 = p2[:, ts - 1:ts]
    run = run_ref[...]
    base = run[:, 0:1]
    r1 = jnp.sum(oh1 * (p1 - 1.0 + base), axis=0, keepdims=True)
    r2 = jnp.sum(oh2 * (p2 - 1.0 + base + c1), axis=0, keepdims=True)
    run = run + (c1 + c2)
    run_ref[...] = run
    cnt_ref[...] = run
    ri_ref[...] = jnp.concatenate([e1, e2, r1.astype(jnp.int32), r2.astype(jnp.int32),
                                   jnp.zeros((SUBLANES - 4, ts), jnp.int32)], axis=0)


def _merge(x, h, y_ret, y_hg, y_ssd, wgl, wout, mod_l, norm_w, wr_hi, wr_lo, br, ts):
    B, S, D = x.shape
    ns = S // ts
    R = wr_hi.shape[0]
    tile = pl.BlockSpec((None, ts, D), lambda b, s: (b, s, 0))
    rt = pl.BlockSpec((SUBLANES, ts), lambda b, s: (0, b * ns + s))
    r = jnp.arange(ts)
    ut = (r[:, None] <= r[None, :]).astype(BF16)
    return pl.pallas_call(
        _merge_kernel,
        grid=(B, ns),
        in_specs=[tile, tile, tile, tile, tile, _const_spec((D, 3 * D)), _const_spec((D, D)),
                  pl.BlockSpec((None, 6, D), lambda b, s: (b, 0, 0)), _const_spec((1, D)),
                  _const_spec((R, D)), _const_spec((R, D)), _const_spec((R, 1)), _const_spec((ts, ts))],
        out_specs=[tile, tile, rt, rt, pl.BlockSpec((MOE_EXPERTS, LANES), lambda b, s: (0, 0))],
        out_shape=[jax.ShapeDtypeStruct((B, S, D), F32), jax.ShapeDtypeStruct((B, S, D), F32),
                   jax.ShapeDtypeStruct((SUBLANES, B * S), F32), jax.ShapeDtypeStruct((SUBLANES, B * S), jnp.int32),
                   jax.ShapeDtypeStruct((MOE_EXPERTS, LANES), F32)],
        scratch_shapes=[pltpu.VMEM((MOE_EXPERTS, LANES), F32)],
        compiler_params=_params(2),
        name="merge_router",
    )(x, h, y_ret, y_hg, y_ssd, wgl, wout, mod_l, norm_w.reshape(1, D), wr_hi, wr_lo, br, ut)


def _row_copy(src, src_row, dst, dst_row, sem):
    return pltpu.make_async_copy(src.at[pl.ds(src_row, 1), :], dst.at[pl.ds(dst_row, 1), :], sem)


DISPATCH_TILE = 1024
DMA_UNROLL = 256


def _dispatch_kernel(pend_ref, plen_ref, nact_ref, dest_ref, h2_ref, xs_hbm, zbuf, sem, zsem):
    i = pl.program_id(0)
    td = dest_ref.shape[2]
    BLK = MOE_BLOCK
    nb = xs_hbm.shape[0] // BLK

    def zero_fill(wait):
        def go(start, size):
            cp = pltpu.make_async_copy(zbuf.at[pl.ds(0, size), :], xs_hbm.at[pl.ds(start, size), :], zsem)
            if wait:
                cp.wait()
            else:
                cp.start()

        for e in range(MOE_EXPERTS):
            ln = plen_ref[e]
            end = pend_ref[e]
            size = BLK // 2
            while size >= SUBLANES:
                @pl.when((ln & size) != 0)
                def _(size=size, ln=ln, end=end):
                    go(pl.multiple_of(end - (ln & ~(size - 1)), size), size)
                size //= 2
            for j in range(SUBLANES - 1):
                @pl.when(j < (ln & (SUBLANES - 1)))
                def _(j=j, ln=ln, end=end):
                    go(end - ln + j, 1)
        for j in range(MOE_EXPERTS):
            @pl.when(nact_ref[0] + j < nb)
            def _(j=j):
                go(pl.multiple_of((nact_ref[0] + j) * BLK, BLK), BLK)

    @pl.when(i == 0)
    def _():
        zbuf[...] = jnp.zeros_like(zbuf)
        zero_fill(False)

    def issue(j, c):
        base = pl.multiple_of(j * DMA_UNROLL, DMA_UNROLL)
        src = h2_ref.at[pl.ds(base, DMA_UNROLL), :]
        for u in range(DMA_UNROLL):
            for k in range(2):
                _row_copy(src, u, xs_hbm, dest_ref[0, k, base + u], sem).start(priority=k)
        return c

    lax.fori_loop(0, td // DMA_UNROLL, issue, 0)
    for k in range(2):
        pltpu.make_async_copy(h2_ref, xs_hbm.at[pl.ds(0, td), :], sem).wait()

    @pl.when(i == 0)
    def _():
        zero_fill(True)


def _dispatch(h2, dest, pad_end, pad_len, nactive, nrows):
    T, D = h2.shape
    td = min(DISPATCH_TILE, T)
    nt = T // td
    dest3 = dest.reshape(2, nt, td).transpose(1, 0, 2)
    grid_spec = pltpu.PrefetchScalarGridSpec(
        num_scalar_prefetch=3,
        grid=(nt,),
        in_specs=[pl.BlockSpec((1, 2, td), lambda i, *_: (i, 0, 0), memory_space=pltpu.SMEM),
                  pl.BlockSpec((td, D), lambda i, *_: (i, 0))],
        out_specs=pl.BlockSpec(memory_space=pl.ANY),
        scratch_shapes=[pltpu.VMEM((MOE_BLOCK, D), F32), pltpu.SemaphoreType.DMA(()),
                        pltpu.SemaphoreType.DMA(())])
    return pl.pallas_call(
        _dispatch_kernel,
        grid_spec=grid_spec,
        out_shape=jax.ShapeDtypeStruct((nrows, D), F32),
        compiler_params=_params(1),
        name="dispatch",
    )(pad_end, pad_len, nactive, dest3, h2)


def _expert_kernel(be_ref, na_ref, x_ref, w1_ref, w3_ref, w2_ref, y_ref, w1_s, w3_s, w2_s):
    i = pl.program_id(0)

    @pl.when((i == 0) | (be_ref[i] != be_ref[jnp.maximum(i - 1, 0)]))
    def _():
        w1_s[...] = w1_ref[...].astype(BF16)
        w3_s[...] = w3_ref[...].astype(BF16)
        w2_s[...] = w2_ref[...].astype(BF16)

    @pl.when(i < na_ref[0])
    def _():
        x = x_ref[...].astype(BF16)
        a = _dot(x, w1_s[...])
        b = _dot(x, w3_s[...])
        y_ref[...] = _dot((_silu(a) * b).astype(BF16), w2_s[...])

    @pl.when(i >= na_ref[0])
    def _():
        y_ref[...] = jnp.zeros_like(y_ref)


def _experts(xs, block_e, nactive, w1, w3, w2, layer):
    nrows, D = xs.shape
    BLK = MOE_BLOCK
    nb = nrows // BLK
    ff = w1.shape[3]
    wsel = lambda i, be, na: (layer, be[i], 0, 0)
    grid_spec = pltpu.PrefetchScalarGridSpec(
        num_scalar_prefetch=2,
        grid=(nb,),
        in_specs=[pl.BlockSpec((BLK, D), lambda i, be, na: (jnp.minimum(i, na[0] - 1), 0)),
                  pl.BlockSpec((None, None, D, ff), wsel), pl.BlockSpec((None, None, D, ff), wsel),
                  pl.BlockSpec((None, None, ff, D), wsel)],
        out_specs=pl.BlockSpec((BLK, D), lambda i, be, na: (i, 0)),
        scratch_shapes=[pltpu.VMEM((D, ff), BF16), pltpu.VMEM((D, ff), BF16), pltpu.VMEM((ff, D), BF16)])
    return pl.pallas_call(
        _expert_kernel,
        grid_spec=grid_spec,
        out_shape=jax.ShapeDtypeStruct((nrows, D), F32),
        compiler_params=_params(1),
        name="experts",
    )(block_e, nactive, xs, w1, w3, w2)


def _dispatch_plan(route_i, counts_f, T):
    BLK = MOE_BLOCK
    nb = -(-(2 * T) // BLK) + MOE_EXPERTS
    counts = counts_f[:, 0].astype(jnp.int32)
    padded = ((counts + BLK - 1) // BLK) * BLK
    pad_ends = jnp.cumsum(padded)
    pad_starts = pad_ends - padded
    e = route_i[0:2]
    rank = route_i[2:4]
    sel = e[:, :, None] == jnp.arange(MOE_EXPERTS, dtype=jnp.int32)[None, None, :]
    dest = rank + jnp.sum(jnp.where(sel, pad_starts[None, None, :], 0), axis=-1)
    nactive = (pad_ends[-1] // BLK).astype(jnp.int32)
    blk_id = jnp.minimum(jnp.arange(nb, dtype=jnp.int32), nactive - 1)
    block_e = jnp.sum((blk_id[:, None] * BLK >= pad_ends[None, :]).astype(jnp.int32), axis=1)
    block_e = jnp.minimum(block_e, MOE_EXPERTS - 1)
    return dest, block_e, pad_ends, padded - counts, nactive.reshape(1), nb * BLK


COMBINE_TILE = 256


def _combine_kernel(final, dcur_ref, dnxt_ref, x_ref, rw_ref, mod_ref, nw_ref, modn_ref, ys_hbm, *rest):
    if final:
        o_ref, gbuf, sem = rest
    else:
        x2_ref, h_ref, gbuf, sem = rest
    i = pl.program_id(0)
    n = pl.num_programs(0)
    ts = x_ref.shape[0]
    slot = i % 2

    def issue(dref, sl):
        for r in range(ts):
            for k in range(2):
                _row_copy(ys_hbm, dref[0, k, r], gbuf.at[sl, k], r, sem.at[sl]).start(priority=k)

    @pl.when(i == 0)
    def _():
        issue(dcur_ref, 0)

    @pl.when(i + 1 < n)
    def _():
        issue(dnxt_ref, 1 - slot)

    for k in range(2):
        pltpu.make_async_copy(ys_hbm.at[pl.ds(0, ts), :], gbuf.at[slot, k], sem.at[slot]).wait()
    y = rw_ref[:, 0:1] * gbuf[slot, 0] + rw_ref[:, 1:2] * gbuf[slot, 1]
    x2 = x_ref[...] + mod_ref[5:6, :] * y
    if final:
        o_ref[...] = (x2 * lax.rsqrt(jnp.mean(x2 * x2, axis=-1, keepdims=True) + EPS) * nw_ref[...]).astype(o_ref.dtype)
    else:
        x2_ref[...] = x2
        h_ref[...] = _rms_mod(x2, nw_ref[...], modn_ref[1:2, :], modn_ref[0:1, :]).astype(h_ref.dtype)


def _combine(x1, ys, dest, rw, mod_l, norm_w, mod_next, final, out_dtype):
    B, S, D = x1.shape
    T = B * S
    ts = min(COMBINE_TILE, S)
    nsb = S // ts
    nt = T // ts
    dest3 = dest.reshape(2, nt, ts).transpose(1, 0, 2)
    tile = pl.BlockSpec((ts, D), lambda i: (i, 0))
    modspec = pl.BlockSpec((None, 6, D), lambda i: (i // nsb, 0, 0))
    if final:
        out_specs = tile
        out_shape = jax.ShapeDtypeStruct((T, D), out_dtype)
    else:
        out_specs = [tile, tile]
        out_shape = [jax.ShapeDtypeStruct((T, D), F32), jax.ShapeDtypeStruct((T, D), BF16)]
    out = pl.pallas_call(
        functools.partial(_combine_kernel, final),
        grid=(nt,),
        in_specs=[pl.BlockSpec((1, 2, ts), lambda i: (i, 0, 0), memory_space=pltpu.SMEM),
                  pl.BlockSpec((1, 2, ts), lambda i: (jnp.minimum(i + 1, nt - 1), 0, 0), memory_space=pltpu.SMEM),
                  tile, pl.BlockSpec((ts, SUBLANES), lambda i: (i, 0)), modspec,
                  pl.BlockSpec((1, D), lambda i: (0, 0)), modspec,
                  pl.BlockSpec(memory_space=pl.ANY)],
        out_specs=out_specs,
        out_shape=out_shape,
        scratch_shapes=[pltpu.VMEM((2, 2, ts, D), F32), pltpu.SemaphoreType.DMA((2,))],
        compiler_params=_params(1),
        name="combine_final" if final else "combine_prenorm",
    )(dest3, dest3, x1.reshape(T, D), rw, mod_l, norm_w.reshape(1, D), mod_next, ys)
    if final:
        return out.reshape(B, S, D)
    return out[0].reshape(B, S, D), out[1].reshape(B, S, D)


def _seq_tile(S, want):
    ts = min(want, S)
    assert S % ts == 0 and ts % CHUNK == 0
    return ts


def kernel(x, c, positions, w_ada, b_ada, norm_mix_w, w_in, ssd_conv_w, ssd_conv_b, ssd_dt_bias, ssd_a_log,
           ssd_d, ssd_norm_w, hg_lb, hg_norm_w, w_ret_o, w_hg_o, w_ssd_o, w_out, norm_ffn_w, moe_w_group,
           moe_b_group, moe_w_expert, moe_b_expert, moe_w1, moe_w3, moe_w2, final_norm_w):
    B, S, D = x.shape
    L = w_ada.shape[0]
    T = B * S
    ts_el = _seq_tile(S, 512)
    ts_mix = _seq_tile(S, 256)
    nb_mix = 2 if B % 2 == 0 else 1

    mod = _adaln(c, w_ada, b_ada)
    cos, sin = _rope_tables(positions, ts_el)
    lb_soft = jax.nn.softmax(hg_lb.astype(F32), axis=0)
    lb_all = jnp.cumsum(lb_soft, axis=0) - lb_soft[0]

    qk = RET_HEADS * RET_DK
    vw = RET_HEADS * RET_DV
    hgw = HG_HEADS * HG_DK
    inner = SSD_HEADS * SSD_P
    cdim = inner + 2 * SSD_GROUPS * SSD_N
    sizes = (qk, qk, vw, vw, hgw, hgw, hgw, hgw, inner, cdim, SSD_HEADS, 3 * D)
    offs = [0]
    for n in sizes:
        offs.append(offs[-1] + n)

    xcur = x.astype(F32)
    h = _prenorm(xcur, norm_mix_w[0].astype(F32), mod[0], ts_el)
    for l in range(L):
        wl = w_in[l].astype(BF16)
        cols = [wl[:, offs[i]:offs[i + 1]] for i in range(len(sizes))]
        y_ret = _retention(h, cos, sin, cols[0], cols[1], cols[2], cols[3], w_ret_o[l].astype(BF16), ts_mix,
                           nb_mix)
        y_hg = _hgrn2(h, cols[4], cols[5], cols[6], cols[7], w_hg_o[l].astype(BF16), lb_all[l],
                      hg_norm_w[l].astype(F32), ts_mix, nb_mix)
        y_ssd = _ssd(h, cols[8], cols[9], cols[10], w_ssd_o[l].astype(BF16), ssd_conv_w[l], ssd_conv_b[l],
                     ssd_dt_bias[l], ssd_a_log[l], ssd_d[l], ssd_norm_w[l], ts_mix, nb_mix)
        wr = jnp.zeros((SUBLANES + MOE_EXPERTS, D), F32)
        wr = wr.at[0:MOE_GROUPS].set(moe_w_group[l].astype(F32).T).at[SUBLANES:].set(moe_w_expert[l].astype(F32).T)
        br = jnp.zeros((SUBLANES + MOE_EXPERTS, 1), F32)
        br = br.at[0:MOE_GROUPS, 0].set(moe_b_group[l].astype(F32)).at[SUBLANES:, 0].set(moe_b_expert[l].astype(F32))
        wr_hi = wr.astype(BF16)
        wr_lo = (wr - wr_hi.astype(F32)).astype(BF16)
        x1, h2, route_w, route_i, counts = _merge(xcur, h, y_ret, y_hg, y_ssd, cols[11], w_out[l].astype(BF16),
                                                  mod[l], norm_ffn_w[l].astype(F32), wr_hi, wr_lo, br, ts_el)
        dest, block_e, pad_end, pad_len, nactive, nrows = _dispatch_plan(route_i, counts, T)
        xs = _dispatch(h2.reshape(T, D), dest, pad_end, pad_len, nactive, nrows)
        ys = _experts(xs, block_e, nactive, moe_w1, moe_w3, moe_w2, l)
        rw = route_w.T
        if l + 1 < L:
            xcur, h = _combine(x1, ys, dest, rw, mod[l], norm_mix_w[l + 1].astype(F32), mod[l + 1], False, x.dtype)
        else:
            out = _combine(x1, ys, dest, rw, mod[l], final_norm_w.astype(F32), mod[l], True, x.dtype)
    return out
```

```python
import functools

import jax
import jax.numpy as jnp
from jax import lax
from jax.experimental import pallas as pl
from jax.experimental.pallas import tpu as pltpu

F32 = jnp.float32
BF16 = jnp.bfloat16

EPS = 1e-6
ROPE_BASE = 10000.0
RET_HEADS = 4
RET_DK = 128
RET_DV = 256
HG_HEADS = 8
HG_DK = 128
SSD_HEADS = 32
SSD_P = 64
SSD_GROUPS = 4
SSD_N = 128
SSD_HPG = SSD_HEADS // SSD_GROUPS
SSD_CONV = 4
MOE_GROUPS = 4
MOE_PER_GROUP = 8
MOE_EXPERTS = 32
MOE_BLOCK = 512
CHUNK = 128
LANES = 128
SUBLANES = 8
VMEM_LIMIT = 56 * 1024 * 1024
HG_LEVELS = (64, 32, 16, 8, 4, 2, 1)
LOG2E = 1.4426950408889634
NEG_BIG = -1e30


def _params(n_axes, vmem=VMEM_LIMIT):
    return pltpu.CompilerParams(dimension_semantics=("arbitrary",) * n_axes,
                                vmem_limit_bytes=vmem)


def _const_spec(shape):
    nd = len(shape)
    return pl.BlockSpec(shape, lambda *_: (0,) * nd, pipeline_mode=pl.Buffered(1))


def _sigmoid(x):
    return 0.5 * jnp.tanh(0.5 * x) + 0.5


def _silu(x):
    h = 0.5 * x
    return h * jnp.tanh(h) + h


def _dot(a, b):
    return jnp.dot(a, b, preferred_element_type=F32)


def _dot_nt(a, b):
    return lax.dot_general(a, b, (((1,), (1,)), ((), ())), preferred_element_type=F32)


def _dot_tn(a, b):
    return lax.dot_general(a, b, (((0,), (0,)), ((), ())), preferred_element_type=F32)


def _split3(x):
    hi = x.astype(BF16)
    r1 = x - hi.astype(F32)
    mid = r1.astype(BF16)
    lo = (r1 - mid.astype(F32)).astype(BF16)
    return hi, mid, lo


def _dot2(m_bf16, x_f32):
    hi = x_f32.astype(BF16)
    lo = (x_f32 - hi.astype(F32)).astype(BF16)
    return _dot(m_bf16, hi) + _dot(m_bf16, lo)


def _dot3(m_bf16, x_f32):
    hi, mid, lo = _split3(x_f32)
    return _dot(m_bf16, hi) + _dot(m_bf16, mid) + _dot(m_bf16, lo)


def _ada_kernel(c_ref, w_ref, b_ref, o_ref):
    c = c_ref[...]
    o_ref[...] = _dot(_silu(c).astype(BF16), w_ref[...].astype(BF16)) + b_ref[...]


def _adaln(c, w_ada, b_ada):
    L, D, N = w_ada.shape
    B = c.shape[0]
    cp = jnp.zeros((SUBLANES, D), F32).at[:B].set(c.astype(F32))
    tn = 1536
    out = pl.pallas_call(
        _ada_kernel,
        grid=(L, N // tn),
        in_specs=[pl.BlockSpec((SUBLANES, D), lambda l, n: (0, 0)),
                  pl.BlockSpec((None, D, tn), lambda l, n: (l, 0, n)),
                  pl.BlockSpec((None, 1, tn), lambda l, n: (l, 0, n))],
        out_specs=pl.BlockSpec((None, SUBLANES, tn), lambda l, n: (l, 0, n)),
        out_shape=jax.ShapeDtypeStruct((L, SUBLANES, N), F32),
        compiler_params=_params(2),
        name="adaln",
    )(cp, w_ada, b_ada.reshape(L, 1, N))
    return out[:, :B].reshape(L, B, 6, D)


def _rope_kernel(pos_ref, invf_ref, sign_ref, cos_ref, sin_ref):
    ang = pos_ref[...].astype(F32) * invf_ref[...]
    cos_ref[...] = jnp.cos(ang)
    sin_ref[...] = jnp.sin(ang) * sign_ref[...]


def _rope_tables(positions, ts):
    B, S = positions.shape
    half = RET_DK // 2
    inv_freq = ROPE_BASE ** (-jnp.linspace(0.0, 1.0, half, dtype=F32))
    invf = jnp.concatenate([inv_freq, inv_freq]).reshape(1, RET_DK)
    sign = jnp.concatenate([-jnp.ones((half,), F32), jnp.ones((half,), F32)]).reshape(1, RET_DK)
    spec = pl.BlockSpec((None, ts, RET_DK), lambda b, s: (b, s, 0))
    return pl.pallas_call(
        _rope_kernel,
        grid=(B, S // ts),
        in_specs=[pl.BlockSpec((None, ts, 1), lambda b, s: (b, s, 0)),
                  pl.BlockSpec((1, RET_DK), lambda b, s: (0, 0)),
                  pl.BlockSpec((1, RET_DK), lambda b, s: (0, 0))],
        out_specs=[spec, spec],
        out_shape=[jax.ShapeDtypeStruct((B, S, RET_DK), F32)] * 2,
        compiler_params=_params(2),
        name="rope_tables",
    )(positions.reshape(B, S, 1), invf, sign)


def _rms_mod(x, w, scale, shift):
    y = x * lax.rsqrt(jnp.mean(x * x, axis=-1, keepdims=True) + EPS)
    return (y * w) * (1.0 + scale) + shift


def _prenorm_kernel(x_ref, w_ref, mod_ref, o_ref):
    o_ref[...] = _rms_mod(x_ref[...], w_ref[...], mod_ref[1:2, :], mod_ref[0:1, :]).astype(o_ref.dtype)


def _prenorm(x, w, mod_l, ts):
    B, S, D = x.shape
    return pl.pallas_call(
        _prenorm_kernel,
        grid=(B, S // ts),
        in_specs=[pl.BlockSpec((None, ts, D), lambda b, s: (b, s, 0)),
                  pl.BlockSpec((1, D), lambda b, s: (0, 0)),
                  pl.BlockSpec((None, 6, D), lambda b, s: (b, 0, 0))],
        out_specs=pl.BlockSpec((None, ts, D), lambda b, s: (b, s, 0)),
        out_shape=jax.ShapeDtypeStruct((B, S, D), BF16),
        compiler_params=_params(2),
        name="prenorm",
    )(x, w.reshape(1, D), mod_l)


def _ret_kernel(h_ref, hn_ref, cos_ref, cosn_ref, sin_ref, sinn_ref, wq_ref, wk_ref, wv_ref, wg_ref, wo_ref,
                di_ref, dq_ref, dk_ref, dc_ref, o_ref,
                st_ref, q_s, k_s, v_s, g_s):
    nb, ts, D = h_ref.shape
    C = CHUNK
    n = ts // C
    R = nb * C
    scale = RET_DK ** -0.5

    def project(hc, cos, sin, slot):
        hc = hc.reshape(R, D)
        cos = cos.reshape(R, RET_DK)
        sin = sin.reshape(R, RET_DK)
        q = _dot(hc, wq_ref[...])
        k = _dot(hc, wk_ref[...])
        for hh in range(RET_HEADS):
            sl = slice(hh * RET_DK, (hh + 1) * RET_DK)
            qh = q[:, sl]
            kh = k[:, sl]
            q_s[slot, :, sl] = (qh * cos + pltpu.roll(qh, RET_DK // 2, 1) * sin).astype(BF16)
            k_s[slot, :, sl] = (kh * cos + pltpu.roll(kh, RET_DK // 2, 1) * sin) * scale
        v_s[slot] = _dot(hc, wv_ref[...]).astype(BF16)
        g_s[slot] = _silu(_dot(hc, wg_ref[...]))

    @pl.when(pl.program_id(1) == 0)
    def _():
        st_ref[...] = jnp.zeros_like(st_ref)
        project(h_ref[:, 0:C, :], cos_ref[:, 0:C, :], sin_ref[:, 0:C, :], 0)

    def one_chunk(bi, slot):
        rows = slice(bi * C, (bi + 1) * C)
        outs = []
        for hh in range(RET_HEADS):
            ksl = slice(hh * RET_DK, (hh + 1) * RET_DK)
            vsl = slice(hh * RET_DV, (hh + 1) * RET_DV)
            qc = q_s[slot, rows, ksl]
            kc = k_s[slot, rows, ksl]
            vc = v_s[slot, rows, vsl]
            st = st_ref[bi * RET_HEADS + hh]
            scores = _dot_nt(qc, kc.astype(BF16)) * di_ref[hh]
            o = _dot(scores.astype(BF16), vc) + _dot(qc, st.astype(BF16)) * dq_ref[hh]
            st_ref[bi * RET_HEADS + hh] = st * dc_ref[hh] + _dot_tn((kc * dk_ref[hh]).astype(BF16), vc)
            mu = jnp.mean(o, axis=-1, keepdims=True)
            d = o - mu
            var = jnp.mean(d * d, axis=-1, keepdims=True)
            outs.append((g_s[slot, rows, vsl] * (d * lax.rsqrt(var + EPS))).astype(BF16))
        return jnp.concatenate(outs, axis=1)

    for j in range(n):
        if j + 1 < n:
            nsl = slice((j + 1) * C, (j + 2) * C)
            project(h_ref[:, nsl, :], cos_ref[:, nsl, :], sin_ref[:, nsl, :], (j + 1) % 2)
        else:
            project(hn_ref[...], cosn_ref[...], sinn_ref[...], (j + 1) % 2)
        o = jnp.concatenate([one_chunk(bi, j % 2) for bi in range(nb)], axis=0)
        o_ref[:, j * C:(j + 1) * C, :] = _dot(o, wo_ref[...]).astype(o_ref.dtype).reshape(nb, C, D)


def _retention(h, cos, sin, wq, wk, wv, wg, wo, ts, nb):
    B, S, D = h.shape
    H, C = RET_HEADS, CHUNK
    log_gamma = jnp.log(1.0 - 2.0 ** (-5.0 - jnp.arange(H, dtype=F32)))
    idx = jnp.arange(C, dtype=F32)
    rel = idx[:, None] - idx[None, :]
    di = jnp.where(rel >= 0.0, jnp.exp(log_gamma[:, None, None] * jnp.maximum(rel, 0.0)), 0.0)
    dq = jnp.broadcast_to(jnp.exp(log_gamma[:, None] * (idx[None, :] + 1.0))[:, :, None], (H, C, RET_DV))
    dk = jnp.broadcast_to(jnp.exp(log_gamma[:, None] * (C - 1.0 - idx[None, :]))[:, :, None], (H, C, RET_DK))
    dc = jnp.broadcast_to(jnp.exp(log_gamma * C)[:, None, None], (H, 1, RET_DV))
    qk, vw = H * RET_DK, H * RET_DV
    n = ts // C
    assert n % 2 == 0
    tile = lambda w: pl.BlockSpec((nb, ts, w), lambda b, s: (b, s, 0))
    nxt = lambda w: pl.BlockSpec((nb, C, w), lambda b, s: (b, jnp.minimum((s + 1) * n, S // C - 1), 0))
    R = nb * C
    return pl.pallas_call(
        _ret_kernel,
        grid=(B // nb, S // ts),
        in_specs=[tile(D), nxt(D), tile(RET_DK), nxt(RET_DK), tile(RET_DK), nxt(RET_DK),
                  _const_spec((D, qk)), _const_spec((D, qk)), _const_spec((D, vw)), _const_spec((D, vw)),
                  _const_spec((vw, D)),
                  _const_spec((H, C, C)), _const_spec((H, C, RET_DV)), _const_spec((H, C, RET_DK)),
                  _const_spec((H, 1, RET_DV))],
        out_specs=tile(D),
        out_shape=jax.ShapeDtypeStruct((B, S, D), BF16),
        scratch_shapes=[pltpu.VMEM((nb * H, RET_DK, RET_DV), F32),
                        pltpu.VMEM((2, R, qk), BF16), pltpu.VMEM((2, R, qk), F32),
                        pltpu.VMEM((2, R, vw), BF16), pltpu.VMEM((2, R, vw), F32)],
        compiler_params=_params(2),
        name="retention",
    )(h, h, cos, cos, sin, sin, wq, wk, wv, wg, wo, di, dq, dk, dc)


def _hg_level_matrices():
    C = CHUNK
    r = jnp.arange(C)
    stack = (r[:, None] >= r[None, :]).astype(BF16)
    masks = []
    for b in HG_LEVELS:
        same = (r[:, None] // (2 * b)) == (r[None, :] // (2 * b))
        masks.append(same & ((r[:, None] & b) != 0) & ((r[None, :] & b) == 0))
    masks.append(r[:, None] == r[None, :])
    return stack, jnp.stack(masks).astype(F32)


def _hg_kernel(h_ref, hn_ref, wq_ref, wf_ref, wi_ref, wg_ref, wo_ref, lb_ref, nw_ref, cm_ref, mk_ref, o_ref,
               st_ref, q_s, k_s, v_s, lf_s, g_s):
    nb, ts, D = h_ref.shape
    C = CHUNK
    W = HG_HEADS * HG_DK
    n = ts // C
    lb = lb_ref[...]

    def project(hc, slot):
        q_s[slot] = _silu(_dot(hc, wq_ref[...]))
        hf = _dot(hc, wf_ref[...])
        lf_s[slot] = jnp.log(lb + (1.0 - lb) * _sigmoid(hf)) * LOG2E
        k_s[slot] = (1.0 - lb) * _sigmoid(-hf)
        v_s[slot] = _dot(hc, wi_ref[...]).astype(BF16)
        g_s[slot] = _silu(_dot(hc, wg_ref[...]))

    @pl.when(pl.program_id(1) == 0)
    def _():
        st_ref[...] = jnp.zeros_like(st_ref)
        project(h_ref[:, 0:C, :].reshape(nb * C, D), 0)

    odd_row = {b: (lax.broadcasted_iota(jnp.int32, (C, 1), 0) & b) != 0 for b in HG_LEVELS}

    def one_chunk(bi, slot):
        rows = slice(bi * C, (bi + 1) * C)
        lf = lf_s[slot, rows, :]
        q = q_s[slot, rows, :]
        k = k_s[slot, rows, :]
        G = _dot2(cm_ref[...], lf)
        acc = [None] * HG_HEADS
        G3 = G.reshape(C // SUBLANES, SUBLANES, W)
        sub8 = lax.broadcasted_iota(jnp.int32, (C // SUBLANES, SUBLANES, 1), 1)

        def tile_row(i):
            return jnp.broadcast_to(G3[:, i:i + 1, :], G3.shape)
        for li, b in enumerate(HG_LEVELS):
            if b >= SUBLANES:
                pieces = []
                for j in range(C // b):
                    ref = G[(j // 2) * 2 * b + b - 1:(j // 2) * 2 * b + b, :]
                    blk = slice(j * b, (j + 1) * b)
                    if j % 2 == 1:
                        pieces.append(q[blk] * jnp.exp2(G[blk] - ref))
                    else:
                        pieces.append(k[blk] * jnp.exp2(ref - G[blk]))
                z = jnp.concatenate(pieces, axis=0)
            elif b > 1:
                ref = tile_row(3) if b == 4 else jnp.where(sub8 < 4, tile_row(1), tile_row(5))
                e = -jnp.abs(G3 - ref)
                z = jnp.where(odd_row[b], q, k) * jnp.exp2(e.reshape(C, W))
            else:
                z = jnp.where(odd_row[b], q * jnp.exp2(lf), k)
            z = z.astype(BF16)
            for hh in range(HG_HEADS):
                sl = slice(hh * HG_DK, (hh + 1) * HG_DK)
                a = _dot_nt(z[:, sl], z[:, sl]) * mk_ref[li]
                acc[hh] = a if li == 0 else acc[hh] + a
        g_last = G[C - 1:C, :]
        qg = (q * jnp.exp2(G)).astype(BF16)
        kh = (k * jnp.exp2(g_last - G)).astype(BF16)
        qk = q * k
        dec = jnp.exp2(g_last)
        eye = mk_ref[len(HG_LEVELS)]
        outs = []
        for hh in range(HG_HEADS):
            sl = slice(hh * HG_DK, (hh + 1) * HG_DK)
            vh = v_s[slot, rows, sl]
            diag = jnp.sum(qk[:, sl], axis=-1, keepdims=True)
            a = (acc[hh] + diag * eye).astype(BF16)
            st = st_ref[bi * HG_HEADS + hh]
            o = _dot(a, vh) + _dot_nt(qg[:, sl], st.astype(BF16))
            st_ref[bi * HG_HEADS + hh] = st * dec[:, sl] + _dot_tn(vh, kh[:, sl])
            on = o * lax.rsqrt(jnp.mean(o * o, axis=-1, keepdims=True) + EPS) * nw_ref[...]
            outs.append((on * g_s[slot, rows, sl]).astype(BF16))
        return jnp.concatenate(outs, axis=1)

    for j in range(n):
        nxt = h_ref[:, (j + 1) * C:(j + 2) * C, :] if j + 1 < n else hn_ref[...]
        project(nxt.reshape(nb * C, D), (j + 1) % 2)
        o = jnp.concatenate([one_chunk(bi, j % 2) for bi in range(nb)], axis=0)
        o_ref[:, j * C:(j + 1) * C, :] = _dot(o, wo_ref[...]).astype(o_ref.dtype).reshape(nb, C, D)


def _hgrn2(h, wq, wf, wi, wg, wo, lb, norm_w, ts, nb):
    B, S, D = h.shape
    W = HG_HEADS * HG_DK
    C = CHUNK
    n = ts // C
    assert n % 2 == 0
    R = nb * C
    cm, mk = _hg_level_matrices()
    tile = lambda w: pl.BlockSpec((nb, ts, w), lambda b, s: (b, s, 0))
    nxt = pl.BlockSpec((nb, C, D), lambda b, s: (b, jnp.minimum((s + 1) * n, S // C - 1), 0))
    return pl.pallas_call(
        _hg_kernel,
        grid=(B // nb, S // ts),
        in_specs=[tile(D), nxt] + [_const_spec((D, W))] * 4 + [_const_spec((W, D)), _const_spec((1, W)),
                  _const_spec((1, HG_DK)), _const_spec(cm.shape), _const_spec(mk.shape)],
        out_specs=tile(D),
        out_shape=jax.ShapeDtypeStruct((B, S, D), BF16),
        scratch_shapes=[pltpu.VMEM((nb * HG_HEADS, HG_DK, HG_DK), F32),
                        pltpu.VMEM((2, R, W), F32), pltpu.VMEM((2, R, W), F32), pltpu.VMEM((2, R, W), BF16),
                        pltpu.VMEM((2, R, W), F32), pltpu.VMEM((2, R, W), F32)],
        compiler_params=_params(2),
        name="hgrn2",
    )(h, h, wq, wf, wi, wg, wo, lb.reshape(1, W), norm_w.reshape(1, HG_DK), cm, mk)


def _shift_rows(x, k, head):
    r, w = x.shape
    t = x.reshape(r // SUBLANES, SUBLANES, w)
    rot = pltpu.roll(t, k, 1)
    hrot = pltpu.roll(head.reshape(1, SUBLANES, w), k, 1)
    prev = jnp.concatenate([hrot, rot[:-1]], axis=0)
    sub = lax.broadcasted_iota(jnp.int32, (1, SUBLANES, 1), 1)
    return jnp.where(sub < k, prev, rot).reshape(r, w)


def _softplus(x):
    return jnp.maximum(x, 0.0) + jnp.log(1.0 + jnp.exp(-jnp.abs(x)))


def _ssd_kernel(h_ref, hn_ref, wz_ref, wx_ref, wdt_ref, wo_ref, cw_ref, cb_ref, dtb_ref, a_ref, dl_ref, nw_ref,
                tril_ref, negm_ref, ex_ref, o_ref,
                st_ref, halo_s, u_s, xc_s, z_s, dt_s):
    nb, ts, D = h_ref.shape
    C = CHUNK
    n = ts // C
    inner = SSD_HEADS * SSD_P
    gw = SSD_GROUPS * SSD_N
    gp = SSD_HPG * SSD_P
    halo = SUBLANES

    def project(hc, slot):
        z_s[slot] = _silu(_dot(hc, wz_ref[...]))
        dt_s[slot] = _softplus(_dot(hc, wdt_ref[...]) + dtb_ref[...])
        u_s[...] = _dot(hc, wx_ref[...])
        for bi in range(nb):
            x = u_s[bi * C:(bi + 1) * C, :]
            hl = halo_s[bi]
            x1 = _shift_rows(x, 1, hl)
            pair = cw_ref[1:2, :] * x + cw_ref[0:1, :] * x1
            pair_hl = cw_ref[1:2, :] * hl + cw_ref[0:1, :] * pltpu.roll(hl, 1, 0)
            y = (cw_ref[3:4, :] * x + cw_ref[2:3, :] * x1) + _shift_rows(pair, 2, pair_hl)
            xc_s[slot, bi * C:(bi + 1) * C, :] = _silu(y + cb_ref[...])
            halo_s[bi] = u_s[(bi + 1) * C - halo:(bi + 1) * C, :]

    @pl.when(pl.program_id(1) == 0)
    def _():
        st_ref[...] = jnp.zeros_like(st_ref)
        halo_s[...] = jnp.zeros_like(halo_s)
        project(h_ref[:, 0:C, :].reshape(nb * C, D), 0)

    low_half = lax.broadcasted_iota(jnp.int32, (C, LANES), 1) < SSD_P

    def one_chunk(bi, slot):
        rows = slice(bi * C, (bi + 1) * C)
        dt = dt_s[slot, rows, :]
        da = dt * a_ref[...]
        cum = jnp.dot(tril_ref[...], da, preferred_element_type=F32, precision=lax.Precision.HIGHEST) * LOG2E
        cum_t = cum.T
        row_t = (cum - jnp.log(dt) * LOG2E).T
        cum_last = cum[C - 1:C, :]
        wfac = dt * jnp.exp2(cum_last - cum)
        w_hi = wfac.astype(BF16)
        w_lo = (wfac - w_hi.astype(F32)).astype(BF16)
        wexp = _dot(w_hi, ex_ref[...]) + _dot(w_lo, ex_ref[...])
        dl = jnp.broadcast_to(jnp.exp2(cum_last), (SUBLANES, LANES))
        d_hi = dl.astype(BF16)
        d_lo = (dl - d_hi.astype(F32)).astype(BF16)
        dexp = (_dot(d_hi, ex_ref[...]) + _dot(d_lo, ex_ref[...]))[0:1, :]
        ypairs = []
        for g in range(SSD_GROUPS):
            bg = xc_s[slot, rows, inner + g * SSD_N:inner + (g + 1) * SSD_N]
            cg = xc_s[slot, rows, inner + gw + g * SSD_N:inner + gw + (g + 1) * SSD_N]
            bgb = bg.astype(BF16)
            cb = _dot_nt(cg.astype(BF16), bgb)
            st = st_ref[bi * SSD_GROUPS + g]
            stb = st.astype(BF16)
            xg = xc_s[slot, rows, g * gp:(g + 1) * gp]
            xgb = xg.astype(BF16)
            for jp in range(SSD_HPG // 2):
                lsl = slice(jp * LANES, (jp + 1) * LANES)
                rhs = jnp.concatenate([xgb[:, lsl], stb[:, lsl]], axis=0)
                ys = []
                for hd in (g * SSD_HPG + 2 * jp, g * SSD_HPG + 2 * jp + 1):
                    colb = jnp.broadcast_to(cum[:, hd:hd + 1], (C, C))
                    w = cb * jnp.exp2(colb - row_t[hd:hd + 1, :] + negm_ref[...])
                    ce = cg * jnp.exp2(colb)
                    ys.append(_dot(jnp.concatenate([w.astype(BF16), ce.astype(BF16)], axis=1), rhs))
                ypairs.append(jnp.where(low_half, ys[0], ys[1]))
            gsl = slice(g * gp, (g + 1) * gp)
            st_ref[bi * SSD_GROUPS + g] = st * dexp[:, gsl] + _dot_tn(bgb, (xg * wexp[:, gsl]).astype(BF16))
        gi = inner // SSD_GROUPS
        ppg = gi // LANES
        outs = []
        for g in range(SSD_GROUPS):
            sl = slice(g * gi, (g + 1) * gi)
            y = jnp.concatenate(ypairs[g * ppg:(g + 1) * ppg], axis=1)
            seg = (y + dl_ref[:, sl] * xc_s[slot, rows, sl]) * z_s[slot, rows, sl]
            nrm = seg * lax.rsqrt(jnp.mean(seg * seg, axis=-1, keepdims=True) + EPS)
            outs.append((nrm * nw_ref[:, sl]).astype(BF16))
        return jnp.concatenate(outs, axis=1)

    for j in range(n):
        nxt = h_ref[:, (j + 1) * C:(j + 2) * C, :] if j + 1 < n else hn_ref[...]
        project(nxt.reshape(nb * C, D), (j + 1) % 2)
        o = jnp.concatenate([one_chunk(bi, j % 2) for bi in range(nb)], axis=0)
        o_ref[:, j * C:(j + 1) * C, :] = _dot(o, wo_ref[...]).astype(o_ref.dtype).reshape(nb, C, D)


def _ssd(h, wz, wx, wdt, wo, conv_w, conv_b, dt_bias, a_log, d_skip, norm_w, ts, nb):
    B, S, D = h.shape
    C = CHUNK
    n = ts // C
    assert n % 2 == 0
    inner = SSD_HEADS * SSD_P
    cdim = inner + 2 * SSD_GROUPS * SSD_N
    pad = LANES - SSD_HEADS
    wdt_p = jnp.pad(wdt, ((0, 0), (0, pad)))
    dtb = jnp.pad(dt_bias.astype(F32), (0, pad)).reshape(1, LANES)
    a = jnp.pad(-jnp.exp(a_log.astype(F32)), (0, pad)).reshape(1, LANES)
    dl = jnp.repeat(d_skip.astype(F32), SSD_P).reshape(1, inner)
    r = jnp.arange(C)
    tril = (r[:, None] >= r[None, :]).astype(F32)
    negm = jnp.where(r[:, None] >= r[None, :], 0.0, NEG_BIG).astype(F32)
    ex = (jnp.arange(LANES)[:, None] == (jnp.arange(inner)[None, :] // SSD_P)).astype(BF16)
    tile = lambda w: pl.BlockSpec((nb, ts, w), lambda b, s: (b, s, 0))
    nxt = pl.BlockSpec((nb, C, D), lambda b, s: (b, jnp.minimum((s + 1) * n, S // C - 1), 0))
    R = nb * C
    return pl.pallas_call(
        _ssd_kernel,
        grid=(B // nb, S // ts),
        in_specs=[tile(D), nxt, _const_spec((D, inner)), _const_spec((D, cdim)), _const_spec((D, LANES)),
                  _const_spec((inner, D)), _const_spec((SSD_CONV, cdim)), _const_spec((1, cdim)),
                  _const_spec((1, LANES)), _const_spec((1, LANES)), _const_spec((1, inner)),
                  _const_spec((1, inner)), _const_spec((C, C)), _const_spec((C, C)),
                  _const_spec((LANES, inner))],
        out_specs=tile(D),
        out_shape=jax.ShapeDtypeStruct((B, S, D), BF16),
        scratch_shapes=[pltpu.VMEM((nb * SSD_GROUPS, SSD_N, SSD_HPG * SSD_P), F32),
                        pltpu.VMEM((nb, SUBLANES, cdim), F32), pltpu.VMEM((R, cdim), F32),
                        pltpu.VMEM((2, R, cdim), F32),
                        pltpu.VMEM((2, R, inner), F32), pltpu.VMEM((2, R, LANES), F32)],
        compiler_params=_params(2),
        name="ssd",
    )(h, h, wz, wx, wdt_p, wo, conv_w.astype(F32), conv_b.astype(F32).reshape(1, cdim), dtb, a, dl,
      norm_w.astype(F32).reshape(1, inner), tril, negm, ex)


def _merge_kernel(x_ref, h_ref, yr_ref, yh_ref, ys_ref, wgl_ref, wout_ref, mod_ref, nw_ref,
                  wr_hi_ref, wr_lo_ref, br_ref, ut_ref, x1_ref, h2_ref, rf_ref, ri_ref, cnt_ref, run_ref):
    D = x_ref.shape[1]
    ts = x_ref.shape[0]

    @pl.when((pl.program_id(0) == 0) & (pl.program_id(1) == 0))
    def _():
        run_ref[...] = jnp.zeros_like(run_ref)

    h = h_ref[...]
    gates = _sigmoid(_dot(h, wgl_ref[...]))
    m = (gates[:, 0:D] * yr_ref[...].astype(F32) + gates[:, D:2 * D] * yh_ref[...].astype(F32)
         + gates[:, 2 * D:3 * D] * ys_ref[...].astype(F32))
    y = _dot(m.astype(BF16), wout_ref[...])
    x1 = x_ref[...] + mod_ref[2:3, :] * y
    x1_ref[...] = x1
    h2 = _rms_mod(x1, nw_ref[...], mod_ref[4:5, :], mod_ref[3:4, :])
    h2_ref[...] = h2
    h_hi = h2.astype(BF16)
    h_lo = (h2 - h_hi.astype(F32)).astype(BF16)
    lg = (_dot_nt(wr_hi_ref[...], h_hi) + _dot_nt(wr_hi_ref[...], h_lo) + _dot_nt(wr_lo_ref[...], h_hi)
          + br_ref[...])
    gl = lg[0:MOE_GROUPS]
    gmax = jnp.max(gl, axis=0, keepdims=True)
    gi = lax.broadcasted_iota(jnp.int32, gl.shape, 0)
    gtop = jnp.min(jnp.where(gl == gmax, gi, MOE_GROUPS), axis=0, keepdims=True)
    p_g = 1.0 / jnp.sum(jnp.exp(gl - gmax), axis=0, keepdims=True)
    el = jnp.zeros((MOE_PER_GROUP, ts), F32)
    for g in range(MOE_GROUPS):
        lo = SUBLANES + g * MOE_PER_GROUP
        el = jnp.where(gtop == g, lg[lo:lo + MOE_PER_GROUP], el)
    ei = lax.broadcasted_iota(jnp.int32, el.shape, 0)
    m1 = jnp.max(el, axis=0, keepdims=True)
    i1 = jnp.min(jnp.where(el == m1, ei, MOE_PER_GROUP), axis=0, keepdims=True)
    el2 = jnp.where(ei == i1, -jnp.inf, el)
    m2 = jnp.max(el2, axis=0, keepdims=True)
    i2 = jnp.min(jnp.where(el2 == m2, ei, MOE_PER_GROUP), axis=0, keepdims=True)
    e21 = jnp.exp(m2 - m1)
    den = 1.0 + e21
    w1 = (1.0 / den) * p_g
    w2 = (e21 / den) * p_g
    zf = jnp.zeros((SUBLANES - 2, ts), F32)
    rf_ref[...] = jnp.concatenate([w1, w2, zf], axis=0)
    e1 = gtop * MOE_PER_GROUP + i1
    e2 = gtop * MOE_PER_GROUP + i2
    ex = lax.broadcasted_iota(jnp.int32, (MOE_EXPERTS, ts), 0)
    oh1 = jnp.where(ex == e1, 1.0, 0.0)
    oh2 = jnp.where(ex == e2, 1.0, 0.0)
    p1 = _dot(oh1.astype(BF16), ut_ref[...])
    p2 = _dot(oh2.astype(BF16), ut_ref[...])
    c1 = p1[:, ts - 1:ts]
    c2 = p2[:, ts - 1:ts]
    run = run_ref[...]
    base = run[:, 0:1]
    r1 = jnp.sum(oh1 * (p1 - 1.0 + base), axis=0, keepdims=True)
    r2 = jnp.sum(oh2 * (p2 - 1.0 + base + c1), axis=0, keepdims=True)
    run = run + (c1 + c2)
    run_ref[...] = run
    cnt_ref[...] = run
    ri_ref[...] = jnp.concatenate([e1, e2, r1.astype(jnp.int32), r2.astype(jnp.int32),
                                   jnp.zeros((SUBLANES - 4, ts), jnp.int32)], axis=0)


def _merge(x, h, y_ret, y_hg, y_ssd, wgl, wout, mod_l, norm_w, wr_hi, wr_lo, br, ts):
    B, S, D = x.shape
    ns = S // ts
    R = wr_hi.shape[0]
    tile = pl.BlockSpec((None, ts, D), lambda b, s: (b, s, 0))
    rt = pl.BlockSpec((SUBLANES, ts), lambda b, s: (0, b * ns + s))
    r = jnp.arange(ts)
    ut = (r[:, None] <= r[None, :]).astype(BF16)
    return pl.pallas_call(
        _merge_kernel,
        grid=(B, ns),
        in_specs=[tile, tile, tile, tile, tile, _const_spec((D, 3 * D)), _const_spec((D, D)),
                  pl.BlockSpec((None, 6, D), lambda b, s: (b, 0, 0)), _const_spec((1, D)),
                  _const_spec((R, D)), _const_spec((R, D)), _const_spec((R, 1)), _const_spec((ts, ts))],
        out_specs=[tile, tile, rt, rt, pl.BlockSpec((MOE_EXPERTS, LANES), lambda b, s: (0, 0))],
        out_shape=[jax.ShapeDtypeStruct((B, S, D), F32), jax.ShapeDtypeStruct((B, S, D), F32),
                   jax.ShapeDtypeStruct((SUBLANES, B * S), F32), jax.ShapeDtypeStruct((SUBLANES, B * S), jnp.int32),
                   jax.ShapeDtypeStruct((MOE_EXPERTS, LANES), F32)],
        scratch_shapes=[pltpu.VMEM((MOE_EXPERTS, LANES), F32)],
        compiler_params=_params(2),
        name="merge_router",
    )(x, h, y_ret, y_hg, y_ssd, wgl, wout, mod_l, norm_w.reshape(1, D), wr_hi, wr_lo, br, ut)


def _row_copy(src, src_row, dst, dst_row, sem):
    return pltpu.make_async_copy(src.at[pl.ds(src_row, 1), :], dst.at[pl.ds(dst_row, 1), :], sem)


DISPATCH_TILE = 1024
DMA_UNROLL = 256


def _dispatch_kernel(pend_ref, plen_ref, nact_ref, dest_ref, h2_ref, xs_hbm, zbuf, sem, zsem):
    i = pl.program_id(0)
    td = dest_ref.shape[2]
    BLK = MOE_BLOCK
    nb = xs_hbm.shape[0] // BLK

    def zero_fill(wait):
        def go(start, size):
            cp = pltpu.make_async_copy(zbuf.at[pl.ds(0, size), :], xs_hbm.at[pl.ds(start, size), :], zsem)
            if wait:
                cp.wait()
            else:
                cp.start()

        for e in range(MOE_EXPERTS):
            ln = plen_ref[e]
            end = pend_ref[e]
            size = BLK // 2
            while size >= SUBLANES:
                @pl.when((ln & size) != 0)
                def _(size=size, ln=ln, end=end):
                    go(pl.multiple_of(end - (ln & ~(size - 1)), size), size)
                size //= 2
            for j in range(SUBLANES - 1):
                @pl.when(j < (ln & (SUBLANES - 1)))
                def _(j=j, ln=ln, end=end):
                    go(end - ln + j, 1)
        for j in range(MOE_EXPERTS):
            @pl.when(nact_ref[0] + j < nb)
            def _(j=j):
                go(pl.multiple_of((nact_ref[0] + j) * BLK, BLK), BLK)

    @pl.when(i == 0)
    def _():
        zbuf[...] = jnp.zeros_like(zbuf)
        zero_fill(False)

    def issue(j, c):
        base = pl.multiple_of(j * DMA_UNROLL, DMA_UNROLL)
        src = h2_ref.at[pl.ds(base, DMA_UNROLL), :]
        for u in range(DMA_UNROLL):
            for k in range(2):
                _row_copy(src, u, xs_hbm, dest_ref[0, k, base + u], sem).start(priority=k)
        return c

    lax.fori_loop(0, td // DMA_UNROLL, issue, 0)
    for k in range(2):
        pltpu.make_async_copy(h2_ref, xs_hbm.at[pl.ds(0, td), :], sem).wait()

    @pl.when(i == 0)
    def _():
        zero_fill(True)


def _dispatch(h2, dest, pad_end, pad_len, nactive, nrows):
    T, D = h2.shape
    td = min(DISPATCH_TILE, T)
    nt = T // td
    dest3 = dest.reshape(2, nt, td).transpose(1, 0, 2)
    grid_spec = pltpu.PrefetchScalarGridSpec(
        num_scalar_prefetch=3,
        grid=(nt,),
        in_specs=[pl.BlockSpec((1, 2, td), lambda i, *_: (i, 0, 0), memory_space=pltpu.SMEM),
                  pl.BlockSpec((td, D), lambda i, *_: (i, 0))],
        out_specs=pl.BlockSpec(memory_space=pl.ANY),
        scratch_shapes=[pltpu.VMEM((MOE_BLOCK, D), F32), pltpu.SemaphoreType.DMA(()),
                        pltpu.SemaphoreType.DMA(())])
    return pl.pallas_call(
        _dispatch_kernel,
        grid_spec=grid_spec,
        out_shape=jax.ShapeDtypeStruct((nrows, D), F32),
        compiler_params=_params(1),
        name="dispatch",
    )(pad_end, pad_len, nactive, dest3, h2)


def _expert_kernel(be_ref, na_ref, x_ref, w1_ref, w3_ref, w2_ref, y_ref, w1_s, w3_s, w2_s):
    i = pl.program_id(0)

    @pl.when((i == 0) | (be_ref[i] != be_ref[jnp.maximum(i - 1, 0)]))
    def _():
        w1_s[...] = w1_ref[...].astype(BF16)
        w3_s[...] = w3_ref[...].astype(BF16)
        w2_s[...] = w2_ref[...].astype(BF16)

    @pl.when(i < na_ref[0])
    def _():
        x = x_ref[...].astype(BF16)
        a = _dot(x, w1_s[...])
        b = _dot(x, w3_s[...])
        y_ref[...] = _dot((_silu(a) * b).astype(BF16), w2_s[...])

    @pl.when(i >= na_ref[0])
    def _():
        y_ref[...] = jnp.zeros_like(y_ref)


def _experts(xs, block_e, nactive, w1, w3, w2, layer):
    nrows, D = xs.shape
    BLK = MOE_BLOCK
    nb = nrows // BLK
    ff = w1.shape[3]
    wsel = lambda i, be, na: (layer, be[i], 0, 0)
    grid_spec = pltpu.PrefetchScalarGridSpec(
        num_scalar_prefetch=2,
        grid=(nb,),
        in_specs=[pl.BlockSpec((BLK, D), lambda i, be, na: (jnp.minimum(i, na[0] - 1), 0)),
                  pl.BlockSpec((None, None, D, ff), wsel), pl.BlockSpec((None, None, D, ff), wsel),
                  pl.BlockSpec((None, None, ff, D), wsel)],
        out_specs=pl.BlockSpec((BLK, D), lambda i, be, na: (i, 0)),
        scratch_shapes=[pltpu.VMEM((D, ff), BF16), pltpu.VMEM((D, ff), BF16), pltpu.VMEM((ff, D), BF16)])
    return pl.pallas_call(
        _expert_kernel,
        grid_spec=grid_spec,
        out_shape=jax.ShapeDtypeStruct((nrows, D), F32),
        compiler_params=_params(1),
        name="experts",
    )(block_e, nactive, xs, w1, w3, w2)


def _dispatch_plan(route_i, counts_f, T):
    BLK = MOE_BLOCK
    nb = -(-(2 * T) // BLK) + MOE_EXPERTS
    counts = counts_f[:, 0].astype(jnp.int32)
    padded = ((counts + BLK - 1) // BLK) * BLK
    pad_ends = jnp.cumsum(padded)
    pad_starts = pad_ends - padded
    e = route_i[0:2]
    rank = route_i[2:4]
    sel = e[:, :, None] == jnp.arange(MOE_EXPERTS, dtype=jnp.int32)[None, None, :]
    dest = rank + jnp.sum(jnp.where(sel, pad_starts[None, None, :], 0), axis=-1)
    nactive = (pad_ends[-1] // BLK).astype(jnp.int32)
    blk_id = jnp.minimum(jnp.arange(nb, dtype=jnp.int32), nactive - 1)
    block_e = jnp.sum((blk_id[:, None] * BLK >= pad_ends[None, :]).astype(jnp.int32), axis=1)
    block_e = jnp.minimum(block_e, MOE_EXPERTS - 1)
    return dest, block_e, pad_ends, padded - counts, nactive.reshape(1), nb * BLK


COMBINE_TILE = 256


def _combine_kernel(final, dcur_ref, dnxt_ref, x_ref, rw_ref, mod_ref, nw_ref, modn_ref, ys_hbm, *rest):
    if final:
        o_ref, gbuf, sem = rest
    else:
        x2_ref, h_ref, gbuf, sem = rest
    i = pl.program_id(0)
    n = pl.num_programs(0)
    ts = x_ref.shape[0]
    slot = i % 2

    def issue(dref, sl):
        for r in range(ts):
            for k in range(2):
                _row_copy(ys_hbm, dref[0, k, r], gbuf.at[sl, k], r, sem.at[sl]).start(priority=k)

    @pl.when(i == 0)
    def _():
        issue(dcur_ref, 0)

    @pl.when(i + 1 < n)
    def _():
        issue(dnxt_ref, 1 - slot)

    for k in range(2):
        pltpu.make_async_copy(ys_hbm.at[pl.ds(0, ts), :], gbuf.at[slot, k], sem.at[slot]).wait()
    y = rw_ref[:, 0:1] * gbuf[slot, 0] + rw_ref[:, 1:2] * gbuf[slot, 1]
    x2 = x_ref[...] + mod_ref[5:6, :] * y
    if final:
        o_ref[...] = (x2 * lax.rsqrt(jnp.mean(x2 * x2, axis=-1, keepdims=True) + EPS) * nw_ref[...]).astype(o_ref.dtype)
    else:
        x2_ref[...] = x2
        h_ref[...] = _rms_mod(x2, nw_ref[...], modn_ref[1:2, :], modn_ref[0:1, :]).astype(h_ref.dtype)


def _combine(x1, ys, dest, rw, mod_l, norm_w, mod_next, final, out_dtype):
    B, S, D = x1.shape
    T = B * S
    ts = min(COMBINE_TILE, S)
    nsb = S // ts
    nt = T // ts
    dest3 = dest.reshape(2, nt, ts).transpose(1, 0, 2)
    tile = pl.BlockSpec((ts, D), lambda i: (i, 0))
    modspec = pl.BlockSpec((None, 6, D), lambda i: (i // nsb, 0, 0))
    if final:
        out_specs = tile
        out_shape = jax.ShapeDtypeStruct((T, D), out_dtype)
    else:
        out_specs = [tile, tile]
        out_shape = [jax.ShapeDtypeStruct((T, D), F32), jax.ShapeDtypeStruct((T, D), BF16)]
    out = pl.pallas_call(
        functools.partial(_combine_kernel, final),
        grid=(nt,),
        in_specs=[pl.BlockSpec((1, 2, ts), lambda i: (i, 0, 0), memory_space=pltpu.SMEM),
                  pl.BlockSpec((1, 2, ts), lambda i: (jnp.minimum(i + 1, nt - 1), 0, 0), memory_space=pltpu.SMEM),
                  tile, pl.BlockSpec((ts, SUBLANES), lambda i: (i, 0)), modspec,
                  pl.BlockSpec((1, D), lambda i: (0, 0)), modspec,
                  pl.BlockSpec(memory_space=pl.ANY)],
        out_specs=out_specs,
        out_shape=out_shape,
        scratch_shapes=[pltpu.VMEM((2, 2, ts, D), F32), pltpu.SemaphoreType.DMA((2,))],
        compiler_params=_params(1),
        name="combine_final" if final else "combine_prenorm",
    )(dest3, dest3, x1.reshape(T, D), rw, mod_l, norm_w.reshape(1, D), mod_next, ys)
    if final:
        return out.reshape(B, S, D)
    return out[0].reshape(B, S, D), out[1].reshape(B, S, D)


def _seq_tile(S, want):
    ts = min(want, S)
    assert S % ts == 0 and ts % CHUNK == 0
    return ts


def kernel(x, c, positions, w_ada, b_ada, norm_mix_w, w_in, ssd_conv_w, ssd_conv_b, ssd_dt_bias, ssd_a_log,
           ssd_d, ssd_norm_w, hg_lb, hg_norm_w, w_ret_o, w_hg_o, w_ssd_o, w_out, norm_ffn_w, moe_w_group,
           moe_b_group, moe_w_expert, moe_b_expert, moe_w1, moe_w3, moe_w2, final_norm_w):
    B, S, D = x.shape
    L = w_ada.shape[0]
    T = B * S
    ts_el = _seq_tile(S, 512)
    ts_mix = _seq_tile(S, 256)
    nb_mix = 2 if B % 2 == 0 else 1

    mod = _adaln(c, w_ada, b_ada)
    cos, sin = _rope_tables(positions, ts_el)
    lb_soft = jax.nn.softmax(hg_lb.astype(F32), axis=0)
    lb_all = jnp.cumsum(lb_soft, axis=0) - lb_soft[0]

    qk = RET_HEADS * RET_DK
    vw = RET_HEADS * RET_DV
    hgw = HG_HEADS * HG_DK
    inner = SSD_HEADS * SSD_P
    cdim = inner + 2 * SSD_GROUPS * SSD_N
    sizes = (qk, qk, vw, vw, hgw, hgw, hgw, hgw, inner, cdim, SSD_HEADS, 3 * D)
    offs = [0]
    for n in sizes:
        offs.append(offs[-1] + n)

    xcur = x.astype(F32)
    h = _prenorm(xcur, norm_mix_w[0].astype(F32), mod[0], ts_el)
    for l in range(L):
        wl = w_in[l].astype(BF16)
        cols = [wl[:, offs[i]:offs[i + 1]] for i in range(len(sizes))]
        y_ret = _retention(h, cos, sin, cols[0], cols[1], cols[2], cols[3], w_ret_o[l].astype(BF16), ts_mix,
                           nb_mix)
        y_hg = _hgrn2(h, cols[4], cols[5], cols[6], cols[7], w_hg_o[l].astype(BF16), lb_all[l],
                      hg_norm_w[l].astype(F32), ts_mix, nb_mix)
        y_ssd = _ssd(h, cols[8], cols[9], cols[10], w_ssd_o[l].astype(BF16), ssd_conv_w[l], ssd_conv_b[l],
                     ssd_dt_bias[l], ssd_a_log[l], ssd_d[l], ssd_norm_w[l], ts_mix, nb_mix)
        wr = jnp.zeros((SUBLANES + MOE_EXPERTS, D), F32)
        wr = wr.at[0:MOE_GROUPS].set(moe_w_group[l].astype(F32).T).at[SUBLANES:].set(moe_w_expert[l].astype(F32).T)
        br = jnp.zeros((SUBLANES + MOE_EXPERTS, 1), F32)
        br = br.at[0:MOE_GROUPS, 0].set(moe_b_group[l].astype(F32)).at[SUBLANES:, 0].set(moe_b_expert[l].astype(F32))
        wr_hi = wr.astype(BF16)
        wr_lo = (wr - wr_hi.astype(F32)).astype(BF16)
        x1, h2, route_w, route_i, counts = _merge(xcur, h, y_ret, y_hg, y_ssd, cols[11], w_out[l].astype(BF16),
                                                  mod[l], norm_ffn_w[l].astype(F32), wr_hi, wr_lo, br, ts_el)
        dest, block_e, pad_end, pad_len, nactive, nrows = _dispatch_plan(route_i, counts, T)
        xs = _dispatch(h2.reshape(T, D), dest, pad_end, pad_len, nactive, nrows)
        ys = _experts(xs, block_e, nactive, moe_w1, moe_w3, moe_w2, l)
        rw = route_w.T
        if l + 1 < L:
            xcur, h = _combine(x1, ys, dest, rw, mod[l], norm_mix_w[l + 1].astype(F32), mod[l + 1], False, x.dtype)
        else:
            out = _combine(x1, ys, dest, rw, mod[l], final_norm_w.astype(F32), mod[l], True, x.dtype)
    return out
```

```python
import functools

import jax
import jax.numpy as jnp
from jax import lax
from jax.experimental import pallas as pl
from jax.experimental.pallas import tpu as pltpu

F32 = jnp.float32
BF16 = jnp.bfloat16

EPS = 1e-6
ROPE_BASE = 10000.0
RET_HEADS = 4
RET_DK = 128
RET_DV = 256
HG_HEADS = 8
HG_DK = 128
SSD_HEADS = 32
SSD_P = 64
SSD_GROUPS = 4
SSD_N = 128
SSD_HPG = SSD_HEADS // SSD_GROUPS
SSD_CONV = 4
MOE_GROUPS = 4
MOE_PER_GROUP = 8
MOE_EXPERTS = 32
MOE_BLOCK = 512
CHUNK = 128
LANES = 128
SUBLANES = 8
VMEM_LIMIT = 56 * 1024 * 1024
HG_LEVELS = (64, 32, 16, 8, 4, 2, 1)
LOG2E = 1.4426950408889634
NEG_BIG = -1e30


def _params(n_axes, vmem=VMEM_LIMIT):
    return pltpu.CompilerParams(dimension_semantics=("arbitrary",) * n_axes,
                                vmem_limit_bytes=vmem)


def _const_spec(shape):
    nd = len(shape)
    return pl.BlockSpec(shape, lambda *_: (0,) * nd, pipeline_mode=pl.Buffered(1))


def _sigmoid(x):
    return 0.5 * jnp.tanh(0.5 * x) + 0.5


def _silu(x):
    h = 0.5 * x
    return h * jnp.tanh(h) + h


def _dot(a, b):
    return jnp.dot(a, b, preferred_element_type=F32)


def _dot_nt(a, b):
    return lax.dot_general(a, b, (((1,), (1,)), ((), ())), preferred_element_type=F32)


def _dot_tn(a, b):
    return lax.dot_general(a, b, (((0,), (0,)), ((), ())), preferred_element_type=F32)


def _dot2(m_bf16, x_f32):
    hi = x_f32.astype(BF16)
    lo = (x_f32 - hi.astype(F32)).astype(BF16)
    return _dot(m_bf16, hi) + _dot(m_bf16, lo)


def _ada_kernel(c_ref, w_ref, b_ref, o_ref):
    c = c_ref[...]
    o_ref[...] = _dot(_silu(c).astype(BF16), w_ref[...].astype(BF16)) + b_ref[...]


def _adaln(c, w_ada, b_ada):
    L, D, N = w_ada.shape
    B = c.shape[0]
    cp = jnp.zeros((SUBLANES, D), F32).at[:B].set(c.astype(F32))
    tn = 1536
    out = pl.pallas_call(
        _ada_kernel,
        grid=(L, N // tn),
        in_specs=[pl.BlockSpec((SUBLANES, D), lambda l, n: (0, 0)),
                  pl.BlockSpec((None, D, tn), lambda l, n: (l, 0, n)),
                  pl.BlockSpec((None, 1, tn), lambda l, n: (l, 0, n))],
        out_specs=pl.BlockSpec((None, SUBLANES, tn), lambda l, n: (l, 0, n)),
        out_shape=jax.ShapeDtypeStruct((L, SUBLANES, N), F32),
        compiler_params=_params(2),
        name="adaln",
    )(cp, w_ada, b_ada.reshape(L, 1, N))
    return out[:, :B].reshape(L, B, 6, D)


def _rope_kernel(pos_ref, invf_ref, sign_ref, cos_ref, sin_ref):
    ang = pos_ref[...].astype(F32) * invf_ref[...]
    cos_ref[...] = jnp.cos(ang)
    sin_ref[...] = jnp.sin(ang) * sign_ref[...]


def _rope_tables(positions, ts):
    B, S = positions.shape
    half = RET_DK // 2
    inv_freq = ROPE_BASE ** (-jnp.linspace(0.0, 1.0, half, dtype=F32))
    invf = jnp.concatenate([inv_freq, inv_freq]).reshape(1, RET_DK)
    sign = jnp.concatenate([-jnp.ones((half,), F32), jnp.ones((half,), F32)]).reshape(1, RET_DK)
    spec = pl.BlockSpec((None, ts, RET_DK), lambda b, s: (b, s, 0))
    return pl.pallas_call(
        _rope_kernel,
        grid=(B, S // ts),
        in_specs=[pl.BlockSpec((None, ts, 1), lambda b, s: (b, s, 0)),
                  pl.BlockSpec((1, RET_DK), lambda b, s: (0, 0)),
                  pl.BlockSpec((1, RET_DK), lambda b, s: (0, 0))],
        out_specs=[spec, spec],
        out_shape=[jax.ShapeDtypeStruct((B, S, RET_DK), F32)] * 2,
        compiler_params=_params(2),
        name="rope_tables",
    )(positions.reshape(B, S, 1), invf, sign)


def _rms_mod(x, w, scale, shift):
    y = x * lax.rsqrt(jnp.mean(x * x, axis=-1, keepdims=True) + EPS)
    return (y * w) * (1.0 + scale) + shift


def _prenorm_kernel(x_ref, w_ref, mod_ref, o_ref):
    o_ref[...] = _rms_mod(x_ref[...], w_ref[...], mod_ref[1:2, :], mod_ref[0:1, :]).astype(o_ref.dtype)


def _prenorm(x, w, mod_l, ts):
    B, S, D = x.shape
    return pl.pallas_call(
        _prenorm_kernel,
        grid=(B, S // ts),
        in_specs=[pl.BlockSpec((None, ts, D), lambda b, s: (b, s, 0)),
                  pl.BlockSpec((1, D), lambda b, s: (0, 0)),
                  pl.BlockSpec((None, 6, D), lambda b, s: (b, 0, 0))],
        out_specs=pl.BlockSpec((None, ts, D), lambda b, s: (b, s, 0)),
        out_shape=jax.ShapeDtypeStruct((B, S, D), BF16),
        compiler_params=_params(2),
        name="prenorm",
    )(x, w.reshape(1, D), mod_l)


def _ret_kernel(h_ref, hn_ref, cos_ref, cosn_ref, sin_ref, sinn_ref, wq_ref, wk_ref, wv_ref, wg_ref, wo_ref,
                di_ref, dq_ref, dk_ref, dc_ref, o_ref,
                st_ref, q_s, k_s, v_s, g_s):
    nb, ts, D = h_ref.shape
    C = CHUNK
    n = ts // C
    R = nb * C
    scale = RET_DK ** -0.5

    def project(hc, cos, sin, slot):
        hc = hc.reshape(R, D)
        cos = cos.reshape(R, RET_DK)
        sin = sin.reshape(R, RET_DK)
        q = _dot(hc, wq_ref[...])
        k = _dot(hc, wk_ref[...])
        for hh in range(RET_HEADS):
            sl = slice(hh * RET_DK, (hh + 1) * RET_DK)
            qh = q[:, sl]
            kh = k[:, sl]
            q_s[slot, :, sl] = (qh * cos + pltpu.roll(qh, RET_DK // 2, 1) * sin).astype(BF16)
            k_s[slot, :, sl] = (kh * cos + pltpu.roll(kh, RET_DK // 2, 1) * sin) * scale
        v_s[slot] = _dot(hc, wv_ref[...]).astype(BF16)
        g_s[slot] = _silu(_dot(hc, wg_ref[...]))

    @pl.when(pl.program_id(1) == 0)
    def _():
        st_ref[...] = jnp.zeros_like(st_ref)
        project(h_ref[:, 0:C, :], cos_ref[:, 0:C, :], sin_ref[:, 0:C, :], 0)

    def one_chunk(bi, slot):
        rows = slice(bi * C, (bi + 1) * C)
        outs = []
        for hh in range(RET_HEADS):
            ksl = slice(hh * RET_DK, (hh + 1) * RET_DK)
            vsl = slice(hh * RET_DV, (hh + 1) * RET_DV)
            qc = q_s[slot, rows, ksl]
            kc = k_s[slot, rows, ksl]
            vc = v_s[slot, rows, vsl]
            st = st_ref[bi * RET_HEADS + hh]
            scores = _dot_nt(qc, kc.astype(BF16)) * di_ref[hh]
            o = _dot(scores.astype(BF16), vc) + _dot(qc, st.astype(BF16)) * dq_ref[hh]
            st_ref[bi * RET_HEADS + hh] = st * dc_ref[hh] + _dot_tn((kc * dk_ref[hh]).astype(BF16), vc)
            mu = jnp.mean(o, axis=-1, keepdims=True)
            d = o - mu
            var = jnp.mean(d * d, axis=-1, keepdims=True)
            outs.append((g_s[slot, rows, vsl] * (d * lax.rsqrt(var + EPS))).astype(BF16))
        return jnp.concatenate(outs, axis=1)

    for j in range(n):
        if j + 1 < n:
            nsl = slice((j + 1) * C, (j + 2) * C)
            project(h_ref[:, nsl, :], cos_ref[:, nsl, :], sin_ref[:, nsl, :], (j + 1) % 2)
        else:
            project(hn_ref[...], cosn_ref[...], sinn_ref[...], (j + 1) % 2)
        o = jnp.concatenate([one_chunk(bi, j % 2) for bi in range(nb)], axis=0)
        o_ref[:, j * C:(j + 1) * C, :] = _dot(o, wo_ref[...]).astype(o_ref.dtype).reshape(nb, C, D)


def _retention(h, cos, sin, wq, wk, wv, wg, wo, ts, nb):
    B, S, D = h.shape
    H, C = RET_HEADS, CHUNK
    log_gamma = jnp.log(1.0 - 2.0 ** (-5.0 - jnp.arange(H, dtype=F32)))
    idx = jnp.arange(C, dtype=F32)
    rel = idx[:, None] - idx[None, :]
    di = jnp.where(rel >= 0.0, jnp.exp(log_gamma[:, None, None] * jnp.maximum(rel, 0.0)), 0.0)
    dq = jnp.broadcast_to(jnp.exp(log_gamma[:, None] * (idx[None, :] + 1.0))[:, :, None], (H, C, RET_DV))
    dk = jnp.broadcast_to(jnp.exp(log_gamma[:, None] * (C - 1.0 - idx[None, :]))[:, :, None], (H, C, RET_DK))
    dc = jnp.broadcast_to(jnp.exp(log_gamma * C)[:, None, None], (H, 1, RET_DV))
    qk, vw = H * RET_DK, H * RET_DV
    n = ts // C
    assert n % 2 == 0
    tile = lambda w: pl.BlockSpec((nb, ts, w), lambda b, s: (b, s, 0))
    nxt = lambda w: pl.BlockSpec((nb, C, w), lambda b, s: (b, jnp.minimum((s + 1) * n, S // C - 1), 0))
    R = nb * C
    return pl.pallas_call(
        _ret_kernel,
        grid=(B // nb, S // ts),
        in_specs=[tile(D), nxt(D), tile(RET_DK), nxt(RET_DK), tile(RET_DK), nxt(RET_DK),
                  _const_spec((D, qk)), _const_spec((D, qk)), _const_spec((D, vw)), _const_spec((D, vw)),
                  _const_spec((vw, D)),
                  _const_spec((H, C, C)), _const_spec((H, C, RET_DV)), _const_spec((H, C, RET_DK)),
                  _const_spec((H, 1, RET_DV))],
        out_specs=tile(D),
        out_shape=jax.ShapeDtypeStruct((B, S, D), BF16),
        scratch_shapes=[pltpu.VMEM((nb * H, RET_DK, RET_DV), F32),
                        pltpu.VMEM((2, R, qk), BF16), pltpu.VMEM((2, R, qk), F32),
                        pltpu.VMEM((2, R, vw), BF16), pltpu.VMEM((2, R, vw), F32)],
        compiler_params=_params(2),
        name="retention",
    )(h, h, cos, cos, sin, sin, wq, wk, wv, wg, wo, di, dq, dk, dc)


def _hg_level_matrices():
    C = CHUNK
    H = C // 2
    r = jnp.arange(C)
    stack = (r[:, None] >= r[None, :]).astype(BF16)
    t = jnp.arange(H)[:, None]
    s = jnp.arange(C)[None, :] % H
    masks = []
    for b in HG_LEVELS[1:]:
        masks.append(((t // (2 * b)) == (s // (2 * b))) & ((t & b) != 0) & ((s & b) == 0))
    masks.append(t == s)
    return stack, jnp.stack(masks).astype(F32)


def _hg_kernel(h_ref, hn_ref, wq_ref, wf_ref, wi_ref, wg_ref, wo_ref, lb_ref, nw_ref, cm_ref, mk_ref, o_ref,
               st_ref, q_s, k_s, v_s, lf_s, g_s):
    nb, ts, D = h_ref.shape
    C = CHUNK
    W = HG_HEADS * HG_DK
    n = ts // C
    lb = lb_ref[...]

    def project(hc, slot):
        q_s[slot] = _silu(_dot(hc, wq_ref[...]))
        hf = _dot(hc, wf_ref[...])
        lf_s[slot] = jnp.log(lb + (1.0 - lb) * _sigmoid(hf)) * LOG2E
        k_s[slot] = (1.0 - lb) * _sigmoid(-hf)
        v_s[slot] = _dot(hc, wi_ref[...]).astype(BF16)
        g_s[slot] = _silu(_dot(hc, wg_ref[...]))

    @pl.when(pl.program_id(1) == 0)
    def _():
        st_ref[...] = jnp.zeros_like(st_ref)
        project(h_ref[:, 0:C, :].reshape(nb * C, D), 0)

    odd_row = {b: (lax.broadcasted_iota(jnp.int32, (C, 1), 0) & b) != 0 for b in HG_LEVELS}

    def one_chunk(bi, slot):
        rows = slice(bi * C, (bi + 1) * C)
        lf = lf_s[slot, rows, :]
        q = q_s[slot, rows, :]
        k = k_s[slot, rows, :]
        G = _dot2(cm_ref[...], lf)
        H = C // 2
        acc = [None] * HG_HEADS
        cross = [None] * HG_HEADS
        left = lax.broadcasted_iota(jnp.int32, (H, C), 1) < H
        G3 = G.reshape(C // SUBLANES, SUBLANES, W)
        sub8 = lax.broadcasted_iota(jnp.int32, (C // SUBLANES, SUBLANES, 1), 1)

        def tile_row(i):
            return jnp.broadcast_to(G3[:, i:i + 1, :], G3.shape)
        for li, b in enumerate(HG_LEVELS):
            if b >= SUBLANES:
                pieces = []
                for j in range(C // b):
                    ref = G[(j // 2) * 2 * b + b - 1:(j // 2) * 2 * b + b, :]
                    blk = slice(j * b, (j + 1) * b)
                    if j % 2 == 1:
                        pieces.append(q[blk] * jnp.exp2(G[blk] - ref))
                    else:
                        pieces.append(k[blk] * jnp.exp2(ref - G[blk]))
                z = jnp.concatenate(pieces, axis=0)
            elif b > 1:
                ref = tile_row(3) if b == 4 else jnp.where(sub8 < 4, tile_row(1), tile_row(5))
                e = -jnp.abs(G3 - ref)
                z = jnp.where(odd_row[b], q, k) * jnp.exp2(e.reshape(C, W))
            else:
                z = jnp.where(odd_row[b], q * jnp.exp2(lf), k)
            z = z.astype(BF16)
            for hh in range(HG_HEADS):
                zh = z[:, hh * HG_DK:(hh + 1) * HG_DK]
                if li == 0:
                    cross[hh] = _dot_nt(zh[H:], zh)
                else:
                    zt, zb = zh[:H], zh[H:]
                    zero = jnp.zeros_like(zt)
                    rhs = jnp.concatenate([jnp.concatenate([zt, zero], axis=1),
                                           jnp.concatenate([zero, zb], axis=1)], axis=0)
                    a = _dot_nt(jnp.concatenate([zt, zb], axis=1), rhs) * mk_ref[li - 1]
                    acc[hh] = a if li == 1 else acc[hh] + a
        g_last = G[C - 1:C, :]
        qg = (q * jnp.exp2(G)).astype(BF16)
        kh = (k * jnp.exp2(g_last - G)).astype(BF16)
        qk = q * k
        dec = jnp.exp2(g_last)
        eye = mk_ref[len(HG_LEVELS) - 1]
        outs = []
        for hh in range(HG_HEADS):
            sl = slice(hh * HG_DK, (hh + 1) * HG_DK)
            vh = v_s[slot, rows, sl]
            diag = jnp.sum(qk[:, sl], axis=-1, keepdims=True)
            dblk = acc[hh] + jnp.where(left, diag[:H], diag[H:]) * eye
            a_top = jnp.where(left, dblk, 0.0).astype(BF16)
            a_bot = jnp.where(left, cross[hh], dblk).astype(BF16)
            st = st_ref[bi * HG_HEADS + hh]
            o = (jnp.concatenate([_dot(a_top, vh), _dot(a_bot, vh)], axis=0)
                 + _dot_nt(qg[:, sl], st.astype(BF16)))
            st_ref[bi * HG_HEADS + hh] = st * dec[:, sl] + _dot_tn(vh, kh[:, sl])
            on = o * lax.rsqrt(jnp.mean(o * o, axis=-1, keepdims=True) + EPS) * nw_ref[...]
            outs.append((on * g_s[slot, rows, sl]).astype(BF16))
        return jnp.concatenate(outs, axis=1)

    for j in range(n):
        nxt = h_ref[:, (j + 1) * C:(j + 2) * C, :] if j + 1 < n else hn_ref[...]
        project(nxt.reshape(nb * C, D), (j + 1) % 2)
        o = jnp.concatenate([one_chunk(bi, j % 2) for bi in range(nb)], axis=0)
        o_ref[:, j * C:(j + 1) * C, :] = _dot(o, wo_ref[...]).astype(o_ref.dtype).reshape(nb, C, D)


def _hgrn2(h, wq, wf, wi, wg, wo, lb, norm_w, ts, nb):
    B, S, D = h.shape
    W = HG_HEADS * HG_DK
    C = CHUNK
    n = ts // C
    assert n % 2 == 0
    R = nb * C
    cm, mk = _hg_level_matrices()
    tile = lambda w: pl.BlockSpec((nb, ts, w), lambda b, s: (b, s, 0))
    nxt = pl.BlockSpec((nb, C, D), lambda b, s: (b, jnp.minimum((s + 1) * n, S // C - 1), 0))
    return pl.pallas_call(
        _hg_kernel,
        grid=(B // nb, S // ts),
        in_specs=[tile(D), nxt] + [_const_spec((D, W))] * 4 + [_const_spec((W, D)), _const_spec((1, W)),
                  _const_spec((1, HG_DK)), _const_spec(cm.shape), _const_spec(mk.shape)],
        out_specs=tile(D),
        out_shape=jax.ShapeDtypeStruct((B, S, D), BF16),
        scratch_shapes=[pltpu.VMEM((nb * HG_HEADS, HG_DK, HG_DK), F32),
                        pltpu.VMEM((2, R, W), F32), pltpu.VMEM((2, R, W), F32), pltpu.VMEM((2, R, W), BF16),
                        pltpu.VMEM((2, R, W), F32), pltpu.VMEM((2, R, W), F32)],
        compiler_params=_params(2),
        name="hgrn2",
    )(h, h, wq, wf, wi, wg, wo, lb.reshape(1, W), norm_w.reshape(1, HG_DK), cm, mk)


def _shift_rows(x, k, head):
    r, w = x.shape
    t = x.reshape(r // SUBLANES, SUBLANES, w)
    rot = pltpu.roll(t, k, 1)
    hrot = pltpu.roll(head.reshape(1, SUBLANES, w), k, 1)
    prev = jnp.concatenate([hrot, rot[:-1]], axis=0)
    sub = lax.broadcasted_iota(jnp.int32, (1, SUBLANES, 1), 1)
    return jnp.where(sub < k, prev, rot).reshape(r, w)


def _softplus(x):
    return jnp.maximum(x, 0.0) + jnp.log(1.0 + jnp.exp(-jnp.abs(x)))


def _ssd_kernel(h_ref, hn_ref, wz_ref, wx_ref, wdt_ref, wo_ref, cw_ref, cb_ref, dtb_ref, a_ref, dl_ref, nw_ref,
                tril_ref, negm_ref, ex_ref, o_ref,
                st_ref, halo_s, u_s, xc_s, z_s, dt_s):
    nb, ts, D = h_ref.shape
    C = CHUNK
    n = ts // C
    inner = SSD_HEADS * SSD_P
    gw = SSD_GROUPS * SSD_N
    gp = SSD_HPG * SSD_P
    halo = SUBLANES

    def project(hc, slot):
        z_s[slot] = _silu(_dot(hc, wz_ref[...]))
        dt_s[slot] = _softplus(_dot(hc, wdt_ref[...]) + dtb_ref[...])
        u_s[...] = _dot(hc, wx_ref[...])
        for bi in range(nb):
            x = u_s[bi * C:(bi + 1) * C, :]
            hl = halo_s[bi]
            x1 = _shift_rows(x, 1, hl)
            pair = cw_ref[1:2, :] * x + cw_ref[0:1, :] * x1
            pair_hl = cw_ref[1:2, :] * hl + cw_ref[0:1, :] * pltpu.roll(hl, 1, 0)
            y = (cw_ref[3:4, :] * x + cw_ref[2:3, :] * x1) + _shift_rows(pair, 2, pair_hl)
            xc_s[slot, bi * C:(bi + 1) * C, :] = _silu(y + cb_ref[...])
            halo_s[bi] = u_s[(bi + 1) * C - halo:(bi + 1) * C, :]

    @pl.when(pl.program_id(1) == 0)
    def _():
        st_ref[...] = jnp.zeros_like(st_ref)
        halo_s[...] = jnp.zeros_like(halo_s)
        project(h_ref[:, 0:C, :].reshape(nb * C, D), 0)

    low_half = lax.broadcasted_iota(jnp.int32, (C, LANES), 1) < SSD_P

    def one_chunk(bi, slot):
        rows = slice(bi * C, (bi + 1) * C)
        dt = dt_s[slot, rows, :]
        da = dt * a_ref[...]
        cum = jnp.dot(tril_ref[...], da, preferred_element_type=F32, precision=lax.Precision.HIGHEST) * LOG2E
        cum_t = cum.T
        row_t = (cum - jnp.log(dt) * LOG2E).T
        cum_last = cum[C - 1:C, :]
        wfac = dt * jnp.exp2(cum_last - cum)
        w_hi = wfac.astype(BF16)
        w_lo = (wfac - w_hi.astype(F32)).astype(BF16)
        wexp = _dot(w_hi, ex_ref[...]) + _dot(w_lo, ex_ref[...])
        dl = jnp.broadcast_to(jnp.exp2(cum_last), (SUBLANES, LANES))
        d_hi = dl.astype(BF16)
        d_lo = (dl - d_hi.astype(F32)).astype(BF16)
        dexp = (_dot(d_hi, ex_ref[...]) + _dot(d_lo, ex_ref[...]))[0:1, :]
        ypairs = []
        for g in range(SSD_GROUPS):
            bg = xc_s[slot, rows, inner + g * SSD_N:inner + (g + 1) * SSD_N]
            cg = xc_s[slot, rows, inner + gw + g * SSD_N:inner + gw + (g + 1) * SSD_N]
            bgb = bg.astype(BF16)
            cb = _dot_nt(cg.astype(BF16), bgb)
            st = st_ref[bi * SSD_GROUPS + g]
            stb = st.astype(BF16)
            xg = xc_s[slot, rows, g * gp:(g + 1) * gp]
            xgb = xg.astype(BF16)
            for jp in range(SSD_HPG // 2):
                lsl = slice(jp * LANES, (jp + 1) * LANES)
                rhs = jnp.concatenate([xgb[:, lsl], stb[:, lsl]], axis=0)
                ys = []
                for hd in (g * SSD_HPG + 2 * jp, g * SSD_HPG + 2 * jp + 1):
                    colb = jnp.broadcast_to(cum[:, hd:hd + 1], (C, C))
                    w = cb * jnp.exp2(colb - row_t[hd:hd + 1, :] + negm_ref[...])
                    ce = cg * jnp.exp2(colb)
                    ys.append(_dot(jnp.concatenate([w.astype(BF16), ce.astype(BF16)], axis=1), rhs))
                ypairs.append(jnp.where(low_half, ys[0], ys[1]))
            gsl = slice(g * gp, (g + 1) * gp)
            st_ref[bi * SSD_GROUPS + g] = st * dexp[:, gsl] + _dot_tn(bgb, (xg * wexp[:, gsl]).astype(BF16))
        gi = inner // SSD_GROUPS
        ppg = gi // LANES
        outs = []
        for g in range(SSD_GROUPS):
            sl = slice(g * gi, (g + 1) * gi)
            y = jnp.concatenate(ypairs[g * ppg:(g + 1) * ppg], axis=1)
            seg = (y + dl_ref[:, sl] * xc_s[slot, rows, sl]) * z_s[slot, rows, sl]
            nrm = seg * lax.rsqrt(jnp.mean(seg * seg, axis=-1, keepdims=True) + EPS)
            outs.append((nrm * nw_ref[:, sl]).astype(BF16))
        return jnp.concatenate(outs, axis=1)

    for j in range(n):
        nxt = h_ref[:, (j + 1) * C:(j + 2) * C, :] if j + 1 < n else hn_ref[...]
        project(nxt.reshape(nb * C, D), (j + 1) % 2)
        o = jnp.concatenate([one_chunk(bi, j % 2) for bi in range(nb)], axis=0)
        o_ref[:, j * C:(j + 1) * C, :] = _dot(o, wo_ref[...]).astype(o_ref.dtype).reshape(nb, C, D)


def _ssd(h, wz, wx, wdt, wo, conv_w, conv_b, dt_bias, a_log, d_skip, norm_w, ts, nb):
    B, S, D = h.shape
    C = CHUNK
    n = ts // C
    assert n % 2 == 0
    inner = SSD_HEADS * SSD_P
    cdim = inner + 2 * SSD_GROUPS * SSD_N
    pad = LANES - SSD_HEADS
    wdt_p = jnp.pad(wdt, ((0, 0), (0, pad)))
    dtb = jnp.pad(dt_bias.astype(F32), (0, pad)).reshape(1, LANES)
    a = jnp.pad(-jnp.exp(a_log.astype(F32)), (0, pad)).reshape(1, LANES)
    dl = jnp.repeat(d_skip.astype(F32), SSD_P).reshape(1, inner)
    r = jnp.arange(C)
    tril = (r[:, None] >= r[None, :]).astype(F32)
    negm = jnp.where(r[:, None] >= r[None, :], 0.0, NEG_BIG).astype(F32)
    ex = (jnp.arange(LANES)[:, None] == (jnp.arange(inner)[None, :] // SSD_P)).astype(BF16)
    tile = lambda w: pl.BlockSpec((nb, ts, w), lambda b, s: (b, s, 0))
    nxt = pl.BlockSpec((nb, C, D), lambda b, s: (b, jnp.minimum((s + 1) * n, S // C - 1), 0))
    R = nb * C
    return pl.pallas_call(
        _ssd_kernel,
        grid=(B // nb, S // ts),
        in_specs=[tile(D), nxt, _const_spec((D, inner)), _const_spec((D, cdim)), _const_spec((D, LANES)),
                  _const_spec((inner, D)), _const_spec((SSD_CONV, cdim)), _const_spec((1, cdim)),
                  _const_spec((1, LANES)), _const_spec((1, LANES)), _const_spec((1, inner)),
                  _const_spec((1, inner)), _const_spec((C, C)), _const_spec((C, C)),
                  _const_spec((LANES, inner))],
        out_specs=tile(D),
        out_shape=jax.ShapeDtypeStruct((B, S, D), BF16),
        scratch_shapes=[pltpu.VMEM((nb * SSD_GROUPS, SSD_N, SSD_HPG * SSD_P), F32),
                        pltpu.VMEM((nb, SUBLANES, cdim), F32), pltpu.VMEM((R, cdim), F32),
                        pltpu.VMEM((2, R, cdim), F32),
                        pltpu.VMEM((2, R, inner), F32), pltpu.VMEM((2, R, LANES), F32)],
        compiler_params=_params(2),
        name="ssd",
    )(h, h, wz, wx, wdt_p, wo, conv_w.astype(F32), conv_b.astype(F32).reshape(1, cdim), dtb, a, dl,
      norm_w.astype(F32).reshape(1, inner), tril, negm, ex)


def _merge_kernel(x_ref, h_ref, yr_ref, yh_ref, ys_ref, wgl_ref, wout_ref, mod_ref, nw_ref,
                  wr_hi_ref, wr_lo_ref, br_ref, ut_ref, x1_ref, h2_ref, rf_ref, ri_ref, cnt_ref, run_ref):
    D = x_ref.shape[1]
    ts = x_ref.shape[0]

    @pl.when((pl.program_id(0) == 0) & (pl.program_id(1) == 0))
    def _():
        run_ref[...] = jnp.zeros_like(run_ref)

    h = h_ref[...]
    gates = _sigmoid(_dot(h, wgl_ref[...]))
    m = (gates[:, 0:D] * yr_ref[...].astype(F32) + gates[:, D:2 * D] * yh_ref[...].astype(F32)
         + gates[:, 2 * D:3 * D] * ys_ref[...].astype(F32))
    y = _dot(m.astype(BF16), wout_ref[...])
    x1 = x_ref[...] + mod_ref[2:3, :] * y
    x1_ref[...] = x1
    h2 = _rms_mod(x1, nw_ref[...], mod_ref[4:5, :], mod_ref[3:4, :])
    h2_ref[...] = h2
    h_hi = h2.astype(BF16)
    h_lo = (h2 - h_hi.astype(F32)).astype(BF16)
    lg = (_dot_nt(wr_hi_ref[...], h_hi) + _dot_nt(wr_hi_ref[...], h_lo) + _dot_nt(wr_lo_ref[...], h_hi)
          + br_ref[...])
    gl = lg[0:MOE_GROUPS]
    gmax = jnp.max(gl, axis=0, keepdims=True)
    gi = lax.broadcasted_iota(jnp.int32, gl.shape, 0)
    gtop = jnp.min(jnp.where(gl == gmax, gi, MOE_GROUPS), axis=0, keepdims=True)
    p_g = 1.0 / jnp.sum(jnp.exp(gl - gmax), axis=0, keepdims=True)
    el = jnp.zeros((MOE_PER_GROUP, ts), F32)
    for g in range(MOE_GROUPS):
        lo = SUBLANES + g * MOE_PER_GROUP
        el = jnp.where(gtop == g, lg[lo:lo + MOE_PER_GROUP], el)
    ei = lax.broadcasted_iota(jnp.int32, el.shape, 0)
    m1 = jnp.max(el, axis=0, keepdims=True)
    i1 = jnp.min(jnp.where(el == m1, ei, MOE_PER_GROUP), axis=0, keepdims=True)
    el2 = jnp.where(ei == i1, -jnp.inf, el)
    m2 = jnp.max(el2, axis=0, keepdims=True)
    i2 = jnp.min(jnp.where(el2 == m2, ei, MOE_PER_GROUP), axis=0, keepdims=True)
    e21 = jnp.exp(m2 - m1)
    den = 1.0 + e21
    w1 = (1.0 / den) * p_g
    w2 = (e21 / den) * p_g
    zf = jnp.zeros((SUBLANES - 2, ts), F32)
    rf_ref[...] = jnp.concatenate([w1, w2, zf], axis=0)
    e1 = gtop * MOE_PER_GROUP + i1
    e2 = gtop * MOE_PER_GROUP + i2
    ex = lax.broadcasted_iota(jnp.int32, (MOE_EXPERTS, ts), 0)
    oh1 = jnp.where(ex == e1, 1.0, 0.0)
    oh2 = jnp.where(ex == e2, 1.0, 0.0)
    p1 = _dot(oh1.astype(BF16), ut_ref[...])
    p2 = _dot(oh2.astype(BF16), ut_ref[...])
    c1 = p1[:, ts - 1:ts]
    c2 = p2[:, ts - 1:ts]
    run = run_ref[...]
    base = run[:, 0:1]
    r1 = jnp.sum(oh1 * (p1 - 1.0 + base), axis=0, keepdims=True)
    r2 = jnp.sum(oh2 * (p2 - 1.0 + base + c1), axis=0, keepdims=True)
    run = run + (c1 + c2)
    run_ref[...] = run
    cnt_ref[...] = run
    ri_ref[...] = jnp.concatenate([e1, e2, r1.astype(jnp.int32), r2.astype(jnp.int32),
                                   jnp.zeros((SUBLANES - 4, ts), jnp.int32)], axis=0)


def _merge(x, h, y_ret, y_hg, y_ssd, wgl, wout, mod_l, norm_w, wr_hi, wr_lo, br, ts):
    B, S, D = x.shape
    ns = S // ts
    R = wr_hi.shape[0]
    tile = pl.BlockSpec((None, ts, D), lambda b, s: (b, s, 0))
    rt = pl.BlockSpec((SUBLANES, ts), lambda b, s: (0, b * ns + s))
    r = jnp.arange(ts)
    ut = (r[:, None] <= r[None, :]).astype(BF16)
    return pl.pallas_call(
        _merge_kernel,
        grid=(B, ns),
        in_specs=[tile, tile, tile, tile, tile, _const_spec((D, 3 * D)), _const_spec((D, D)),
                  pl.BlockSpec((None, 6, D), lambda b, s: (b, 0, 0)), _const_spec((1, D)),
                  _const_spec((R, D)), _const_spec((R, D)), _const_spec((R, 1)), _const_spec((ts, ts))],
        out_specs=[tile, tile, rt, rt, pl.BlockSpec((MOE_EXPERTS, LANES), lambda b, s: (0, 0))],
        out_shape=[jax.ShapeDtypeStruct((B, S, D), F32), jax.ShapeDtypeStruct((B, S, D), F32),
                   jax.ShapeDtypeStruct((SUBLANES, B * S), F32), jax.ShapeDtypeStruct((SUBLANES, B * S), jnp.int32),
                   jax.ShapeDtypeStruct((MOE_EXPERTS, LANES), F32)],
        scratch_shapes=[pltpu.VMEM((MOE_EXPERTS, LANES), F32)],
        compiler_params=_params(2),
        name="merge_router",
    )(x, h, y_ret, y_hg, y_ssd, wgl, wout, mod_l, norm_w.reshape(1, D), wr_hi, wr_lo, br, ut)


def _row_copy(src, src_row, dst, dst_row, sem):
    return pltpu.make_async_copy(src.at[pl.ds(src_row, 1), :], dst.at[pl.ds(dst_row, 1), :], sem)


DISPATCH_TILE = 1024
DMA_UNROLL = 256


def _dispatch_kernel(pend_ref, plen_ref, nact_ref, dest_ref, h2_ref, xs_hbm, zbuf, sem, zsem):
    i = pl.program_id(0)
    td = dest_ref.shape[2]
    BLK = MOE_BLOCK
    nb = xs_hbm.shape[0] // BLK

    def zero_fill(wait):
        def go(start, size):
            cp = pltpu.make_async_copy(zbuf.at[pl.ds(0, size), :], xs_hbm.at[pl.ds(start, size), :], zsem)
            if wait:
                cp.wait()
            else:
                cp.start()

        for e in range(MOE_EXPERTS):
            ln = plen_ref[e]
            end = pend_ref[e]
            size = BLK // 2
            while size >= SUBLANES:
                @pl.when((ln & size) != 0)
                def _(size=size, ln=ln, end=end):
                    go(pl.multiple_of(end - (ln & ~(size - 1)), size), size)
                size //= 2
            for j in range(SUBLANES - 1):
                @pl.when(j < (ln & (SUBLANES - 1)))
                def _(j=j, ln=ln, end=end):
                    go(end - ln + j, 1)
        for j in range(MOE_EXPERTS):
            @pl.when(nact_ref[0] + j < nb)
            def _(j=j):
                go(pl.multiple_of((nact_ref[0] + j) * BLK, BLK), BLK)

    @pl.when(i == 0)
    def _():
        zbuf[...] = jnp.zeros_like(zbuf)
        zero_fill(False)

    def issue(j, c):
        base = pl.multiple_of(j * DMA_UNROLL, DMA_UNROLL)
        src = h2_ref.at[pl.ds(base, DMA_UNROLL), :]
        for u in range(DMA_UNROLL):
            for k in range(2):
                _row_copy(src, u, xs_hbm, dest_ref[0, k, base + u], sem).start(priority=k)
        return c

    lax.fori_loop(0, td // DMA_UNROLL, issue, 0)
    for k in range(2):
        pltpu.make_async_copy(h2_ref, xs_hbm.at[pl.ds(0, td), :], sem).wait()

    @pl.when(i == 0)
    def _():
        zero_fill(True)


def _dispatch(h2, dest, pad_end, pad_len, nactive, nrows):
    T, D = h2.shape
    td = min(DISPATCH_TILE, T)
    nt = T // td
    dest3 = dest.reshape(2, nt, td).transpose(1, 0, 2)
    grid_spec = pltpu.PrefetchScalarGridSpec(
        num_scalar_prefetch=3,
        grid=(nt,),
        in_specs=[pl.BlockSpec((1, 2, td), lambda i, *_: (i, 0, 0), memory_space=pltpu.SMEM),
                  pl.BlockSpec((td, D), lambda i, *_: (i, 0))],
        out_specs=pl.BlockSpec(memory_space=pl.ANY),
        scratch_shapes=[pltpu.VMEM((MOE_BLOCK, D), F32), pltpu.SemaphoreType.DMA(()),
                        pltpu.SemaphoreType.DMA(())])
    return pl.pallas_call(
        _dispatch_kernel,
        grid_spec=grid_spec,
        out_shape=jax.ShapeDtypeStruct((nrows, D), F32),
        compiler_params=_params(1),
        name="dispatch",
    )(pad_end, pad_len, nactive, dest3, h2)


def _expert_kernel(be_ref, na_ref, x_ref, w1_ref, w3_ref, w2_ref, y_ref, w1_s, w3_s, w2_s):
    i = pl.program_id(0)

    @pl.when((i == 0) | (be_ref[i] != be_ref[jnp.maximum(i - 1, 0)]))
    def _():
        w1_s[...] = w1_ref[...].astype(BF16)
        w3_s[...] = w3_ref[...].astype(BF16)
        w2_s[...] = w2_ref[...].astype(BF16)

    @pl.when(i < na_ref[0])
    def _():
        x = x_ref[...].astype(BF16)
        a = _dot(x, w1_s[...])
        b = _dot(x, w3_s[...])
        y_ref[...] = _dot((_silu(a) * b).astype(BF16), w2_s[...])

    @pl.when(i >= na_ref[0])
    def _():
        y_ref[...] = jnp.zeros_like(y_ref)


def _experts(xs, block_e, nactive, w1, w3, w2, layer):
    nrows, D = xs.shape
    BLK = MOE_BLOCK
    nb = nrows // BLK
    ff = w1.shape[3]
    wsel = lambda i, be, na: (layer, be[i], 0, 0)
    grid_spec = pltpu.PrefetchScalarGridSpec(
        num_scalar_prefetch=2,
        grid=(nb,),
        in_specs=[pl.BlockSpec((BLK, D), lambda i, be, na: (jnp.minimum(i, na[0] - 1), 0)),
                  pl.BlockSpec((None, None, D, ff), wsel), pl.BlockSpec((None, None, D, ff), wsel),
                  pl.BlockSpec((None, None, ff, D), wsel)],
        out_specs=pl.BlockSpec((BLK, D), lambda i, be, na: (i, 0)),
        scratch_shapes=[pltpu.VMEM((D, ff), BF16), pltpu.VMEM((D, ff), BF16), pltpu.VMEM((ff, D), BF16)])
    return pl.pallas_call(
        _expert_kernel,
        grid_spec=grid_spec,
        out_shape=jax.ShapeDtypeStruct((nrows, D), F32),
        compiler_params=_params(1),
        name="experts",
    )(block_e, nactive, xs, w1, w3, w2)


def _dispatch_plan(route_i, counts_f, T):
    BLK = MOE_BLOCK
    nb = -(-(2 * T) // BLK) + MOE_EXPERTS
    counts = counts_f[:, 0].astype(jnp.int32)
    padded = ((counts + BLK - 1) // BLK) * BLK
    pad_ends = jnp.cumsum(padded)
    pad_starts = pad_ends - padded
    e = route_i[0:2]
    rank = route_i[2:4]
    sel = e[:, :, None] == jnp.arange(MOE_EXPERTS, dtype=jnp.int32)[None, None, :]
    dest = rank + jnp.sum(jnp.where(sel, pad_starts[None, None, :], 0), axis=-1)
    nactive = (pad_ends[-1] // BLK).astype(jnp.int32)
    blk_id = jnp.minimum(jnp.arange(nb, dtype=jnp.int32), nactive - 1)
    block_e = jnp.sum((blk_id[:, None] * BLK >= pad_ends[None, :]).astype(jnp.int32), axis=1)
    block_e = jnp.minimum(block_e, MOE_EXPERTS - 1)
    return dest, block_e, pad_ends, padded - counts, nactive.reshape(1), nb * BLK


COMBINE_TILE = 256


def _combine_kernel(final, dcur_ref, dnxt_ref, x_ref, rw_ref, mod_ref, nw_ref, modn_ref, ys_hbm, *rest):
    if final:
        o_ref, gbuf, sem = rest
    else:
        x2_ref, h_ref, gbuf, sem = rest
    i = pl.program_id(0)
    n = pl.num_programs(0)
    ts = x_ref.shape[0]
    slot = i % 2

    def issue(dref, sl):
        for r in range(ts):
            for k in range(2):
                _row_copy(ys_hbm, dref[0, k, r], gbuf.at[sl, k], r, sem.at[sl]).start(priority=k)

    @pl.when(i == 0)
    def _():
        issue(dcur_ref, 0)

    @pl.when(i + 1 < n)
    def _():
        issue(dnxt_ref, 1 - slot)

    for k in range(2):
        pltpu.make_async_copy(ys_hbm.at[pl.ds(0, ts), :], gbuf.at[slot, k], sem.at[slot]).wait()
    y = rw_ref[:, 0:1] * gbuf[slot, 0] + rw_ref[:, 1:2] * gbuf[slot, 1]
    x2 = x_ref[...] + mod_ref[5:6, :] * y
    if final:
        o_ref[...] = (x2 * lax.rsqrt(jnp.mean(x2 * x2, axis=-1, keepdims=True) + EPS) * nw_ref[...]).astype(o_ref.dtype)
    else:
        x2_ref[...] = x2
        h_ref[...] = _rms_mod(x2, nw_ref[...], modn_ref[1:2, :], modn_ref[0:1, :]).astype(h_ref.dtype)


def _combine(x1, ys, dest, rw, mod_l, norm_w, mod_next, final, out_dtype):
    B, S, D = x1.shape
    T = B * S
    ts = min(COMBINE_TILE, S)
    nsb = S // ts
    nt = T // ts
    dest3 = dest.reshape(2, nt, ts).transpose(1, 0, 2)
    tile = pl.BlockSpec((ts, D), lambda i: (i, 0))
    modspec = pl.BlockSpec((None, 6, D), lambda i: (i // nsb, 0, 0))
    if final:
        out_specs = tile
        out_shape = jax.ShapeDtypeStruct((T, D), out_dtype)
    else:
        out_specs = [tile, tile]
        out_shape = [jax.ShapeDtypeStruct((T, D), F32), jax.ShapeDtypeStruct((T, D), BF16)]
    out = pl.pallas_call(
        functools.partial(_combine_kernel, final),
        grid=(nt,),
        in_specs=[pl.BlockSpec((1, 2, ts), lambda i: (i, 0, 0), memory_space=pltpu.SMEM),
                  pl.BlockSpec((1, 2, ts), lambda i: (jnp.minimum(i + 1, nt - 1), 0, 0), memory_space=pltpu.SMEM),
                  tile, pl.BlockSpec((ts, SUBLANES), lambda i: (i, 0)), modspec,
                  pl.BlockSpec((1, D), lambda i: (0, 0)), modspec,
                  pl.BlockSpec(memory_space=pl.ANY)],
        out_specs=out_specs,
        out_shape=out_shape,
        scratch_shapes=[pltpu.VMEM((2, 2, ts, D), F32), pltpu.SemaphoreType.DMA((2,))],
        compiler_params=_params(1),
        name="combine_final" if final else "combine_prenorm",
    )(dest3, dest3, x1.reshape(T, D), rw, mod_l, norm_w.reshape(1, D), mod_next, ys)
    if final:
        return out.reshape(B, S, D)
    return out[0].reshape(B, S, D), out[1].reshape(B, S, D)


def _seq_tile(S, want):
    ts = min(want, S)
    assert S % ts == 0 and ts % CHUNK == 0
    return ts


def kernel(x, c, positions, w_ada, b_ada, norm_mix_w, w_in, ssd_conv_w, ssd_conv_b, ssd_dt_bias, ssd_a_log,
           ssd_d, ssd_norm_w, hg_lb, hg_norm_w, w_ret_o, w_hg_o, w_ssd_o, w_out, norm_ffn_w, moe_w_group,
           moe_b_group, moe_w_expert, moe_b_expert, moe_w1, moe_w3, moe_w2, final_norm_w):
    B, S, D = x.shape
    L = w_ada.shape[0]
    T = B * S
    ts_el = _seq_tile(S, 512)
    ts_mix = _seq_tile(S, 256)
    nb_mix = 2 if B % 2 == 0 else 1

    mod = _adaln(c, w_ada, b_ada)
    cos, sin = _rope_tables(positions, ts_el)
    lb_soft = jax.nn.softmax(hg_lb.astype(F32), axis=0)
    lb_all = jnp.cumsum(lb_soft, axis=0) - lb_soft[0]

    qk = RET_HEADS * RET_DK
    vw = RET_HEADS * RET_DV
    hgw = HG_HEADS * HG_DK
    inner = SSD_HEADS * SSD_P
    cdim = inner + 2 * SSD_GROUPS * SSD_N
    sizes = (qk, qk, vw, vw, hgw, hgw, hgw, hgw, inner, cdim, SSD_HEADS, 3 * D)
    offs = [0]
    for n in sizes:
        offs.append(offs[-1] + n)

    xcur = x.astype(F32)
    h = _prenorm(xcur, norm_mix_w[0].astype(F32), mod[0], ts_el)
    for l in range(L):
        wl = w_in[l].astype(BF16)
        cols = [wl[:, offs[i]:offs[i + 1]] for i in range(len(sizes))]
        y_ret = _retention(h, cos, sin, cols[0], cols[1], cols[2], cols[3], w_ret_o[l].astype(BF16), ts_mix,
                           nb_mix)
        y_hg = _hgrn2(h, cols[4], cols[5], cols[6], cols[7], w_hg_o[l].astype(BF16), lb_all[l],
                      hg_norm_w[l].astype(F32), ts_mix, nb_mix)
        y_ssd = _ssd(h, cols[8], cols[9], cols[10], w_ssd_o[l].astype(BF16), ssd_conv_w[l], ssd_conv_b[l],
                     ssd_dt_bias[l], ssd_a_log[l], ssd_d[l], ssd_norm_w[l], ts_mix, 1)
        wr = jnp.zeros((SUBLANES + MOE_EXPERTS, D), F32)
        wr = wr.at[0:MOE_GROUPS].set(moe_w_group[l].astype(F32).T).at[SUBLANES:].set(moe_w_expert[l].astype(F32).T)
        br = jnp.zeros((SUBLANES + MOE_EXPERTS, 1), F32)
        br = br.at[0:MOE_GROUPS, 0].set(moe_b_group[l].astype(F32)).at[SUBLANES:, 0].set(moe_b_expert[l].astype(F32))
        wr_hi = wr.astype(BF16)
        wr_lo = (wr - wr_hi.astype(F32)).astype(BF16)
        x1, h2, route_w, route_i, counts = _merge(xcur, h, y_ret, y_hg, y_ssd, cols[11], w_out[l].astype(BF16),
                                                  mod[l], norm_ffn_w[l].astype(F32), wr_hi, wr_lo, br, ts_el)
        dest, block_e, pad_end, pad_len, nactive, nrows = _dispatch_plan(route_i, counts, T)
        xs = _dispatch(h2.reshape(T, D), dest, pad_end, pad_len, nactive, nrows)
        ys = _experts(xs, block_e, nactive, moe_w1, moe_w3, moe_w2, l)
        rw = route_w.T
        if l + 1 < L:
            xcur, h = _combine(x1, ys, dest, rw, mod[l], norm_mix_w[l + 1].astype(F32), mod[l + 1], False, x.dtype)
        else:
            out = _combine(x1, ys, dest, rw, mod[l], final_norm_w.astype(F32), mod[l], True, x.dtype)
    return out
```

```python
import functools

import jax
import jax.numpy as jnp
from jax import lax
from jax.experimental import pallas as pl
from jax.experimental.pallas import tpu as pltpu

F32 = jnp.float32
BF16 = jnp.bfloat16

EPS = 1e-6
ROPE_BASE = 10000.0
RET_HEADS = 4
RET_DK = 128
RET_DV = 256
HG_HEADS = 8
HG_DK = 128
SSD_HEADS = 32
SSD_P = 64
SSD_GROUPS = 4
SSD_N = 128
SSD_HPG = SSD_HEADS // SSD_GROUPS
SSD_CONV = 4
MOE_GROUPS = 4
MOE_PER_GROUP = 8
MOE_EXPERTS = 32
MOE_BLOCK = 512
CHUNK = 128
LANES = 128
SUBLANES = 8
VMEM_LIMIT = 56 * 1024 * 1024
HG_LEVELS = (64, 32, 16, 8, 4, 2, 1)
LOG2E = 1.4426950408889634
NEG_BIG = -1e30


def _params(n_axes, vmem=VMEM_LIMIT):
    return pltpu.CompilerParams(dimension_semantics=("arbitrary",) * n_axes,
                                vmem_limit_bytes=vmem)


def _const_spec(shape):
    nd = len(shape)
    return pl.BlockSpec(shape, lambda *_: (0,) * nd, pipeline_mode=pl.Buffered(1))


def _sigmoid(x):
    return 0.5 * jnp.tanh(0.5 * x) + 0.5


def _silu(x):
    h = 0.5 * x
    return h * jnp.tanh(h) + h


def _dot(a, b):
    return jnp.dot(a, b, preferred_element_type=F32)


def _dot_nt(a, b):
    return lax.dot_general(a, b, (((1,), (1,)), ((), ())), preferred_element_type=F32)


def _dot_tn(a, b):
    return lax.dot_general(a, b, (((0,), (0,)), ((), ())), preferred_element_type=F32)


def _dot2(m_bf16, x_f32):
    hi = x_f32.astype(BF16)
    lo = (x_f32 - hi.astype(F32)).astype(BF16)
    return _dot(m_bf16, hi) + _dot(m_bf16, lo)


def _ada_kernel(c_ref, w_ref, b_ref, o_ref):
    c = c_ref[...]
    o_ref[...] = _dot(_silu(c).astype(BF16), w_ref[...].astype(BF16)) + b_ref[...]


def _adaln(c, w_ada, b_ada):
    L, D, N = w_ada.shape
    B = c.shape[0]
    cp = jnp.zeros((SUBLANES, D), F32).at[:B].set(c.astype(F32))
    tn = 1536
    out = pl.pallas_call(
        _ada_kernel,
        grid=(L, N // tn),
        in_specs=[pl.BlockSpec((SUBLANES, D), lambda l, n: (0, 0)),
                  pl.BlockSpec((None, D, tn), lambda l, n: (l, 0, n)),
                  pl.BlockSpec((None, 1, tn), lambda l, n: (l, 0, n))],
        out_specs=pl.BlockSpec((None, SUBLANES, tn), lambda l, n: (l, 0, n)),
        out_shape=jax.ShapeDtypeStruct((L, SUBLANES, N), F32),
        compiler_params=_params(2),
        name="adaln",
    )(cp, w_ada, b_ada.reshape(L, 1, N))
    return out[:, :B].reshape(L, B, 6, D)


def _rope_kernel(pos_ref, invf_ref, sign_ref, cos_ref, sin_ref):
    ts = pos_ref.shape[0]
    hs = ts // 2
    lo = lax.broadcasted_iota(jnp.int32, (1, RET_DK), 1) < RET_DK // 2
    pos = jnp.where(lo, pos_ref[0:hs, :], pos_ref[hs:ts, :]).astype(F32)
    ang = pos * invf_ref[...]
    for fn, out_ref, scale in ((jnp.cos, cos_ref, None), (jnp.sin, sin_ref, sign_ref)):
        v = fn(ang)
        w = pltpu.roll(v, RET_DK // 2, 1)
        first = jnp.where(lo, v, w)
        second = jnp.where(lo, w, v)
        if scale is not None:
            first = first * scale[...]
            second = second * scale[...]
        out_ref[0:hs, :] = first
        out_ref[hs:ts, :] = second


def _rope_tables(positions, ts):
    B, S = positions.shape
    half = RET_DK // 2
    inv_freq = ROPE_BASE ** (-jnp.linspace(0.0, 1.0, half, dtype=F32))
    invf = jnp.concatenate([inv_freq, inv_freq]).reshape(1, RET_DK)
    sign = jnp.concatenate([-jnp.ones((half,), F32), jnp.ones((half,), F32)]).reshape(1, RET_DK)
    spec = pl.BlockSpec((None, ts, RET_DK), lambda b, s: (b, s, 0))
    return pl.pallas_call(
        _rope_kernel,
        grid=(B, S // ts),
        in_specs=[pl.BlockSpec((None, ts, 1), lambda b, s: (b, s, 0)),
                  pl.BlockSpec((1, RET_DK), lambda b, s: (0, 0)),
                  pl.BlockSpec((1, RET_DK), lambda b, s: (0, 0))],
        out_specs=[spec, spec],
        out_shape=[jax.ShapeDtypeStruct((B, S, RET_DK), F32)] * 2,
        compiler_params=_params(2),
        name="rope_tables",
    )(positions.reshape(B, S, 1), invf, sign)


def _rms_mod(x, w, scale, shift):
    y = x * lax.rsqrt(jnp.mean(x * x, axis=-1, keepdims=True) + EPS)
    return (y * w) * (1.0 + scale) + shift


def _prenorm_kernel(x_ref, w_ref, mod_ref, o_ref):
    o_ref[...] = _rms_mod(x_ref[...], w_ref[...], mod_ref[1:2, :], mod_ref[0:1, :]).astype(o_ref.dtype)


def _prenorm(x, w, mod_l, ts):
    B, S, D = x.shape
    return pl.pallas_call(
        _prenorm_kernel,
        grid=(B, S // ts),
        in_specs=[pl.BlockSpec((None, ts, D), lambda b, s: (b, s, 0)),
                  pl.BlockSpec((1, D), lambda b, s: (0, 0)),
                  pl.BlockSpec((None, 6, D), lambda b, s: (b, 0, 0))],
        out_specs=pl.BlockSpec((None, ts, D), lambda b, s: (b, s, 0)),
        out_shape=jax.ShapeDtypeStruct((B, S, D), BF16),
        compiler_params=_params(2),
        name="prenorm",
    )(x, w.reshape(1, D), mod_l)


def _ret_kernel(h_ref, hn_ref, cos_ref, cosn_ref, sin_ref, sinn_ref, wq_ref, wk_ref, wv_ref, wg_ref, wo_ref,
                di_ref, dq_ref, dk_ref, dc_ref, o_ref,
                st_ref, q_s, k_s, v_s, g_s):
    nb, ts, D = h_ref.shape
    C = CHUNK
    n = ts // C
    R = nb * C
    scale = RET_DK ** -0.5

    def project(hc, cos, sin, slot):
        hc = hc.reshape(R, D)
        cos = cos.reshape(R, RET_DK)
        sin = sin.reshape(R, RET_DK)
        q = _dot(hc, wq_ref[...])
        k = _dot(hc, wk_ref[...])
        for hh in range(RET_HEADS):
            sl = slice(hh * RET_DK, (hh + 1) * RET_DK)
            qh = q[:, sl]
            kh = k[:, sl]
            q_s[slot, :, sl] = (qh * cos + pltpu.roll(qh, RET_DK // 2, 1) * sin).astype(BF16)
            k_s[slot, :, sl] = (kh * cos + pltpu.roll(kh, RET_DK // 2, 1) * sin) * scale
        v_s[slot] = _dot(hc, wv_ref[...]).astype(BF16)
        g_s[slot] = _silu(_dot(hc, wg_ref[...]))

    @pl.when(pl.program_id(1) == 0)
    def _():
        st_ref[...] = jnp.zeros_like(st_ref)
        project(h_ref[:, 0:C, :], cos_ref[:, 0:C, :], sin_ref[:, 0:C, :], 0)

    def one_chunk(bi, slot):
        rows = slice(bi * C, (bi + 1) * C)
        outs = []
        for hh in range(RET_HEADS):
            ksl = slice(hh * RET_DK, (hh + 1) * RET_DK)
            vsl = slice(hh * RET_DV, (hh + 1) * RET_DV)
            qc = q_s[slot, rows, ksl]
            kc = k_s[slot, rows, ksl]
            vc = v_s[slot, rows, vsl]
            st = st_ref[bi * RET_HEADS + hh]
            scores = _dot_nt(qc, kc.astype(BF16)) * di_ref[hh]
            o = _dot(scores.astype(BF16), vc) + _dot(qc, st.astype(BF16)) * dq_ref[hh]
            st_ref[bi * RET_HEADS + hh] = st * dc_ref[hh] + _dot_tn((kc * dk_ref[hh]).astype(BF16), vc)
            mu = jnp.mean(o, axis=-1, keepdims=True)
            d = o - mu
            var = jnp.mean(d * d, axis=-1, keepdims=True)
            outs.append((g_s[slot, rows, vsl] * (d * lax.rsqrt(var + EPS))).astype(BF16))
        return jnp.concatenate(outs, axis=1)

    for j in range(n):
        if j + 1 < n:
            nsl = slice((j + 1) * C, (j + 2) * C)
            project(h_ref[:, nsl, :], cos_ref[:, nsl, :], sin_ref[:, nsl, :], (j + 1) % 2)
        else:
            project(hn_ref[...], cosn_ref[...], sinn_ref[...], (j + 1) % 2)
        o = jnp.concatenate([one_chunk(bi, j % 2) for bi in range(nb)], axis=0)
        o_ref[:, j * C:(j + 1) * C, :] = _dot(o, wo_ref[...]).astype(o_ref.dtype).reshape(nb, C, D)


def _retention(h, cos, sin, wq, wk, wv, wg, wo, ts, nb):
    B, S, D = h.shape
    H, C = RET_HEADS, CHUNK
    log_gamma = jnp.log(1.0 - 2.0 ** (-5.0 - jnp.arange(H, dtype=F32)))
    idx = jnp.arange(C, dtype=F32)
    rel = idx[:, None] - idx[None, :]
    di = jnp.where(rel >= 0.0, jnp.exp(log_gamma[:, None, None] * jnp.maximum(rel, 0.0)), 0.0)
    dq = jnp.broadcast_to(jnp.exp(log_gamma[:, None] * (idx[None, :] + 1.0))[:, :, None], (H, C, RET_DV))
    dk = jnp.broadcast_to(jnp.exp(log_gamma[:, None] * (C - 1.0 - idx[None, :]))[:, :, None], (H, C, RET_DK))
    dc = jnp.broadcast_to(jnp.exp(log_gamma * C)[:, None, None], (H, 1, RET_DV))
    qk, vw = H * RET_DK, H * RET_DV
    n = ts // C
    assert n % 2 == 0
    tile = lambda w: pl.BlockSpec((nb, ts, w), lambda b, s: (b, s, 0))
    nxt = lambda w: pl.BlockSpec((nb, C, w), lambda b, s: (b, jnp.minimum((s + 1) * n, S // C - 1), 0))
    R = nb * C
    return pl.pallas_call(
        _ret_kernel,
        grid=(B // nb, S // ts),
        in_specs=[tile(D), nxt(D), tile(RET_DK), nxt(RET_DK), tile(RET_DK), nxt(RET_DK),
                  _const_spec((D, qk)), _const_spec((D, qk)), _const_spec((D, vw)), _const_spec((D, vw)),
                  _const_spec((vw, D)),
                  _const_spec((H, C, C)), _const_spec((H, C, RET_DV)), _const_spec((H, C, RET_DK)),
                  _const_spec((H, 1, RET_DV))],
        out_specs=tile(D),
        out_shape=jax.ShapeDtypeStruct((B, S, D), BF16),
        scratch_shapes=[pltpu.VMEM((nb * H, RET_DK, RET_DV), F32),
                        pltpu.VMEM((2, R, qk), BF16), pltpu.VMEM((2, R, qk), F32),
                        pltpu.VMEM((2, R, vw), BF16), pltpu.VMEM((2, R, vw), F32)],
        compiler_params=_params(2),
        name="retention",
    )(h, h, cos, cos, sin, sin, wq, wk, wv, wg, wo, di, dq, dk, dc)


def _hg_level_matrices():
    C = CHUNK
    H = C // 2
    r = jnp.arange(C)
    stack = (r[:, None] >= r[None, :]).astype(BF16)
    t = jnp.arange(H)[:, None]
    s = jnp.arange(C)[None, :] % H
    masks = []
    for b in HG_LEVELS[1:]:
        masks.append(((t // (2 * b)) == (s // (2 * b))) & ((t & b) != 0) & ((s & b) == 0))
    masks.append(t == s)
    return stack, jnp.stack(masks).astype(F32)


def _hg_kernel(h_ref, hn_ref, wq_ref, wf_ref, wi_ref, wg_ref, wo_ref, lb_ref, nw_ref, cm_ref, mk_ref, o_ref,
               st_ref, q_s, k_s, v_s, lf_s, g_s):
    nb, ts, D = h_ref.shape
    C = CHUNK
    W = HG_HEADS * HG_DK
    n = ts // C
    lb = lb_ref[...]

    def project(hc, slot):
        q_s[slot] = _silu(_dot(hc, wq_ref[...]))
        hf = _dot(hc, wf_ref[...])
        lf_s[slot] = jnp.log(lb + (1.0 - lb) * _sigmoid(hf)) * LOG2E
        k_s[slot] = (1.0 - lb) * _sigmoid(-hf)
        v_s[slot] = _dot(hc, wi_ref[...]).astype(BF16)
        g_s[slot] = _silu(_dot(hc, wg_ref[...]))

    @pl.when(pl.program_id(1) == 0)
    def _():
        st_ref[...] = jnp.zeros_like(st_ref)
        project(h_ref[:, 0:C, :].reshape(nb * C, D), 0)

    odd_row = {b: (lax.broadcasted_iota(jnp.int32, (C, 1), 0) & b) != 0 for b in HG_LEVELS}

    def one_chunk(bi, slot):
        rows = slice(bi * C, (bi + 1) * C)
        lf = lf_s[slot, rows, :]
        q = q_s[slot, rows, :]
        k = k_s[slot, rows, :]
        G = _dot2(cm_ref[...], lf)
        H = C // 2
        acc = [None] * HG_HEADS
        cross = [None] * HG_HEADS
        left = lax.broadcasted_iota(jnp.int32, (H, C), 1) < H
        G3 = G.reshape(C // SUBLANES, SUBLANES, W)
        sub8 = lax.broadcasted_iota(jnp.int32, (C // SUBLANES, SUBLANES, 1), 1)

        def tile_row(i):
            return jnp.broadcast_to(G3[:, i:i + 1, :], G3.shape)
        for li, b in enumerate(HG_LEVELS):
            if b >= SUBLANES:
                pieces = []
                for j in range(C // b):
                    ref = G[(j // 2) * 2 * b + b - 1:(j // 2) * 2 * b + b, :]
                    blk = slice(j * b, (j + 1) * b)
                    if j % 2 == 1:
                        pieces.append(q[blk] * jnp.exp2(G[blk] - ref))
                    else:
                        pieces.append(k[blk] * jnp.exp2(ref - G[blk]))
                z = jnp.concatenate(pieces, axis=0)
            elif b > 1:
                ref = tile_row(3) if b == 4 else jnp.where(sub8 < 4, tile_row(1), tile_row(5))
                e = -jnp.abs(G3 - ref)
                z = jnp.where(odd_row[b], q, k) * jnp.exp2(e.reshape(C, W))
            else:
                z = jnp.where(odd_row[b], q * jnp.exp2(lf), k)
            z = z.astype(BF16)
            for hh in range(HG_HEADS):
                zh = z[:, hh * HG_DK:(hh + 1) * HG_DK]
                if li == 0:
                    cross[hh] = _dot_nt(zh[H:], zh)
                else:
                    zt, zb = zh[:H], zh[H:]
                    zero = jnp.zeros_like(zt)
                    rhs = jnp.concatenate([jnp.concatenate([zt, zero], axis=1),
                                           jnp.concatenate([zero, zb], axis=1)], axis=0)
                    a = _dot_nt(jnp.concatenate([zt, zb], axis=1), rhs) * mk_ref[li - 1]
                    acc[hh] = a if li == 1 else acc[hh] + a
        g_last = G[C - 1:C, :]
        qg = (q * jnp.exp2(G)).astype(BF16)
        kh = (k * jnp.exp2(g_last - G)).astype(BF16)
        qk = q * k
        dec = jnp.exp2(g_last)
        eye = mk_ref[len(HG_LEVELS) - 1]
        outs = []
        for hh in range(HG_HEADS):
            sl = slice(hh * HG_DK, (hh + 1) * HG_DK)
            vh = v_s[slot, rows, sl]
            diag = jnp.sum(qk[:, sl], axis=-1, keepdims=True)
            dblk = acc[hh] + jnp.where(left, diag[:H], diag[H:]) * eye
            a_top = jnp.where(left, dblk, 0.0).astype(BF16)
            a_bot = jnp.where(left, cross[hh], dblk).astype(BF16)
            st = st_ref[bi * HG_HEADS + hh]
            o = (jnp.concatenate([_dot(a_top, vh), _dot(a_bot, vh)], axis=0)
                 + _dot_nt(qg[:, sl], st.astype(BF16)))
            st_ref[bi * HG_HEADS + hh] = st * dec[:, sl] + _dot_tn(vh, kh[:, sl])
            on = o * lax.rsqrt(jnp.mean(o * o, axis=-1, keepdims=True) + EPS) * nw_ref[...]
            outs.append((on * g_s[slot, rows, sl]).astype(BF16))
        return jnp.concatenate(outs, axis=1)

    for j in range(n):
        nxt = h_ref[:, (j + 1) * C:(j + 2) * C, :] if j + 1 < n else hn_ref[...]
        project(nxt.reshape(nb * C, D), (j + 1) % 2)
        o = jnp.concatenate([one_chunk(bi, j % 2) for bi in range(nb)], axis=0)
        o_ref[:, j * C:(j + 1) * C, :] = _dot(o, wo_ref[...]).astype(o_ref.dtype).reshape(nb, C, D)


def _hgrn2(h, wq, wf, wi, wg, wo, lb, norm_w, ts, nb):
    B, S, D = h.shape
    W = HG_HEADS * HG_DK
    C = CHUNK
    n = ts // C
    assert n % 2 == 0
    R = nb * C
    cm, mk = _hg_level_matrices()
    tile = lambda w: pl.BlockSpec((nb, ts, w), lambda b, s: (b, s, 0))
    nxt = pl.BlockSpec((nb, C, D), lambda b, s: (b, jnp.minimum((s + 1) * n, S // C - 1), 0))
    return pl.pallas_call(
        _hg_kernel,
        grid=(B // nb, S // ts),
        in_specs=[tile(D), nxt] + [_const_spec((D, W))] * 4 + [_const_spec((W, D)), _const_spec((1, W)),
                  _const_spec((1, HG_DK)), _const_spec(cm.shape), _const_spec(mk.shape)],
        out_specs=tile(D),
        out_shape=jax.ShapeDtypeStruct((B, S, D), BF16),
        scratch_shapes=[pltpu.VMEM((nb * HG_HEADS, HG_DK, HG_DK), F32),
                        pltpu.VMEM((2, R, W), F32), pltpu.VMEM((2, R, W), F32), pltpu.VMEM((2, R, W), BF16),
                        pltpu.VMEM((2, R, W), F32), pltpu.VMEM((2, R, W), F32)],
        compiler_params=_params(2),
        name="hgrn2",
    )(h, h, wq, wf, wi, wg, wo, lb.reshape(1, W), norm_w.reshape(1, HG_DK), cm, mk)


def _shift_rows(x, k, head):
    r, w = x.shape
    t = x.reshape(r // SUBLANES, SUBLANES, w)
    rot = pltpu.roll(t, k, 1)
    hrot = pltpu.roll(head.reshape(1, SUBLANES, w), k, 1)
    prev = jnp.concatenate([hrot, rot[:-1]], axis=0)
    sub = lax.broadcasted_iota(jnp.int32, (1, SUBLANES, 1), 1)
    return jnp.where(sub < k, prev, rot).reshape(r, w)


def _softplus(x):
    return jnp.maximum(x, 0.0) + jnp.log(1.0 + jnp.exp(-jnp.abs(x)))


def _ssd_kernel(h_ref, hn_ref, wz_ref, wx_ref, wdt_ref, wo_ref, cw_ref, cb_ref, dtb_ref, a_ref, dl_ref, nw_ref,
                tril_ref, negm_ref, ex_ref, o_ref,
                st_ref, halo_s, u_s, xc_s, z_s, dt_s):
    nb, ts, D = h_ref.shape
    C = CHUNK
    n = ts // C
    inner = SSD_HEADS * SSD_P
    gw = SSD_GROUPS * SSD_N
    gp = SSD_HPG * SSD_P
    halo = SUBLANES

    def project(hc, slot):
        z_s[slot] = _silu(_dot(hc, wz_ref[...]))
        dt_s[slot] = _softplus(_dot(hc, wdt_ref[...]) + dtb_ref[...])
        u_s[...] = _dot(hc, wx_ref[...])
        for bi in range(nb):
            x = u_s[bi * C:(bi + 1) * C, :]
            hl = halo_s[bi]
            x1 = _shift_rows(x, 1, hl)
            pair = cw_ref[1:2, :] * x + cw_ref[0:1, :] * x1
            pair_hl = cw_ref[1:2, :] * hl + cw_ref[0:1, :] * pltpu.roll(hl, 1, 0)
            y = (cw_ref[3:4, :] * x + cw_ref[2:3, :] * x1) + _shift_rows(pair, 2, pair_hl)
            xc_s[slot, bi * C:(bi + 1) * C, :] = _silu(y + cb_ref[...])
            halo_s[bi] = u_s[(bi + 1) * C - halo:(bi + 1) * C, :]

    @pl.when(pl.program_id(1) == 0)
    def _():
        st_ref[...] = jnp.zeros_like(st_ref)
        halo_s[...] = jnp.zeros_like(halo_s)
        project(h_ref[:, 0:C, :].reshape(nb * C, D), 0)

    low_half = lax.broadcasted_iota(jnp.int32, (C, LANES), 1) < SSD_P

    def one_chunk(bi, slot):
        rows = slice(bi * C, (bi + 1) * C)
        dt = dt_s[slot, rows, :]
        da = dt * a_ref[...]
        cum = jnp.dot(tril_ref[...], da, preferred_element_type=F32, precision=lax.Precision.HIGHEST) * LOG2E
        cum_t = cum.T
        row_t = (cum - jnp.log(dt) * LOG2E).T
        cum_last = cum[C - 1:C, :]
        wfac = dt * jnp.exp2(cum_last - cum)
        w_hi = wfac.astype(BF16)
        w_lo = (wfac - w_hi.astype(F32)).astype(BF16)
        wexp = _dot(w_hi, ex_ref[...]) + _dot(w_lo, ex_ref[...])
        dl = jnp.broadcast_to(jnp.exp2(cum_last), (SUBLANES, LANES))
        d_hi = dl.astype(BF16)
        d_lo = (dl - d_hi.astype(F32)).astype(BF16)
        dexp = (_dot(d_hi, ex_ref[...]) + _dot(d_lo, ex_ref[...]))[0:1, :]
        ypairs = []
        for g in range(SSD_GROUPS):
            bg = xc_s[slot, rows, inner + g * SSD_N:inner + (g + 1) * SSD_N]
            cg = xc_s[slot, rows, inner + gw + g * SSD_N:inner + gw + (g + 1) * SSD_N]
            bgb = bg.astype(BF16)
            cb = _dot_nt(cg.astype(BF16), bgb)
            st = st_ref[bi * SSD_GROUPS + g]
            stb = st.astype(BF16)
            xg = xc_s[slot, rows, g * gp:(g + 1) * gp]
            xgb = xg.astype(BF16)
            for jp in range(SSD_HPG // 2):
                lsl = slice(jp * LANES, (jp + 1) * LANES)
                rhs = jnp.concatenate([xgb[:, lsl], stb[:, lsl]], axis=0)
                ys = []
                for hd in (g * SSD_HPG + 2 * jp, g * SSD_HPG + 2 * jp + 1):
                    colb = jnp.broadcast_to(cum[:, hd:hd + 1], (C, C))
                    w = cb * jnp.exp2(colb - row_t[hd:hd + 1, :] + negm_ref[...])
                    ce = cg * jnp.exp2(colb)
                    ys.append(_dot(jnp.concatenate([w.astype(BF16), ce.astype(BF16)], axis=1), rhs))
                ypairs.append(jnp.where(low_half, ys[0], ys[1]))
            gsl = slice(g * gp, (g + 1) * gp)
            st_ref[bi * SSD_GROUPS + g] = st * dexp[:, gsl] + _dot_tn(bgb, (xg * wexp[:, gsl]).astype(BF16))
        gi = inner // SSD_GROUPS
        ppg = gi // LANES
        outs = []
        for g in range(SSD_GROUPS):
            sl = slice(g * gi, (g + 1) * gi)
            y = jnp.concatenate(ypairs[g * ppg:(g + 1) * ppg], axis=1)
            seg = (y + dl_ref[:, sl] * xc_s[slot, rows, sl]) * z_s[slot, rows, sl]
            nrm = seg * lax.rsqrt(jnp.mean(seg * seg, axis=-1, keepdims=True) + EPS)
            outs.append((nrm * nw_ref[:, sl]).astype(BF16))
        return jnp.concatenate(outs, axis=1)

    for j in range(n):
        nxt = h_ref[:, (j + 1) * C:(j + 2) * C, :] if j + 1 < n else hn_ref[...]
        project(nxt.reshape(nb * C, D), (j + 1) % 2)
        o = jnp.concatenate([one_chunk(bi, j % 2) for bi in range(nb)], axis=0)
        o_ref[:, j * C:(j + 1) * C, :] = _dot(o, wo_ref[...]).astype(o_ref.dtype).reshape(nb, C, D)


def _ssd(h, wz, wx, wdt, wo, conv_w, conv_b, dt_bias, a_log, d_skip, norm_w, ts, nb):
    B, S, D = h.shape
    C = CHUNK
    n = ts // C
    assert n % 2 == 0
    inner = SSD_HEADS * SSD_P
    cdim = inner + 2 * SSD_GROUPS * SSD_N
    pad = LANES - SSD_HEADS
    wdt_p = jnp.pad(wdt, ((0, 0), (0, pad)))
    dtb = jnp.pad(dt_bias.astype(F32), (0, pad)).reshape(1, LANES)
    a = jnp.pad(-jnp.exp(a_log.astype(F32)), (0, pad)).reshape(1, LANES)
    dl = jnp.repeat(d_skip.astype(F32), SSD_P).reshape(1, inner)
    r = jnp.arange(C)
    tril = (r[:, None] >= r[None, :]).astype(F32)
    negm = jnp.where(r[:, None] >= r[None, :], 0.0, NEG_BIG).astype(F32)
    ex = (jnp.arange(LANES)[:, None] == (jnp.arange(inner)[None, :] // SSD_P)).astype(BF16)
    tile = lambda w: pl.BlockSpec((nb, ts, w), lambda b, s: (b, s, 0))
    nxt = pl.BlockSpec((nb, C, D), lambda b, s: (b, jnp.minimum((s + 1) * n, S // C - 1), 0))
    R = nb * C
    return pl.pallas_call(
        _ssd_kernel,
        grid=(B // nb, S // ts),
        in_specs=[tile(D), nxt, _const_spec((D, inner)), _const_spec((D, cdim)), _const_spec((D, LANES)),
                  _const_spec((inner, D)), _const_spec((SSD_CONV, cdim)), _const_spec((1, cdim)),
                  _const_spec((1, LANES)), _const_spec((1, LANES)), _const_spec((1, inner)),
                  _const_spec((1, inner)), _const_spec((C, C)), _const_spec((C, C)),
                  _const_spec((LANES, inner))],
        out_specs=tile(D),
        out_shape=jax.ShapeDtypeStruct((B, S, D), BF16),
        scratch_shapes=[pltpu.VMEM((nb * SSD_GROUPS, SSD_N, SSD_HPG * SSD_P), F32),
                        pltpu.VMEM((nb, SUBLANES, cdim), F32), pltpu.VMEM((R, cdim), F32),
                        pltpu.VMEM((2, R, cdim), F32),
                        pltpu.VMEM((2, R, inner), F32), pltpu.VMEM((2, R, LANES), F32)],
        compiler_params=_params(2),
        name="ssd",
    )(h, h, wz, wx, wdt_p, wo, conv_w.astype(F32), conv_b.astype(F32).reshape(1, cdim), dtb, a, dl,
      norm_w.astype(F32).reshape(1, inner), tril, negm, ex)


def _merge_kernel(x_ref, h_ref, yr_ref, yh_ref, ys_ref, wgl_ref, wout_ref, mod_ref, nw_ref,
                  wr_hi_ref, wr_lo_ref, br_ref, ut_ref, x1_ref, h2_ref, rf_ref, ri_ref, cnt_ref, run_ref):
    D = x_ref.shape[1]
    ts = x_ref.shape[0]

    @pl.when((pl.program_id(0) == 0) & (pl.program_id(1) == 0))
    def _():
        run_ref[...] = jnp.zeros_like(run_ref)

    h = h_ref[...]
    gates = _sigmoid(_dot(h, wgl_ref[...]))
    m = (gates[:, 0:D] * yr_ref[...].astype(F32) + gates[:, D:2 * D] * yh_ref[...].astype(F32)
         + gates[:, 2 * D:3 * D] * ys_ref[...].astype(F32))
    y = _dot(m.astype(BF16), wout_ref[...])
    x1 = x_ref[...] + mod_ref[2:3, :] * y
    x1_ref[...] = x1
    h2 = _rms_mod(x1, nw_ref[...], mod_ref[4:5, :], mod_ref[3:4, :])
    h2_ref[...] = h2
    h_hi = h2.astype(BF16)
    h_lo = (h2 - h_hi.astype(F32)).astype(BF16)
    lg = (_dot_nt(wr_hi_ref[...], h_hi) + _dot_nt(wr_hi_ref[...], h_lo) + _dot_nt(wr_lo_ref[...], h_hi)
          + br_ref[...])
    gl = lg[0:MOE_GROUPS]
    gmax = jnp.max(gl, axis=0, keepdims=True)
    gi = lax.broadcasted_iota(jnp.int32, gl.shape, 0)
    gtop = jnp.min(jnp.where(gl == gmax, gi, MOE_GROUPS), axis=0, keepdims=True)
    p_g = 1.0 / jnp.sum(jnp.exp(gl - gmax), axis=0, keepdims=True)
    el = jnp.zeros((MOE_PER_GROUP, ts), F32)
    for g in range(MOE_GROUPS):
        lo = SUBLANES + g * MOE_PER_GROUP
        el = jnp.where(gtop == g, lg[lo:lo + MOE_PER_GROUP], el)
    ei = lax.broadcasted_iota(jnp.int32, el.shape, 0)
    m1 = jnp.max(el, axis=0, keepdims=True)
    i1 = jnp.min(jnp.where(el == m1, ei, MOE_PER_GROUP), axis=0, keepdims=True)
    el2 = jnp.where(ei == i1, -jnp.inf, el)
    m2 = jnp.max(el2, axis=0, keepdims=True)
    i2 = jnp.min(jnp.where(el2 == m2, ei, MOE_PER_GROUP), axis=0, keepdims=True)
    e21 = jnp.exp(m2 - m1)
    den = 1.0 + e21
    w1 = (1.0 / den) * p_g
    w2 = (e21 / den) * p_g
    zf = jnp.zeros((SUBLANES - 2, ts), F32)
    rf_ref[...] = jnp.concatenate([w1, w2, zf], axis=0)
    e1 = gtop * MOE_PER_GROUP + i1
    e2 = gtop * MOE_PER_GROUP + i2
    ex = lax.broadcasted_iota(jnp.int32, (MOE_EXPERTS, ts), 0)
    oh1 = jnp.where(ex == e1, 1.0, 0.0)
    oh2 = jnp.where(ex == e2, 1.0, 0.0)
    p1 = _dot(oh1.astype(BF16), ut_ref[...])
    p2 = _dot(oh2.astype(BF16), ut_ref[...])
    c1 = p1[:, ts - 1:ts]
    c2 = p2[:, ts - 1:ts]
    run = run_ref[...]
    base = run[:, 0:1]
    r1 = jnp.sum(oh1 * (p1 - 1.0 + base), axis=0, keepdims=True)
    r2 = jnp.sum(oh2 * (p2 - 1.0 + base + c1), axis=0, keepdims=True)
    run = run + (c1 + c2)
    run_ref[...] = run
    cnt_ref[...] = run
    ri_ref[...] = jnp.concatenate([e1, e2, r1.astype(jnp.int32), r2.astype(jnp.int32),
                                   jnp.zeros((SUBLANES - 4, ts), jnp.int32)], axis=0)


def _merge(x, h, y_ret, y_hg, y_ssd, wgl, wout, mod_l, norm_w, wr_hi, wr_lo, br, ts):
    B, S, D = x.shape
    ns = S // ts
    R = wr_hi.shape[0]
    tile = pl.BlockSpec((None, ts, D), lambda b, s: (b, s, 0))
    rt = pl.BlockSpec((SUBLANES, ts), lambda b, s: (0, b * ns + s))
    r = jnp.arange(ts)
    ut = (r[:, None] <= r[None, :]).astype(BF16)
    return pl.pallas_call(
        _merge_kernel,
        grid=(B, ns),
        in_specs=[tile, tile, tile, tile, tile, _const_spec((D, 3 * D)), _const_spec((D, D)),
                  pl.BlockSpec((None, 6, D), lambda b, s: (b, 0, 0)), _const_spec((1, D)),
                  _const_spec((R, D)), _const_spec((R, D)), _const_spec((R, 1)), _const_spec((ts, ts))],
        out_specs=[tile, tile, rt, rt, pl.BlockSpec((MOE_EXPERTS, LANES), lambda b, s: (0, 0))],
        out_shape=[jax.ShapeDtypeStruct((B, S, D), F32), jax.ShapeDtypeStruct((B, S, D), F32),
                   jax.ShapeDtypeStruct((SUBLANES, B * S), F32), jax.ShapeDtypeStruct((SUBLANES, B * S), jnp.int32),
                   jax.ShapeDtypeStruct((MOE_EXPERTS, LANES), F32)],
        scratch_shapes=[pltpu.VMEM((MOE_EXPERTS, LANES), F32)],
        compiler_params=_params(2),
        name="merge_router",
    )(x, h, y_ret, y_hg, y_ssd, wgl, wout, mod_l, norm_w.reshape(1, D), wr_hi, wr_lo, br, ut)


def _row_copy(src, src_row, dst, dst_row, sem):
    return pltpu.make_async_copy(src.at[pl.ds(src_row, 1), :], dst.at[pl.ds(dst_row, 1), :], sem)


DISPATCH_TILE = 1024
DMA_UNROLL = 256


def _dispatch_kernel(pend_ref, plen_ref, nact_ref, dest_ref, h2_ref, xs_hbm, zbuf, sem, zsem):
    i = pl.program_id(0)
    td = dest_ref.shape[2]
    BLK = MOE_BLOCK
    nb = xs_hbm.shape[0] // BLK

    def zero_fill(wait):
        def go(start, size):
            cp = pltpu.make_async_copy(zbuf.at[pl.ds(0, size), :], xs_hbm.at[pl.ds(start, size), :], zsem)
            if wait:
                cp.wait()
            else:
                cp.start()

        for e in range(MOE_EXPERTS):
            ln = plen_ref[e]
            end = pend_ref[e]
            size = BLK // 2
            while size >= SUBLANES:
                @pl.when((ln & size) != 0)
                def _(size=size, ln=ln, end=end):
                    go(pl.multiple_of(end - (ln & ~(size - 1)), size), size)
                size //= 2
            for j in range(SUBLANES - 1):
                @pl.when(j < (ln & (SUBLANES - 1)))
                def _(j=j, ln=ln, end=end):
                    go(end - ln + j, 1)
        for j in range(MOE_EXPERTS):
            @pl.when(nact_ref[0] + j < nb)
            def _(j=j):
                go(pl.multiple_of((nact_ref[0] + j) * BLK, BLK), BLK)

    @pl.when(i == 0)
    def _():
        zbuf[...] = jnp.zeros_like(zbuf)
        zero_fill(False)

    def issue(j, c):
        base = pl.multiple_of(j * DMA_UNROLL, DMA_UNROLL)
        src = h2_ref.at[pl.ds(base, DMA_UNROLL), :]
        for u in range(DMA_UNROLL):
            for k in range(2):
                _row_copy(src, u, xs_hbm, dest_ref[0, k, base + u], sem).start(priority=k)
        return c

    lax.fori_loop(0, td // DMA_UNROLL, issue, 0)
    for k in range(2):
        pltpu.make_async_copy(h2_ref, xs_hbm.at[pl.ds(0, td), :], sem).wait()

    @pl.when(i == 0)
    def _():
        zero_fill(True)


def _dispatch(h2, dest, pad_end, pad_len, nactive, nrows):
    T, D = h2.shape
    td = min(DISPATCH_TILE, T)
    nt = T // td
    dest3 = dest.reshape(2, nt, td).transpose(1, 0, 2)
    grid_spec = pltpu.PrefetchScalarGridSpec(
        num_scalar_prefetch=3,
        grid=(nt,),
        in_specs=[pl.BlockSpec((1, 2, td), lambda i, *_: (i, 0, 0), memory_space=pltpu.SMEM),
                  pl.BlockSpec((td, D), lambda i, *_: (i, 0))],
        out_specs=pl.BlockSpec(memory_space=pl.ANY),
        scratch_shapes=[pltpu.VMEM((MOE_BLOCK, D), F32), pltpu.SemaphoreType.DMA(()),
                        pltpu.SemaphoreType.DMA(())])
    return pl.pallas_call(
        _dispatch_kernel,
        grid_spec=grid_spec,
        out_shape=jax.ShapeDtypeStruct((nrows, D), F32),
        compiler_params=_params(1),
        name="dispatch",
    )(pad_end, pad_len, nactive, dest3, h2)


def _expert_kernel(be_ref, na_ref, x_ref, w1_ref, w3_ref, w2_ref, y_ref, w1_s, w3_s, w2_s):
    i = pl.program_id(0)

    @pl.when((i == 0) | (be_ref[i] != be_ref[jnp.maximum(i - 1, 0)]))
    def _():
        w1_s[...] = w1_ref[...].astype(BF16)
        w3_s[...] = w3_ref[...].astype(BF16)
        w2_s[...] = w2_ref[...].astype(BF16)

    @pl.when(i < na_ref[0])
    def _():
        x = x_ref[...].astype(BF16)
        a = _dot(x, w1_s[...])
        b = _dot(x, w3_s[...])
        y_ref[...] = _dot((_silu(a) * b).astype(BF16), w2_s[...])

    @pl.when(i >= na_ref[0])
    def _():
        y_ref[...] = jnp.zeros_like(y_ref)


def _experts(xs, block_e, nactive, w1, w3, w2, layer):
    nrows, D = xs.shape
    BLK = MOE_BLOCK
    nb = nrows // BLK
    ff = w1.shape[3]
    wsel = lambda i, be, na: (layer, be[i], 0, 0)
    grid_spec = pltpu.PrefetchScalarGridSpec(
        num_scalar_prefetch=2,
        grid=(nb,),
        in_specs=[pl.BlockSpec((BLK, D), lambda i, be, na: (jnp.minimum(i, na[0] - 1), 0)),
                  pl.BlockSpec((None, None, D, ff), wsel), pl.BlockSpec((None, None, D, ff), wsel),
                  pl.BlockSpec((None, None, ff, D), wsel)],
        out_specs=pl.BlockSpec((BLK, D), lambda i, be, na: (i, 0)),
        scratch_shapes=[pltpu.VMEM((D, ff), BF16), pltpu.VMEM((D, ff), BF16), pltpu.VMEM((ff, D), BF16)])
    return pl.pallas_call(
        _expert_kernel,
        grid_spec=grid_spec,
        out_shape=jax.ShapeDtypeStruct((nrows, D), F32),
        compiler_params=_params(1),
        name="experts",
    )(block_e, nactive, xs, w1, w3, w2)


def _dispatch_plan(route_i, counts_f, T):
    BLK = MOE_BLOCK
    nb = -(-(2 * T) // BLK) + MOE_EXPERTS
    counts = counts_f[:, 0].astype(jnp.int32)
    padded = ((counts + BLK - 1) // BLK) * BLK
    pad_ends = jnp.cumsum(padded)
    pad_starts = pad_ends - padded
    e = route_i[0:2]
    rank = route_i[2:4]
    sel = e[:, :, None] == jnp.arange(MOE_EXPERTS, dtype=jnp.int32)[None, None, :]
    dest = rank + jnp.sum(jnp.where(sel, pad_starts[None, None, :], 0), axis=-1)
    nactive = (pad_ends[-1] // BLK).astype(jnp.int32)
    blk_id = jnp.minimum(jnp.arange(nb, dtype=jnp.int32), nactive - 1)
    block_e = jnp.sum((blk_id[:, None] * BLK >= pad_ends[None, :]).astype(jnp.int32), axis=1)
    block_e = jnp.minimum(block_e, MOE_EXPERTS - 1)
    return dest, block_e, pad_ends, padded - counts, nactive.reshape(1), nb * BLK


COMBINE_TILE = 256


def _combine_kernel(final, dcur_ref, dnxt_ref, x_ref, rw_ref, mod_ref, nw_ref, modn_ref, ys_hbm, *rest):
    if final:
        o_ref, gbuf, sem = rest
    else:
        x2_ref, h_ref, gbuf, sem = rest
    i = pl.program_id(0)
    n = pl.num_programs(0)
    ts = x_ref.shape[0]
    slot = i % 2

    def issue(dref, sl):
        for r in range(ts):
            for k in range(2):
                _row_copy(ys_hbm, dref[0, k, r], gbuf.at[sl, k], r, sem.at[sl]).start(priority=k)

    @pl.when(i == 0)
    def _():
        issue(dcur_ref, 0)

    @pl.when(i + 1 < n)
    def _():
        issue(dnxt_ref, 1 - slot)

    for k in range(2):
        pltpu.make_async_copy(ys_hbm.at[pl.ds(0, ts), :], gbuf.at[slot, k], sem.at[slot]).wait()
    y = rw_ref[:, 0:1] * gbuf[slot, 0] + rw_ref[:, 1:2] * gbuf[slot, 1]
    x2 = x_ref[...] + mod_ref[5:6, :] * y
    if final:
        o_ref[...] = (x2 * lax.rsqrt(jnp.mean(x2 * x2, axis=-1, keepdims=True) + EPS) * nw_ref[...]).astype(o_ref.dtype)
    else:
        x2_ref[...] = x2
        h_ref[...] = _rms_mod(x2, nw_ref[...], modn_ref[1:2, :], modn_ref[0:1, :]).astype(h_ref.dtype)


def _combine(x1, ys, dest, rw, mod_l, norm_w, mod_next, final, out_dtype):
    B, S, D = x1.shape
    T = B * S
    ts = min(COMBINE_TILE, S)
    nsb = S // ts
    nt = T // ts
    dest3 = dest.reshape(2, nt, ts).transpose(1, 0, 2)
    tile = pl.BlockSpec((ts, D), lambda i: (i, 0))
    modspec = pl.BlockSpec((None, 6, D), lambda i: (i // nsb, 0, 0))
    if final:
        out_specs = tile
        out_shape = jax.ShapeDtypeStruct((T, D), out_dtype)
    else:
        out_specs = [tile, tile]
        out_shape = [jax.ShapeDtypeStruct((T, D), F32), jax.ShapeDtypeStruct((T, D), BF16)]
    out = pl.pallas_call(
        functools.partial(_combine_kernel, final),
        grid=(nt,),
        in_specs=[pl.BlockSpec((1, 2, ts), lambda i: (i, 0, 0), memory_space=pltpu.SMEM),
                  pl.BlockSpec((1, 2, ts), lambda i: (jnp.minimum(i + 1, nt - 1), 0, 0), memory_space=pltpu.SMEM),
                  tile, pl.BlockSpec((ts, SUBLANES), lambda i: (i, 0)), modspec,
                  pl.BlockSpec((1, D), lambda i: (0, 0)), modspec,
                  pl.BlockSpec(memory_space=pl.ANY)],
        out_specs=out_specs,
        out_shape=out_shape,
        scratch_shapes=[pltpu.VMEM((2, 2, ts, D), F32), pltpu.SemaphoreType.DMA((2,))],
        compiler_params=_params(1),
        name="combine_final" if final else "combine_prenorm",
    )(dest3, dest3, x1.reshape(T, D), rw, mod_l, norm_w.reshape(1, D), mod_next, ys)
    if final:
        return out.reshape(B, S, D)
    return out[0].reshape(B, S, D), out[1].reshape(B, S, D)


def _seq_tile(S, want):
    ts = min(want, S)
    assert S % ts == 0 and ts % CHUNK == 0
    return ts


def kernel(x, c, positions, w_ada, b_ada, norm_mix_w, w_in, ssd_conv_w, ssd_conv_b, ssd_dt_bias, ssd_a_log,
           ssd_d, ssd_norm_w, hg_lb, hg_norm_w, w_ret_o, w_hg_o, w_ssd_o, w_out, norm_ffn_w, moe_w_group,
           moe_b_group, moe_w_expert, moe_b_expert, moe_w1, moe_w3, moe_w2, final_norm_w):
    B, S, D = x.shape
    L = w_ada.shape[0]
    T = B * S
    ts_el = _seq_tile(S, 512)
    ts_mix = _seq_tile(S, 256)
    nb_mix = 2 if B % 2 == 0 else 1

    mod = _adaln(c, w_ada, b_ada)
    cos, sin = _rope_tables(positions, ts_el)
    lb_soft = jax.nn.softmax(hg_lb.astype(F32), axis=0)
    lb_all = jnp.cumsum(lb_soft, axis=0) - lb_soft[0]

    qk = RET_HEADS * RET_DK
    vw = RET_HEADS * RET_DV
    hgw = HG_HEADS * HG_DK
    inner = SSD_HEADS * SSD_P
    cdim = inner + 2 * SSD_GROUPS * SSD_N
    sizes = (qk, qk, vw, vw, hgw, hgw, hgw, hgw, inner, cdim, SSD_HEADS, 3 * D)
    offs = [0]
    for n in sizes:
        offs.append(offs[-1] + n)

    xcur = x.astype(F32)
    h = _prenorm(xcur, norm_mix_w[0].astype(F32), mod[0], ts_el)
    for l in range(L):
        wl = w_in[l].astype(BF16)
        cols = [wl[:, offs[i]:offs[i + 1]] for i in range(len(sizes))]
        y_ret = _retention(h, cos, sin, cols[0], cols[1], cols[2], cols[3], w_ret_o[l].astype(BF16), ts_el,
                           nb_mix)
        y_hg = _hgrn2(h, cols[4], cols[5], cols[6], cols[7], w_hg_o[l].astype(BF16), lb_all[l],
                      hg_norm_w[l].astype(F32), ts_mix, nb_mix)
        y_ssd = _ssd(h, cols[8], cols[9], cols[10], w_ssd_o[l].astype(BF16), ssd_conv_w[l], ssd_conv_b[l],
                     ssd_dt_bias[l], ssd_a_log[l], ssd_d[l], ssd_norm_w[l], ts_mix, 1)
        wr = jnp.zeros((SUBLANES + MOE_EXPERTS, D), F32)
        wr = wr.at[0:MOE_GROUPS].set(moe_w_group[l].astype(F32).T).at[SUBLANES:].set(moe_w_expert[l].astype(F32).T)
        br = jnp.zeros((SUBLANES + MOE_EXPERTS, 1), F32)
        br = br.at[0:MOE_GROUPS, 0].set(moe_b_group[l].astype(F32)).at[SUBLANES:, 0].set(moe_b_expert[l].astype(F32))
        wr_hi = wr.astype(BF16)
        wr_lo = (wr - wr_hi.astype(F32)).astype(BF16)
        x1, h2, route_w, route_i, counts = _merge(xcur, h, y_ret, y_hg, y_ssd, cols[11], w_out[l].astype(BF16),
                                                  mod[l], norm_ffn_w[l].astype(F32), wr_hi, wr_lo, br, ts_el)
        dest, block_e, pad_end, pad_len, nactive, nrows = _dispatch_plan(route_i, counts, T)
        xs = _dispatch(h2.reshape(T, D), dest, pad_end, pad_len, nactive, nrows)
        ys = _experts(xs, block_e, nactive, moe_w1, moe_w3, moe_w2, l)
        rw = route_w.T
        if l + 1 < L:
            xcur, h = _combine(x1, ys, dest, rw, mod[l], norm_mix_w[l + 1].astype(F32), mod[l + 1], False, x.dtype)
        else:
            out = _combine(x1, ys, dest, rw, mod[l], final_norm_w.astype(F32), mod[l], True, x.dtype)
    return out
```

```python
import functools

import jax
import jax.numpy as jnp
from jax import lax
from jax.experimental import pallas as pl
from jax.experimental.pallas import tpu as pltpu

F32 = jnp.float32
BF16 = jnp.bfloat16

EPS = 1e-6
ROPE_BASE = 10000.0
RET_HEADS = 4
RET_DK = 128
RET_DV = 256
HG_HEADS = 8
HG_DK = 128
SSD_HEADS = 32
SSD_P = 64
SSD_GROUPS = 4
SSD_N = 128
SSD_HPG = SSD_HEADS // SSD_GROUPS
SSD_CONV = 4
MOE_GROUPS = 4
MOE_PER_GROUP = 8
MOE_EXPERTS = 32
MOE_BLOCK = 512
CHUNK = 128
LANES = 128
SUBLANES = 8
VMEM_LIMIT = 56 * 1024 * 1024
HG_LEVELS = (64, 32, 16, 8, 4, 2, 1)
LOG2E = 1.4426950408889634
NEG_BIG = -1e30


def _params(n_axes, vmem=VMEM_LIMIT):
    return pltpu.CompilerParams(dimension_semantics=("arbitrary",) * n_axes,
                                vmem_limit_bytes=vmem)


def _const_spec(shape):
    nd = len(shape)
    return pl.BlockSpec(shape, lambda *_: (0,) * nd, pipeline_mode=pl.Buffered(1))


def _sigmoid(x):
    return 0.5 * jnp.tanh(0.5 * x) + 0.5


def _silu(x):
    h = 0.5 * x
    return h * jnp.tanh(h) + h


def _dot(a, b):
    return jnp.dot(a, b, preferred_element_type=F32)


def _dot_nt(a, b):
    return lax.dot_general(a, b, (((1,), (1,)), ((), ())), preferred_element_type=F32)


def _dot_tn(a, b):
    return lax.dot_general(a, b, (((0,), (0,)), ((), ())), preferred_element_type=F32)


def _dot2(m_bf16, x_f32):
    hi = x_f32.astype(BF16)
    lo = (x_f32 - hi.astype(F32)).astype(BF16)
    return _dot(m_bf16, hi) + _dot(m_bf16, lo)


def _ada_kernel(c_ref, w_ref, b_ref, o_ref):
    c = c_ref[...]
    o_ref[...] = _dot(_silu(c).astype(BF16), w_ref[...].astype(BF16)) + b_ref[...]


def _adaln(c, w_ada, b_ada):
    L, D, N = w_ada.shape
    B = c.shape[0]
    cp = jnp.zeros((SUBLANES, D), F32).at[:B].set(c.astype(F32))
    tn = 1536
    out = pl.pallas_call(
        _ada_kernel,
        grid=(L, N // tn),
        in_specs=[pl.BlockSpec((SUBLANES, D), lambda l, n: (0, 0)),
                  pl.BlockSpec((None, D, tn), lambda l, n: (l, 0, n)),
                  pl.BlockSpec((None, 1, tn), lambda l, n: (l, 0, n))],
        out_specs=pl.BlockSpec((None, SUBLANES, tn), lambda l, n: (l, 0, n)),
        out_shape=jax.ShapeDtypeStruct((L, SUBLANES, N), F32),
        compiler_params=_params(2),
        name="adaln",
    )(cp, w_ada, b_ada.reshape(L, 1, N))
    return out[:, :B].reshape(L, B, 6, D)


def _rope_kernel(pos_ref, invf_ref, sign_ref, cos_ref, sin_ref):
    ts = pos_ref.shape[0]
    hs = ts // 2
    lo = lax.broadcasted_iota(jnp.int32, (1, RET_DK), 1) < RET_DK // 2
    pos = jnp.where(lo, pos_ref[0:hs, :], pos_ref[hs:ts, :]).astype(F32)
    ang = pos * invf_ref[...]
    for fn, out_ref, scale in ((jnp.cos, cos_ref, None), (jnp.sin, sin_ref, sign_ref)):
        v = fn(ang)
        w = pltpu.roll(v, RET_DK // 2, 1)
        first = jnp.where(lo, v, w)
        second = jnp.where(lo, w, v)
        if scale is not None:
            first = first * scale[...]
            second = second * scale[...]
        out_ref[0:hs, :] = first
        out_ref[hs:ts, :] = second


def _rope_tables(positions, ts):
    B, S = positions.shape
    half = RET_DK // 2
    inv_freq = ROPE_BASE ** (-jnp.linspace(0.0, 1.0, half, dtype=F32))
    invf = jnp.concatenate([inv_freq, inv_freq]).reshape(1, RET_DK)
    sign = jnp.concatenate([-jnp.ones((half,), F32), jnp.ones((half,), F32)]).reshape(1, RET_DK)
    spec = pl.BlockSpec((None, ts, RET_DK), lambda b, s: (b, s, 0))
    return pl.pallas_call(
        _rope_kernel,
        grid=(B, S // ts),
        in_specs=[pl.BlockSpec((None, ts, 1), lambda b, s: (b, s, 0)),
                  pl.BlockSpec((1, RET_DK), lambda b, s: (0, 0)),
                  pl.BlockSpec((1, RET_DK), lambda b, s: (0, 0))],
        out_specs=[spec, spec],
        out_shape=[jax.ShapeDtypeStruct((B, S, RET_DK), F32)] * 2,
        compiler_params=_params(2),
        name="rope_tables",
    )(positions.reshape(B, S, 1), invf, sign)


def _rms_mod(x, w, scale, shift):
    y = x * lax.rsqrt(jnp.mean(x * x, axis=-1, keepdims=True) + EPS)
    return (y * w) * (1.0 + scale) + shift


def _prenorm_kernel(x_ref, w_ref, mod_ref, o_ref):
    o_ref[...] = _rms_mod(x_ref[...], w_ref[...], mod_ref[1:2, :], mod_ref[0:1, :]).astype(o_ref.dtype)


def _prenorm(x, w, mod_l, ts):
    B, S, D = x.shape
    return pl.pallas_call(
        _prenorm_kernel,
        grid=(B, S // ts),
        in_specs=[pl.BlockSpec((None, ts, D), lambda b, s: (b, s, 0)),
                  pl.BlockSpec((1, D), lambda b, s: (0, 0)),
                  pl.BlockSpec((None, 6, D), lambda b, s: (b, 0, 0))],
        out_specs=pl.BlockSpec((None, ts, D), lambda b, s: (b, s, 0)),
        out_shape=jax.ShapeDtypeStruct((B, S, D), BF16),
        compiler_params=_params(2),
        name="prenorm",
    )(x, w.reshape(1, D), mod_l)


def _ret_kernel(h_ref, hn_ref, cos_ref, cosn_ref, sin_ref, sinn_ref, wq_ref, wk_ref, wv_ref, wg_ref, wo_ref,
                di_ref, dq_ref, dk_ref, dc_ref, o_ref,
                st_ref, q_s, k_s, v_s, g_s):
    nb, ts, D = h_ref.shape
    C = CHUNK
    n = ts // C
    R = nb * C
    scale = RET_DK ** -0.5

    def project(hc, cos, sin, slot):
        hc = hc.reshape(R, D)
        cos = cos.reshape(R, RET_DK)
        sin = sin.reshape(R, RET_DK)
        q = _dot(hc, wq_ref[...])
        k = _dot(hc, wk_ref[...])
        for hh in range(RET_HEADS):
            sl = slice(hh * RET_DK, (hh + 1) * RET_DK)
            qh = q[:, sl]
            kh = k[:, sl]
            q_s[slot, :, sl] = (qh * cos + pltpu.roll(qh, RET_DK // 2, 1) * sin).astype(BF16)
            k_s[slot, :, sl] = (kh * cos + pltpu.roll(kh, RET_DK // 2, 1) * sin) * scale
        v_s[slot] = _dot(hc, wv_ref[...]).astype(BF16)
        g_s[slot] = _silu(_dot(hc, wg_ref[...]))

    @pl.when(pl.program_id(1) == 0)
    def _():
        st_ref[...] = jnp.zeros_like(st_ref)
        project(h_ref[:, 0:C, :], cos_ref[:, 0:C, :], sin_ref[:, 0:C, :], 0)

    def one_chunk(bi, slot):
        rows = slice(bi * C, (bi + 1) * C)
        outs = []
        for hh in range(RET_HEADS):
            ksl = slice(hh * RET_DK, (hh + 1) * RET_DK)
            vsl = slice(hh * RET_DV, (hh + 1) * RET_DV)
            qc = q_s[slot, rows, ksl]
            kc = k_s[slot, rows, ksl]
            vc = v_s[slot, rows, vsl]
            st = st_ref[bi * RET_HEADS + hh]
            scores = _dot_nt(qc, kc.astype(BF16)) * di_ref[hh]
            o = _dot(scores.astype(BF16), vc) + _dot(qc, st.astype(BF16)) * dq_ref[hh]
            st_ref[bi * RET_HEADS + hh] = st * dc_ref[hh] + _dot_tn((kc * dk_ref[hh]).astype(BF16), vc)
            mu = jnp.mean(o, axis=-1, keepdims=True)
            d = o - mu
            var = jnp.mean(d * d, axis=-1, keepdims=True)
            outs.append((g_s[slot, rows, vsl] * (d * lax.rsqrt(var + EPS))).astype(BF16))
        return jnp.concatenate(outs, axis=1)

    for j in range(n):
        if j + 1 < n:
            nsl = slice((j + 1) * C, (j + 2) * C)
            project(h_ref[:, nsl, :], cos_ref[:, nsl, :], sin_ref[:, nsl, :], (j + 1) % 2)
        else:
            project(hn_ref[...], cosn_ref[...], sinn_ref[...], (j + 1) % 2)
        o = jnp.concatenate([one_chunk(bi, j % 2) for bi in range(nb)], axis=0)
        o_ref[:, j * C:(j + 1) * C, :] = _dot(o, wo_ref[...]).astype(o_ref.dtype).reshape(nb, C, D)


def _retention(h, cos, sin, wq, wk, wv, wg, wo, ts, nb):
    B, S, D = h.shape
    H, C = RET_HEADS, CHUNK
    log_gamma = jnp.log(1.0 - 2.0 ** (-5.0 - jnp.arange(H, dtype=F32)))
    idx = jnp.arange(C, dtype=F32)
    rel = idx[:, None] - idx[None, :]
    di = jnp.where(rel >= 0.0, jnp.exp(log_gamma[:, None, None] * jnp.maximum(rel, 0.0)), 0.0)
    dq = jnp.broadcast_to(jnp.exp(log_gamma[:, None] * (idx[None, :] + 1.0))[:, :, None], (H, C, RET_DV))
    dk = jnp.broadcast_to(jnp.exp(log_gamma[:, None] * (C - 1.0 - idx[None, :]))[:, :, None], (H, C, RET_DK))
    dc = jnp.broadcast_to(jnp.exp(log_gamma * C)[:, None, None], (H, 1, RET_DV))
    qk, vw = H * RET_DK, H * RET_DV
    n = ts // C
    assert n % 2 == 0
    tile = lambda w: pl.BlockSpec((nb, ts, w), lambda b, s: (b, s, 0))
    nxt = lambda w: pl.BlockSpec((nb, C, w), lambda b, s: (b, jnp.minimum((s + 1) * n, S // C - 1), 0))
    R = nb * C
    return pl.pallas_call(
        _ret_kernel,
        grid=(B // nb, S // ts),
        in_specs=[tile(D), nxt(D), tile(RET_DK), nxt(RET_DK), tile(RET_DK), nxt(RET_DK),
                  _const_spec((D, qk)), _const_spec((D, qk)), _const_spec((D, vw)), _const_spec((D, vw)),
                  _const_spec((vw, D)),
                  _const_spec((H, C, C)), _const_spec((H, C, RET_DV)), _const_spec((H, C, RET_DK)),
                  _const_spec((H, 1, RET_DV))],
        out_specs=tile(D),
        out_shape=jax.ShapeDtypeStruct((B, S, D), BF16),
        scratch_shapes=[pltpu.VMEM((nb * H, RET_DK, RET_DV), F32),
                        pltpu.VMEM((2, R, qk), BF16), pltpu.VMEM((2, R, qk), F32),
                        pltpu.VMEM((2, R, vw), BF16), pltpu.VMEM((2, R, vw), F32)],
        compiler_params=_params(2),
        name="retention",
    )(h, h, cos, cos, sin, sin, wq, wk, wv, wg, wo, di, dq, dk, dc)


def _hg_level_matrices():
    C = CHUNK
    H = C // 2
    r = jnp.arange(C)
    stack = (r[:, None] >= r[None, :]).astype(BF16)
    t = jnp.arange(H)[:, None]
    s = jnp.arange(C)[None, :] % H
    masks = []
    for b in HG_LEVELS[1:]:
        masks.append(((t // (2 * b)) == (s // (2 * b))) & ((t & b) != 0) & ((s & b) == 0))
    masks.append(t == s)
    return stack, jnp.stack(masks).astype(F32)


def _hg_kernel(h_ref, hn_ref, wq_ref, wf_ref, wi_ref, wg_ref, wo_ref, lb_ref, nw_ref, cm_ref, mk_ref, o_ref,
               st_ref, q_s, k_s, v_s, lf_s, g_s):
    nb, ts, D = h_ref.shape
    C = CHUNK
    W = HG_HEADS * HG_DK
    n = ts // C
    lb = lb_ref[...]

    def project(hc, slot):
        q_s[slot] = _silu(_dot(hc, wq_ref[...]))
        hf = _dot(hc, wf_ref[...])
        lf_s[slot] = jnp.log(lb + (1.0 - lb) * _sigmoid(hf)) * LOG2E
        k_s[slot] = (1.0 - lb) * _sigmoid(-hf)
        v_s[slot] = _dot(hc, wi_ref[...]).astype(BF16)
        g_s[slot] = _silu(_dot(hc, wg_ref[...]))

    @pl.when(pl.program_id(1) == 0)
    def _():
        st_ref[...] = jnp.zeros_like(st_ref)
        project(h_ref[:, 0:C, :].reshape(nb * C, D), 0)

    odd_row = {b: (lax.broadcasted_iota(jnp.int32, (C, 1), 0) & b) != 0 for b in HG_LEVELS}

    def one_chunk(bi, slot):
        rows = slice(bi * C, (bi + 1) * C)
        lf = lf_s[slot, rows, :]
        q = q_s[slot, rows, :]
        k = k_s[slot, rows, :]
        G = _dot2(cm_ref[...], lf)
        H = C // 2
        acc = [None] * HG_HEADS
        cross = [None] * HG_HEADS
        left = lax.broadcasted_iota(jnp.int32, (H, C), 1) < H
        G3 = G.reshape(C // SUBLANES, SUBLANES, W)
        sub8 = lax.broadcasted_iota(jnp.int32, (C // SUBLANES, SUBLANES, 1), 1)

        def tile_row(i):
            return jnp.broadcast_to(G3[:, i:i + 1, :], G3.shape)
        for li, b in enumerate(HG_LEVELS):
            if b >= SUBLANES:
                pieces = []
                for j in range(C // b):
                    ref = G[(j // 2) * 2 * b + b - 1:(j // 2) * 2 * b + b, :]
                    blk = slice(j * b, (j + 1) * b)
                    if j % 2 == 1:
                        pieces.append(q[blk] * jnp.exp2(G[blk] - ref))
                    else:
                        pieces.append(k[blk] * jnp.exp2(ref - G[blk]))
                z = jnp.concatenate(pieces, axis=0)
            elif b > 1:
                ref = tile_row(3) if b == 4 else jnp.where(sub8 < 4, tile_row(1), tile_row(5))
                e = -jnp.abs(G3 - ref)
                z = jnp.where(odd_row[b], q, k) * jnp.exp2(e.reshape(C, W))
            else:
                z = jnp.where(odd_row[b], q * jnp.exp2(lf), k)
            z = z.astype(BF16)
            for hh in range(HG_HEADS):
                zh = z[:, hh * HG_DK:(hh + 1) * HG_DK]
                if li == 0:
                    cross[hh] = _dot_nt(zh[H:], zh)
                else:
                    zt, zb = zh[:H], zh[H:]
                    zero = jnp.zeros_like(zt)
                    rhs = jnp.concatenate([jnp.concatenate([zt, zero], axis=1),
                                           jnp.concatenate([zero, zb], axis=1)], axis=0)
                    a = _dot_nt(jnp.concatenate([zt, zb], axis=1), rhs) * mk_ref[li - 1]
                    acc[hh] = a if li == 1 else acc[hh] + a
        g_last = G[C - 1:C, :]
        qg = (q * jnp.exp2(G)).astype(BF16)
        kh = (k * jnp.exp2(g_last - G)).astype(BF16)
        qk = q * k
        dec = jnp.exp2(g_last)
        eye = mk_ref[len(HG_LEVELS) - 1]
        outs = []
        for hh in range(HG_HEADS):
            sl = slice(hh * HG_DK, (hh + 1) * HG_DK)
            vh = v_s[slot, rows, sl]
            diag = jnp.sum(qk[:, sl], axis=-1, keepdims=True)
            dblk = acc[hh] + jnp.where(left, diag[:H], diag[H:]) * eye
            a_top = jnp.where(left, dblk, 0.0).astype(BF16)
            a_bot = jnp.where(left, cross[hh], dblk).astype(BF16)
            st = st_ref[bi * HG_HEADS + hh]
            o = (jnp.concatenate([_dot(a_top, vh), _dot(a_bot, vh)], axis=0)
                 + _dot_nt(qg[:, sl], st.astype(BF16)))
            st_ref[bi * HG_HEADS + hh] = st * dec[:, sl] + _dot_tn(vh, kh[:, sl])
            on = o * lax.rsqrt(jnp.mean(o * o, axis=-1, keepdims=True) + EPS) * nw_ref[...]
            outs.append((on * g_s[slot, rows, sl]).astype(BF16))
        return jnp.concatenate(outs, axis=1)

    for j in range(n):
        nxt = h_ref[:, (j + 1) * C:(j + 2) * C, :] if j + 1 < n else hn_ref[...]
        project(nxt.reshape(nb * C, D), (j + 1) % 2)
        o = jnp.concatenate([one_chunk(bi, j % 2) for bi in range(nb)], axis=0)
        o_ref[:, j * C:(j + 1) * C, :] = _dot(o, wo_ref[...]).astype(o_ref.dtype).reshape(nb, C, D)


def _hgrn2(h, wq, wf, wi, wg, wo, lb, norm_w, ts, nb):
    B, S, D = h.shape
    W = HG_HEADS * HG_DK
    C = CHUNK
    n = ts // C
    assert n % 2 == 0
    R = nb * C
    cm, mk = _hg_level_matrices()
    tile = lambda w: pl.BlockSpec((nb, ts, w), lambda b, s: (b, s, 0))
    nxt = pl.BlockSpec((nb, C, D), lambda b, s: (b, jnp.minimum((s + 1) * n, S // C - 1), 0))
    return pl.pallas_call(
        _hg_kernel,
        grid=(B // nb, S // ts),
        in_specs=[tile(D), nxt] + [_const_spec((D, W))] * 4 + [_const_spec((W, D)), _const_spec((1, W)),
                  _const_spec((1, HG_DK)), _const_spec(cm.shape), _const_spec(mk.shape)],
        out_specs=tile(D),
        out_shape=jax.ShapeDtypeStruct((B, S, D), BF16),
        scratch_shapes=[pltpu.VMEM((nb * HG_HEADS, HG_DK, HG_DK), F32),
                        pltpu.VMEM((2, R, W), F32), pltpu.VMEM((2, R, W), F32), pltpu.VMEM((2, R, W), BF16),
                        pltpu.VMEM((2, R, W), F32), pltpu.VMEM((2, R, W), F32)],
        compiler_params=_params(2),
        name="hgrn2",
    )(h, h, wq, wf, wi, wg, wo, lb.reshape(1, W), norm_w.reshape(1, HG_DK), cm, mk)


def _shift_rows(x, k, head):
    r, w = x.shape
    t = x.reshape(r // SUBLANES, SUBLANES, w)
    rot = pltpu.roll(t, k, 1)
    hrot = pltpu.roll(head.reshape(1, SUBLANES, w), k, 1)
    prev = jnp.concatenate([hrot, rot[:-1]], axis=0)
    sub = lax.broadcasted_iota(jnp.int32, (1, SUBLANES, 1), 1)
    return jnp.where(sub < k, prev, rot).reshape(r, w)


def _softplus(x):
    return jnp.maximum(x, 0.0) + jnp.log(1.0 + jnp.exp(-jnp.abs(x)))


def _ssd_kernel(h_ref, hn_ref, wz_ref, wx_ref, wdt_ref, wo_ref, cw_ref, cb_ref, dtb_ref, a_ref, dl_ref, nw_ref,
                tril_ref, negm_ref, ex_ref, o_ref,
                st_ref, halo_s, u_s, xc_s, z_s, dt_s):
    nb, ts, D = h_ref.shape
    C = CHUNK
    n = ts // C
    inner = SSD_HEADS * SSD_P
    gw = SSD_GROUPS * SSD_N
    gp = SSD_HPG * SSD_P
    halo = SUBLANES

    def project(hc, slot):
        z_s[slot] = _silu(_dot(hc, wz_ref[...]))
        dt_s[slot] = _softplus(_dot(hc, wdt_ref[...]) + dtb_ref[...])
        u_s[...] = _dot(hc, wx_ref[...])
        for bi in range(nb):
            x = u_s[bi * C:(bi + 1) * C, :]
            hl = halo_s[bi]
            x1 = _shift_rows(x, 1, hl)
            pair = cw_ref[1:2, :] * x + cw_ref[0:1, :] * x1
            pair_hl = cw_ref[1:2, :] * hl + cw_ref[0:1, :] * pltpu.roll(hl, 1, 0)
            y = (cw_ref[3:4, :] * x + cw_ref[2:3, :] * x1) + _shift_rows(pair, 2, pair_hl)
            xc_s[slot, bi * C:(bi + 1) * C, :] = _silu(y + cb_ref[...])
            halo_s[bi] = u_s[(bi + 1) * C - halo:(bi + 1) * C, :]

    @pl.when(pl.program_id(1) == 0)
    def _():
        st_ref[...] = jnp.zeros_like(st_ref)
        halo_s[...] = jnp.zeros_like(halo_s)
        project(h_ref[:, 0:C, :].reshape(nb * C, D), 0)

    low_half = lax.broadcasted_iota(jnp.int32, (C, LANES), 1) < SSD_P

    def one_chunk(bi, slot):
        rows = slice(bi * C, (bi + 1) * C)
        dt = dt_s[slot, rows, :]
        da = dt * a_ref[...]
        cum = jnp.dot(tril_ref[...], da, preferred_element_type=F32, precision=lax.Precision.HIGHEST) * LOG2E
        cum_t = cum.T
        row_t = (cum - jnp.log(dt) * LOG2E).T
        cum_last = cum[C - 1:C, :]
        wfac = dt * jnp.exp2(cum_last - cum)
        w_hi = wfac.astype(BF16)
        w_lo = (wfac - w_hi.astype(F32)).astype(BF16)
        wexp = _dot(w_hi, ex_ref[...]) + _dot(w_lo, ex_ref[...])
        dl = jnp.broadcast_to(jnp.exp2(cum_last), (SUBLANES, LANES))
        d_hi = dl.astype(BF16)
        d_lo = (dl - d_hi.astype(F32)).astype(BF16)
        dexp = (_dot(d_hi, ex_ref[...]) + _dot(d_lo, ex_ref[...]))[0:1, :]
        ypairs = []
        for g in range(SSD_GROUPS):
            bg = xc_s[slot, rows, inner + g * SSD_N:inner + (g + 1) * SSD_N]
            cg = xc_s[slot, rows, inner + gw + g * SSD_N:inner + gw + (g + 1) * SSD_N]
            bgb = bg.astype(BF16)
            cb = _dot_nt(cg.astype(BF16), bgb)
            st = st_ref[bi * SSD_GROUPS + g]
            stb = st.astype(BF16)
            xg = xc_s[slot, rows, g * gp:(g + 1) * gp]
            xgb = xg.astype(BF16)
            for jp in range(SSD_HPG // 2):
                lsl = slice(jp * LANES, (jp + 1) * LANES)
                rhs = jnp.concatenate([xgb[:, lsl], stb[:, lsl]], axis=0)
                ys = []
                for hd in (g * SSD_HPG + 2 * jp, g * SSD_HPG + 2 * jp + 1):
                    colb = jnp.broadcast_to(cum[:, hd:hd + 1], (C, C))
                    w = cb * jnp.exp2(colb - row_t[hd:hd + 1, :] + negm_ref[...])
                    ce = cg * jnp.exp2(colb)
                    ys.append(_dot(jnp.concatenate([w.astype(BF16), ce.astype(BF16)], axis=1), rhs))
                ypairs.append(jnp.where(low_half, ys[0], ys[1]))
            gsl = slice(g * gp, (g + 1) * gp)
            st_ref[bi * SSD_GROUPS + g] = st * dexp[:, gsl] + _dot_tn(bgb, (xg * wexp[:, gsl]).astype(BF16))
        gi = inner // SSD_GROUPS
        ppg = gi // LANES
        outs = []
        for g in range(SSD_GROUPS):
            sl = slice(g * gi, (g + 1) * gi)
            y = jnp.concatenate(ypairs[g * ppg:(g + 1) * ppg], axis=1)
            seg = (y + dl_ref[:, sl] * xc_s[slot, rows, sl]) * z_s[slot, rows, sl]
            nrm = seg * lax.rsqrt(jnp.mean(seg * seg, axis=-1, keepdims=True) + EPS)
            outs.append((nrm * nw_ref[:, sl]).astype(BF16))
        return jnp.concatenate(outs, axis=1)

    for j in range(n):
        nxt = h_ref[:, (j + 1) * C:(j + 2) * C, :] if j + 1 < n else hn_ref[...]
        project(nxt.reshape(nb * C, D), (j + 1) % 2)
        o = jnp.concatenate([one_chunk(bi, j % 2) for bi in range(nb)], axis=0)
        o_ref[:, j * C:(j + 1) * C, :] = _dot(o, wo_ref[...]).astype(o_ref.dtype).reshape(nb, C, D)


def _ssd(h, wz, wx, wdt, wo, conv_w, conv_b, dt_bias, a_log, d_skip, norm_w, ts, nb):
    B, S, D = h.shape
    C = CHUNK
    n = ts // C
    assert n % 2 == 0
    inner = SSD_HEADS * SSD_P
    cdim = inner + 2 * SSD_GROUPS * SSD_N
    pad = LANES - SSD_HEADS
    wdt_p = jnp.pad(wdt, ((0, 0), (0, pad)))
    dtb = jnp.pad(dt_bias.astype(F32), (0, pad)).reshape(1, LANES)
    a = jnp.pad(-jnp.exp(a_log.astype(F32)), (0, pad)).reshape(1, LANES)
    dl = jnp.repeat(d_skip.astype(F32), SSD_P).reshape(1, inner)
    r = jnp.arange(C)
    tril = (r[:, None] >= r[None, :]).astype(F32)
    negm = jnp.where(r[:, None] >= r[None, :], 0.0, NEG_BIG).astype(F32)
    ex = (jnp.arange(LANES)[:, None] == (jnp.arange(inner)[None, :] // SSD_P)).astype(BF16)
    tile = lambda w: pl.BlockSpec((nb, ts, w), lambda b, s: (b, s, 0))
    nxt = pl.BlockSpec((nb, C, D), lambda b, s: (b, jnp.minimum((s + 1) * n, S // C - 1), 0))
    R = nb * C
    return pl.pallas_call(
        _ssd_kernel,
        grid=(B // nb, S // ts),
        in_specs=[tile(D), nxt, _const_spec((D, inner)), _const_spec((D, cdim)), _const_spec((D, LANES)),
                  _const_spec((inner, D)), _const_spec((SSD_CONV, cdim)), _const_spec((1, cdim)),
                  _const_spec((1, LANES)), _const_spec((1, LANES)), _const_spec((1, inner)),
                  _const_spec((1, inner)), _const_spec((C, C)), _const_spec((C, C)),
                  _const_spec((LANES, inner))],
        out_specs=tile(D),
        out_shape=jax.ShapeDtypeStruct((B, S, D), BF16),
        scratch_shapes=[pltpu.VMEM((nb * SSD_GROUPS, SSD_N, SSD_HPG * SSD_P), F32),
                        pltpu.VMEM((nb, SUBLANES, cdim), F32), pltpu.VMEM((R, cdim), F32),
                        pltpu.VMEM((2, R, cdim), F32),
                        pltpu.VMEM((2, R, inner), F32), pltpu.VMEM((2, R, LANES), F32)],
        compiler_params=_params(2),
        name="ssd",
    )(h, h, wz, wx, wdt_p, wo, conv_w.astype(F32), conv_b.astype(F32).reshape(1, cdim), dtb, a, dl,
      norm_w.astype(F32).reshape(1, inner), tril, negm, ex)


def _merge_kernel(x_ref, h_ref, yr_ref, yh_ref, ys_ref, wgl_ref, wout_ref, mod_ref, nw_ref,
                  wr_hi_ref, wr_lo_ref, br_ref, ut_ref, x1_ref, h2_ref, rf_ref, ri_ref, cnt_ref, run_ref):
    D = x_ref.shape[1]
    ts = x_ref.shape[0]

    @pl.when((pl.program_id(0) == 0) & (pl.program_id(1) == 0))
    def _():
        run_ref[...] = jnp.zeros_like(run_ref)

    h = h_ref[...]
    gates = _sigmoid(_dot(h, wgl_ref[...]))
    m = (gates[:, 0:D] * yr_ref[...].astype(F32) + gates[:, D:2 * D] * yh_ref[...].astype(F32)
         + gates[:, 2 * D:3 * D] * ys_ref[...].astype(F32))
    y = _dot(m.astype(BF16), wout_ref[...])
    x1 = x_ref[...] + mod_ref[2:3, :] * y
    x1_ref[...] = x1
    h2 = _rms_mod(x1, nw_ref[...], mod_ref[4:5, :], mod_ref[3:4, :])
    h2_ref[...] = h2
    h_hi = h2.astype(BF16)
    h_lo = (h2 - h_hi.astype(F32)).astype(BF16)
    lg = (_dot_nt(wr_hi_ref[...], h_hi) + _dot_nt(wr_hi_ref[...], h_lo) + _dot_nt(wr_lo_ref[...], h_hi)
          + br_ref[...])
    gl = lg[0:MOE_GROUPS]
    gmax = jnp.max(gl, axis=0, keepdims=True)
    gi = lax.broadcasted_iota(jnp.int32, gl.shape, 0)
    gtop = jnp.min(jnp.where(gl == gmax, gi, MOE_GROUPS), axis=0, keepdims=True)
    p_g = 1.0 / jnp.sum(jnp.exp(gl - gmax), axis=0, keepdims=True)
    el = jnp.zeros((MOE_PER_GROUP, ts), F32)
    for g in range(MOE_GROUPS):
        lo = SUBLANES + g * MOE_PER_GROUP
        el = jnp.where(gtop == g, lg[lo:lo + MOE_PER_GROUP], el)
    ei = lax.broadcasted_iota(jnp.int32, el.shape, 0)
    m1 = jnp.max(el, axis=0, keepdims=True)
    i1 = jnp.min(jnp.where(el == m1, ei, MOE_PER_GROUP), axis=0, keepdims=True)
    el2 = jnp.where(ei == i1, -jnp.inf, el)
    m2 = jnp.max(el2, axis=0, keepdims=True)
    i2 = jnp.min(jnp.where(el2 == m2, ei, MOE_PER_GROUP), axis=0, keepdims=True)
    e21 = jnp.exp(m2 - m1)
    den = 1.0 + e21
    w1 = (1.0 / den) * p_g
    w2 = (e21 / den) * p_g
    zf = jnp.zeros((SUBLANES - 2, ts), F32)
    rf_ref[...] = jnp.concatenate([w1, w2, zf], axis=0)
    e1 = gtop * MOE_PER_GROUP + i1
    e2 = gtop * MOE_PER_GROUP + i2
    ex = lax.broadcasted_iota(jnp.int32, (MOE_EXPERTS, ts), 0)
    oh1 = jnp.where(ex == e1, 1.0, 0.0)
    oh2 = jnp.where(ex == e2, 1.0, 0.0)
    p1 = _dot(oh1.astype(BF16), ut_ref[...])
    p2 = _dot(oh2.astype(BF16), ut_ref[...])
    c1 = p1[:, ts - 1:ts]
    c2 = p2[:, ts - 1:ts]
    run = run_ref[...]
    base = run[:, 0:1]
    r1 = jnp.sum(oh1 * (p1 - 1.0 + base), axis=0, keepdims=True)
    r2 = jnp.sum(oh2 * (p2 - 1.0 + base + c1), axis=0, keepdims=True)
    run = run + (c1 + c2)
    run_ref[...] = run
    cnt_ref[...] = run
    ri_ref[...] = jnp.concatenate([e1, e2, r1.astype(jnp.int32), r2.astype(jnp.int32),
                                   jnp.zeros((SUBLANES - 4, ts), jnp.int32)], axis=0)


def _merge(x, h, y_ret, y_hg, y_ssd, wgl, wout, mod_l, norm_w, wr_hi, wr_lo, br, ts):
    B, S, D = x.shape
    ns = S // ts
    R = wr_hi.shape[0]
    tile = pl.BlockSpec((None, ts, D), lambda b, s: (b, s, 0))
    rt = pl.BlockSpec((SUBLANES, ts), lambda b, s: (0, b * ns + s))
    r = jnp.arange(ts)
    ut = (r[:, None] <= r[None, :]).astype(BF16)
    return pl.pallas_call(
        _merge_kernel,
        grid=(B, ns),
        in_specs=[tile, tile, tile, tile, tile, _const_spec((D, 3 * D)), _const_spec((D, D)),
                  pl.BlockSpec((None, 6, D), lambda b, s: (b, 0, 0)), _const_spec((1, D)),
                  _const_spec((R, D)), _const_spec((R, D)), _const_spec((R, 1)), _const_spec((ts, ts))],
        out_specs=[tile, tile, rt, rt, pl.BlockSpec((MOE_EXPERTS, LANES), lambda b, s: (0, 0))],
        out_shape=[jax.ShapeDtypeStruct((B, S, D), F32), jax.ShapeDtypeStruct((B, S, D), F32),
                   jax.ShapeDtypeStruct((SUBLANES, B * S), F32), jax.ShapeDtypeStruct((SUBLANES, B * S), jnp.int32),
                   jax.ShapeDtypeStruct((MOE_EXPERTS, LANES), F32)],
        scratch_shapes=[pltpu.VMEM((MOE_EXPERTS, LANES), F32)],
        compiler_params=_params(2),
        name="merge_router",
    )(x, h, y_ret, y_hg, y_ssd, wgl, wout, mod_l, norm_w.reshape(1, D), wr_hi, wr_lo, br, ut)


def _row_copy(src, src_row, dst, dst_row, sem):
    return pltpu.make_async_copy(src.at[pl.ds(src_row, 1), :], dst.at[pl.ds(dst_row, 1), :], sem)


DISPATCH_TILE = 1024
DMA_UNROLL = 256


def _dispatch_kernel(pend_ref, plen_ref, nact_ref, dest_ref, h2_ref, xs_hbm, zbuf, stage_s, sem, zsem):
    i = pl.program_id(0)
    td = dest_ref.shape[2]
    BLK = MOE_BLOCK
    nb = xs_hbm.shape[0] // BLK

    def zero_fill(wait):
        def go(start, size):
            cp = pltpu.make_async_copy(zbuf.at[pl.ds(0, size), :], xs_hbm.at[pl.ds(start, size), :], zsem)
            if wait:
                cp.wait()
            else:
                cp.start()

        for e in range(MOE_EXPERTS):
            ln = plen_ref[e]
            end = pend_ref[e]
            size = BLK // 2
            while size >= SUBLANES:
                @pl.when((ln & size) != 0)
                def _(size=size, ln=ln, end=end):
                    go(pl.multiple_of(end - (ln & ~(size - 1)), size), size)
                size //= 2
            for j in range(SUBLANES - 1):
                @pl.when(j < (ln & (SUBLANES - 1)))
                def _(j=j, ln=ln, end=end):
                    go(end - ln + j, 1)
        for j in range(MOE_EXPERTS):
            @pl.when(nact_ref[0] + j < nb)
            def _(j=j):
                go(pl.multiple_of((nact_ref[0] + j) * BLK, BLK), BLK)

    @pl.when(i == 0)
    def _():
        zbuf[...] = jnp.zeros_like(zbuf)
        zero_fill(False)

    slot = i % 2
    stage = stage_s.at[slot]
    stage[...] = h2_ref[...]

    def issue(j, c):
        base = pl.multiple_of(j * DMA_UNROLL, DMA_UNROLL)
        src = stage.at[pl.ds(base, DMA_UNROLL), :]
        for u in range(DMA_UNROLL):
            for k in range(2):
                _row_copy(src, u, xs_hbm, dest_ref[0, k, base + u], sem.at[slot]).start(priority=k)
        return c

    lax.fori_loop(0, td // DMA_UNROLL, issue, 0)

    def wait_tile(sl):
        for k in range(2):
            pltpu.make_async_copy(stage_s.at[sl], xs_hbm.at[pl.ds(0, td), :], sem.at[sl]).wait()

    @pl.when(i > 0)
    def _():
        wait_tile(1 - slot)

    @pl.when(i == pl.num_programs(0) - 1)
    def _():
        wait_tile(slot)

    @pl.when(i == 0)
    def _():
        zero_fill(True)


def _dispatch(h2, dest, pad_end, pad_len, nactive, nrows):
    T, D = h2.shape
    td = min(DISPATCH_TILE, T)
    nt = T // td
    dest3 = dest.reshape(2, nt, td).transpose(1, 0, 2)
    grid_spec = pltpu.PrefetchScalarGridSpec(
        num_scalar_prefetch=3,
        grid=(nt,),
        in_specs=[pl.BlockSpec((1, 2, td), lambda i, *_: (i, 0, 0), memory_space=pltpu.SMEM),
                  pl.BlockSpec((td, D), lambda i, *_: (i, 0))],
        out_specs=pl.BlockSpec(memory_space=pl.ANY),
        scratch_shapes=[pltpu.VMEM((MOE_BLOCK, D), F32), pltpu.VMEM((2, td, D), F32),
                        pltpu.SemaphoreType.DMA((2,)), pltpu.SemaphoreType.DMA(())])
    return pl.pallas_call(
        _dispatch_kernel,
        grid_spec=grid_spec,
        out_shape=jax.ShapeDtypeStruct((nrows, D), F32),
        compiler_params=_params(1),
        name="dispatch",
    )(pad_end, pad_len, nactive, dest3, h2)


def _expert_kernel(be_ref, na_ref, x_ref, w1_ref, w3_ref, w2_ref, y_ref, w1_s, w3_s, w2_s):
    i = pl.program_id(0)

    @pl.when((i == 0) | (be_ref[i] != be_ref[jnp.maximum(i - 1, 0)]))
    def _():
        w1_s[...] = w1_ref[...].astype(BF16)
        w3_s[...] = w3_ref[...].astype(BF16)
        w2_s[...] = w2_ref[...].astype(BF16)

    @pl.when(i < na_ref[0])
    def _():
        x = x_ref[...].astype(BF16)
        a = _dot(x, w1_s[...])
        b = _dot(x, w3_s[...])
        y_ref[...] = _dot((_silu(a) * b).astype(BF16), w2_s[...])

    @pl.when(i >= na_ref[0])
    def _():
        y_ref[...] = jnp.zeros_like(y_ref)


def _experts(xs, block_e, nactive, w1, w3, w2, layer):
    nrows, D = xs.shape
    BLK = MOE_BLOCK
    nb = nrows // BLK
    ff = w1.shape[3]
    wsel = lambda i, be, na: (layer, be[i], 0, 0)
    grid_spec = pltpu.PrefetchScalarGridSpec(
        num_scalar_prefetch=2,
        grid=(nb,),
        in_specs=[pl.BlockSpec((BLK, D), lambda i, be, na: (jnp.minimum(i, na[0] - 1), 0)),
                  pl.BlockSpec((None, None, D, ff), wsel), pl.BlockSpec((None, None, D, ff), wsel),
                  pl.BlockSpec((None, None, ff, D), wsel)],
        out_specs=pl.BlockSpec((BLK, D), lambda i, be, na: (i, 0)),
        scratch_shapes=[pltpu.VMEM((D, ff), BF16), pltpu.VMEM((D, ff), BF16), pltpu.VMEM((ff, D), BF16)])
    return pl.pallas_call(
        _expert_kernel,
        grid_spec=grid_spec,
        out_shape=jax.ShapeDtypeStruct((nrows, D), F32),
        compiler_params=_params(1),
        name="experts",
    )(block_e, nactive, xs, w1, w3, w2)


def _dispatch_plan(route_i, counts_f, T):
    BLK = MOE_BLOCK
    nb = -(-(2 * T) // BLK) + MOE_EXPERTS
    counts = counts_f[:, 0].astype(jnp.int32)
    padded = ((counts + BLK - 1) // BLK) * BLK
    pad_ends = jnp.cumsum(padded)
    pad_starts = pad_ends - padded
    e = route_i[0:2]
    rank = route_i[2:4]
    sel = e[:, :, None] == jnp.arange(MOE_EXPERTS, dtype=jnp.int32)[None, None, :]
    dest = rank + jnp.sum(jnp.where(sel, pad_starts[None, None, :], 0), axis=-1)
    nactive = (pad_ends[-1] // BLK).astype(jnp.int32)
    blk_id = jnp.minimum(jnp.arange(nb, dtype=jnp.int32), nactive - 1)
    block_e = jnp.sum((blk_id[:, None] * BLK >= pad_ends[None, :]).astype(jnp.int32), axis=1)
    block_e = jnp.minimum(block_e, MOE_EXPERTS - 1)
    return dest, block_e, pad_ends, padded - counts, nactive.reshape(1), nb * BLK


COMBINE_TILE = 256


def _combine_kernel(final, dcur_ref, dnxt_ref, x_ref, rw_ref, mod_ref, nw_ref, modn_ref, ys_hbm, *rest):
    if final:
        o_ref, gbuf, sem = rest
    else:
        x2_ref, h_ref, gbuf, sem = rest
    i = pl.program_id(0)
    n = pl.num_programs(0)
    ts = x_ref.shape[0]
    slot = i % 2

    def issue(dref, sl):
        for r in range(ts):
            for k in range(2):
                _row_copy(ys_hbm, dref[0, k, r], gbuf.at[sl, k], r, sem.at[sl]).start(priority=k)

    @pl.when(i == 0)
    def _():
        issue(dcur_ref, 0)

    @pl.when(i + 1 < n)
    def _():
        issue(dnxt_ref, 1 - slot)

    for k in range(2):
        pltpu.make_async_copy(ys_hbm.at[pl.ds(0, ts), :], gbuf.at[slot, k], sem.at[slot]).wait()
    y = rw_ref[:, 0:1] * gbuf[slot, 0] + rw_ref[:, 1:2] * gbuf[slot, 1]
    x2 = x_ref[...] + mod_ref[5:6, :] * y
    if final:
        o_ref[...] = (x2 * lax.rsqrt(jnp.mean(x2 * x2, axis=-1, keepdims=True) + EPS) * nw_ref[...]).astype(o_ref.dtype)
    else:
        x2_ref[...] = x2
        h_ref[...] = _rms_mod(x2, nw_ref[...], modn_ref[1:2, :], modn_ref[0:1, :]).astype(h_ref.dtype)


def _combine(x1, ys, dest, rw, mod_l, norm_w, mod_next, final, out_dtype):
    B, S, D = x1.shape
    T = B * S
    ts = min(COMBINE_TILE, S)
    nsb = S // ts
    nt = T // ts
    dest3 = dest.reshape(2, nt, ts).transpose(1, 0, 2)
    tile = pl.BlockSpec((ts, D), lambda i: (i, 0))
    modspec = pl.BlockSpec((None, 6, D), lambda i: (i // nsb, 0, 0))
    if final:
        out_specs = tile
        out_shape = jax.ShapeDtypeStruct((T, D), out_dtype)
    else:
        out_specs = [tile, tile]
        out_shape = [jax.ShapeDtypeStruct((T, D), F32), jax.ShapeDtypeStruct((T, D), BF16)]
    out = pl.pallas_call(
        functools.partial(_combine_kernel, final),
        grid=(nt,),
        in_specs=[pl.BlockSpec((1, 2, ts), lambda i: (i, 0, 0), memory_space=pltpu.SMEM),
                  pl.BlockSpec((1, 2, ts), lambda i: (jnp.minimum(i + 1, nt - 1), 0, 0), memory_space=pltpu.SMEM),
                  tile, pl.BlockSpec((ts, SUBLANES), lambda i: (i, 0)), modspec,
                  pl.BlockSpec((1, D), lambda i: (0, 0)), modspec,
                  pl.BlockSpec(memory_space=pl.ANY)],
        out_specs=out_specs,
        out_shape=out_shape,
        scratch_shapes=[pltpu.VMEM((2, 2, ts, D), F32), pltpu.SemaphoreType.DMA((2,))],
        compiler_params=_params(1),
        name="combine_final" if final else "combine_prenorm",
    )(dest3, dest3, x1.reshape(T, D), rw, mod_l, norm_w.reshape(1, D), mod_next, ys)
    if final:
        return out.reshape(B, S, D)
    return out[0].reshape(B, S, D), out[1].reshape(B, S, D)


def _seq_tile(S, want):
    ts = min(want, S)
    assert S % ts == 0 and ts % CHUNK == 0
    return ts


def kernel(x, c, positions, w_ada, b_ada, norm_mix_w, w_in, ssd_conv_w, ssd_conv_b, ssd_dt_bias, ssd_a_log,
           ssd_d, ssd_norm_w, hg_lb, hg_norm_w, w_ret_o, w_hg_o, w_ssd_o, w_out, norm_ffn_w, moe_w_group,
           moe_b_group, moe_w_expert, moe_b_expert, moe_w1, moe_w3, moe_w2, final_norm_w):
    B, S, D = x.shape
    L = w_ada.shape[0]
    T = B * S
    ts_el = _seq_tile(S, 512)
    ts_mix = _seq_tile(S, 256)
    nb_mix = 2 if B % 2 == 0 else 1
    nb_ret = 4 if B % 4 == 0 else nb_mix

    mod = _adaln(c, w_ada, b_ada)
    cos, sin = _rope_tables(positions, ts_el)
    lb_soft = jax.nn.softmax(hg_lb.astype(F32), axis=0)
    lb_all = jnp.cumsum(lb_soft, axis=0) - lb_soft[0]

    qk = RET_HEADS * RET_DK
    vw = RET_HEADS * RET_DV
    hgw = HG_HEADS * HG_DK
    inner = SSD_HEADS * SSD_P
    cdim = inner + 2 * SSD_GROUPS * SSD_N
    sizes = (qk, qk, vw, vw, hgw, hgw, hgw, hgw, inner, cdim, SSD_HEADS, 3 * D)
    offs = [0]
    for n in sizes:
        offs.append(offs[-1] + n)

    xcur = x.astype(F32)
    h = _prenorm(xcur, norm_mix_w[0].astype(F32), mod[0], ts_el)
    for l in range(L):
        wl = w_in[l].astype(BF16)
        cols = [wl[:, offs[i]:offs[i + 1]] for i in range(len(sizes))]
        y_ret = _retention(h, cos, sin, cols[0], cols[1], cols[2], cols[3], w_ret_o[l].astype(BF16), ts_mix,
                           nb_ret)
        y_hg = _hgrn2(h, cols[4], cols[5], cols[6], cols[7], w_hg_o[l].astype(BF16), lb_all[l],
                      hg_norm_w[l].astype(F32), ts_mix, nb_mix)
        y_ssd = _ssd(h, cols[8], cols[9], cols[10], w_ssd_o[l].astype(BF16), ssd_conv_w[l], ssd_conv_b[l],
                     ssd_dt_bias[l], ssd_a_log[l], ssd_d[l], ssd_norm_w[l], ts_mix, 1)
        wr = jnp.zeros((SUBLANES + MOE_EXPERTS, D), F32)
        wr = wr.at[0:MOE_GROUPS].set(moe_w_group[l].astype(F32).T).at[SUBLANES:].set(moe_w_expert[l].astype(F32).T)
        br = jnp.zeros((SUBLANES + MOE_EXPERTS, 1), F32)
        br = br.at[0:MOE_GROUPS, 0].set(moe_b_group[l].astype(F32)).at[SUBLANES:, 0].set(moe_b_expert[l].astype(F32))
        wr_hi = wr.astype(BF16)
        wr_lo = (wr - wr_hi.astype(F32)).astype(BF16)
        x1, h2, route_w, route_i, counts = _merge(xcur, h, y_ret, y_hg, y_ssd, cols[11], w_out[l].astype(BF16),
                                                  mod[l], norm_ffn_w[l].astype(F32), wr_hi, wr_lo, br, ts_el)
        dest, block_e, pad_end, pad_len, nactive, nrows = _dispatch_plan(route_i, counts, T)
        xs = _dispatch(h2.reshape(T, D), dest, pad_end, pad_len, nactive, nrows)
        ys = _experts(xs, block_e, nactive, moe_w1, moe_w3, moe_w2, l)
        rw = route_w.T
        if l + 1 < L:
            xcur, h = _combine(x1, ys, dest, rw, mod[l], norm_mix_w[l + 1].astype(F32), mod[l + 1], False, x.dtype)
        else:
            out = _combine(x1, ys, dest, rw, mod[l], final_norm_w.astype(F32), mod[l], True, x.dtype)
    return out
```

```python
import functools

import jax
import jax.numpy as jnp
from jax import lax
from jax.experimental import pallas as pl
from jax.experimental.pallas import tpu as pltpu

F32 = jnp.float32
BF16 = jnp.bfloat16

EPS = 1e-6
ROPE_BASE = 10000.0
RET_HEADS = 4
RET_DK = 128
RET_DV = 256
HG_HEADS = 8
HG_DK = 128
SSD_HEADS = 32
SSD_P = 64
SSD_GROUPS = 4
SSD_N = 128
SSD_HPG = SSD_HEADS // SSD_GROUPS
SSD_CONV = 4
MOE_GROUPS = 4
MOE_PER_GROUP = 8
MOE_EXPERTS = 32
MOE_BLOCK = 512
CHUNK = 128
LANES = 128
SUBLANES = 8
VMEM_LIMIT = 56 * 1024 * 1024
HG_LEVELS = (64, 32, 16, 8, 4, 2, 1)
LOG2E = 1.4426950408889634
NEG_BIG = -1e30


def _params(n_axes, vmem=VMEM_LIMIT):
    return pltpu.CompilerParams(dimension_semantics=("arbitrary",) * n_axes,
                                vmem_limit_bytes=vmem)


def _const_spec(shape):
    nd = len(shape)
    return pl.BlockSpec(shape, lambda *_: (0,) * nd, pipeline_mode=pl.Buffered(1))


def _sigmoid(x):
    return 0.5 * jnp.tanh(0.5 * x) + 0.5


def _silu(x):
    h = 0.5 * x
    return h * jnp.tanh(h) + h


def _dot(a, b):
    return jnp.dot(a, b, preferred_element_type=F32)


def _dot_nt(a, b):
    return lax.dot_general(a, b, (((1,), (1,)), ((), ())), preferred_element_type=F32)


def _dot_tn(a, b):
    return lax.dot_general(a, b, (((0,), (0,)), ((), ())), preferred_element_type=F32)


def _dot3(m_bf16, x_f32):
    hi = x_f32.astype(BF16)
    r1 = x_f32 - hi.astype(F32)
    mid = r1.astype(BF16)
    lo = (r1 - mid.astype(F32)).astype(BF16)
    return _dot(m_bf16, hi) + _dot(m_bf16, mid) + _dot(m_bf16, lo)


def _dot2(m_bf16, x_f32):
    hi = x_f32.astype(BF16)
    lo = (x_f32 - hi.astype(F32)).astype(BF16)
    return _dot(m_bf16, hi) + _dot(m_bf16, lo)


def _ada_kernel(c_ref, w_ref, b_ref, o_ref):
    c = c_ref[...]
    o_ref[...] = _dot(_silu(c).astype(BF16), w_ref[...].astype(BF16)) + b_ref[...]


def _adaln(c, w_ada, b_ada):
    L, D, N = w_ada.shape
    B = c.shape[0]
    cp = jnp.zeros((SUBLANES, D), F32).at[:B].set(c.astype(F32))
    tn = 1536
    out = pl.pallas_call(
        _ada_kernel,
        grid=(L, N // tn),
        in_specs=[pl.BlockSpec((SUBLANES, D), lambda l, n: (0, 0)),
                  pl.BlockSpec((None, D, tn), lambda l, n: (l, 0, n)),
                  pl.BlockSpec((None, 1, tn), lambda l, n: (l, 0, n))],
        out_specs=pl.BlockSpec((None, SUBLANES, tn), lambda l, n: (l, 0, n)),
        out_shape=jax.ShapeDtypeStruct((L, SUBLANES, N), F32),
        compiler_params=_params(2),
        name="adaln",
    )(cp, w_ada, b_ada.reshape(L, 1, N))
    return out[:, :B].reshape(L, B, 6, D)


def _rope_kernel(pos_ref, invf_ref, sign_ref, cos_ref, sin_ref):
    ts = pos_ref.shape[0]
    hs = ts // 2
    lo = lax.broadcasted_iota(jnp.int32, (1, RET_DK), 1) < RET_DK // 2
    pos = jnp.where(lo, pos_ref[0:hs, :], pos_ref[hs:ts, :]).astype(F32)
    ang = pos * invf_ref[...]
    for fn, out_ref, scale in ((jnp.cos, cos_ref, None), (jnp.sin, sin_ref, sign_ref)):
        v = fn(ang)
        w = pltpu.roll(v, RET_DK // 2, 1)
        first = jnp.where(lo, v, w)
        second = jnp.where(lo, w, v)
        if scale is not None:
            first = first * scale[...]
            second = second * scale[...]
        out_ref[0:hs, :] = first
        out_ref[hs:ts, :] = second


def _rope_tables(positions, ts):
    B, S = positions.shape
    half = RET_DK // 2
    inv_freq = ROPE_BASE ** (-jnp.linspace(0.0, 1.0, half, dtype=F32))
    invf = jnp.concatenate([inv_freq, inv_freq]).reshape(1, RET_DK)
    sign = jnp.concatenate([-jnp.ones((half,), F32), jnp.ones((half,), F32)]).reshape(1, RET_DK)
    spec = pl.BlockSpec((None, ts, RET_DK), lambda b, s: (b, s, 0))
    return pl.pallas_call(
        _rope_kernel,
        grid=(B, S // ts),
        in_specs=[pl.BlockSpec((None, ts, 1), lambda b, s: (b, s, 0)),
                  pl.BlockSpec((1, RET_DK), lambda b, s: (0, 0)),
                  pl.BlockSpec((1, RET_DK), lambda b, s: (0, 0))],
        out_specs=[spec, spec],
        out_shape=[jax.ShapeDtypeStruct((B, S, RET_DK), F32)] * 2,
        compiler_params=_params(2),
        name="rope_tables",
    )(positions.reshape(B, S, 1), invf, sign)


def _rms_mod(x, w, scale, shift):
    y = x * lax.rsqrt(jnp.mean(x * x, axis=-1, keepdims=True) + EPS)
    return (y * w) * (1.0 + scale) + shift


def _prenorm_kernel(x_ref, w_ref, mod_ref, o_ref):
    o_ref[...] = _rms_mod(x_ref[...], w_ref[...], mod_ref[1:2, :], mod_ref[0:1, :]).astype(o_ref.dtype)


def _prenorm(x, w, mod_l, ts):
    B, S, D = x.shape
    return pl.pallas_call(
        _prenorm_kernel,
        grid=(B, S // ts),
        in_specs=[pl.BlockSpec((None, ts, D), lambda b, s: (b, s, 0)),
                  pl.BlockSpec((1, D), lambda b, s: (0, 0)),
                  pl.BlockSpec((None, 6, D), lambda b, s: (b, 0, 0))],
        out_specs=pl.BlockSpec((None, ts, D), lambda b, s: (b, s, 0)),
        out_shape=jax.ShapeDtypeStruct((B, S, D), BF16),
        compiler_params=_params(2),
        name="prenorm",
    )(x, w.reshape(1, D), mod_l)


def _ret_kernel(h_ref, hn_ref, cos_ref, cosn_ref, sin_ref, sinn_ref, wq_ref, wk_ref, wv_ref, wg_ref, wo_ref,
                di_ref, dq_ref, dk_ref, dc_ref, o_ref,
                st_ref, q_s, k_s, v_s, g_s):
    nb, ts, D = h_ref.shape
    C = CHUNK
    n = ts // C
    R = nb * C
    scale = RET_DK ** -0.5

    def project(hc, cos, sin, slot):
        hc = hc.reshape(R, D)
        cos = cos.reshape(R, RET_DK)
        sin = sin.reshape(R, RET_DK)
        q = _dot(hc, wq_ref[...])
        k = _dot(hc, wk_ref[...])
        for hh in range(RET_HEADS):
            sl = slice(hh * RET_DK, (hh + 1) * RET_DK)
            qh = q[:, sl]
            kh = k[:, sl]
            q_s[slot, :, sl] = (qh * cos + pltpu.roll(qh, RET_DK // 2, 1) * sin).astype(BF16)
            k_s[slot, :, sl] = (kh * cos + pltpu.roll(kh, RET_DK // 2, 1) * sin) * scale
        v_s[slot] = _dot(hc, wv_ref[...]).astype(BF16)
        g_s[slot] = _silu(_dot(hc, wg_ref[...]))

    @pl.when(pl.program_id(1) == 0)
    def _():
        st_ref[...] = jnp.zeros_like(st_ref)
        project(h_ref[:, 0:C, :], cos_ref[:, 0:C, :], sin_ref[:, 0:C, :], 0)

    def one_chunk(bi, slot):
        rows = slice(bi * C, (bi + 1) * C)
        outs = []
        for hh in range(RET_HEADS):
            ksl = slice(hh * RET_DK, (hh + 1) * RET_DK)
            vsl = slice(hh * RET_DV, (hh + 1) * RET_DV)
            qc = q_s[slot, rows, ksl]
            kc = k_s[slot, rows, ksl]
            vc = v_s[slot, rows, vsl]
            st = st_ref[bi * RET_HEADS + hh]
            scores = _dot_nt(qc, kc.astype(BF16)) * di_ref[hh]
            o = _dot(scores.astype(BF16), vc) + _dot(qc, st.astype(BF16)) * dq_ref[hh]
            st_ref[bi * RET_HEADS + hh] = st * dc_ref[hh] + _dot_tn((kc * dk_ref[hh]).astype(BF16), vc)
            mu = jnp.mean(o, axis=-1, keepdims=True)
            d = o - mu
            var = jnp.mean(d * d, axis=-1, keepdims=True)
            outs.append((g_s[slot, rows, vsl] * (d * lax.rsqrt(var + EPS))).astype(BF16))
        return jnp.concatenate(outs, axis=1)

    for j in range(n):
        if j + 1 < n:
            nsl = slice((j + 1) * C, (j + 2) * C)
            project(h_ref[:, nsl, :], cos_ref[:, nsl, :], sin_ref[:, nsl, :], (j + 1) % 2)
        else:
            project(hn_ref[...], cosn_ref[...], sinn_ref[...], (j + 1) % 2)
        o = jnp.concatenate([one_chunk(bi, j % 2) for bi in range(nb)], axis=0)
        o_ref[:, j * C:(j + 1) * C, :] = _dot(o, wo_ref[...]).astype(o_ref.dtype).reshape(nb, C, D)


def _retention(h, cos, sin, wq, wk, wv, wg, wo, ts, nb):
    B, S, D = h.shape
    H, C = RET_HEADS, CHUNK
    log_gamma = jnp.log(1.0 - 2.0 ** (-5.0 - jnp.arange(H, dtype=F32)))
    idx = jnp.arange(C, dtype=F32)
    rel = idx[:, None] - idx[None, :]
    di = jnp.where(rel >= 0.0, jnp.exp(log_gamma[:, None, None] * jnp.maximum(rel, 0.0)), 0.0)
    dq = jnp.broadcast_to(jnp.exp(log_gamma[:, None] * (idx[None, :] + 1.0))[:, :, None], (H, C, RET_DV))
    dk = jnp.broadcast_to(jnp.exp(log_gamma[:, None] * (C - 1.0 - idx[None, :]))[:, :, None], (H, C, RET_DK))
    dc = jnp.broadcast_to(jnp.exp(log_gamma * C)[:, None, None], (H, 1, RET_DV))
    qk, vw = H * RET_DK, H * RET_DV
    n = ts // C
    assert n % 2 == 0
    tile = lambda w: pl.BlockSpec((nb, ts, w), lambda b, s: (b, s, 0))
    nxt = lambda w: pl.BlockSpec((nb, C, w), lambda b, s: (b, jnp.minimum((s + 1) * n, S // C - 1), 0))
    R = nb * C
    return pl.pallas_call(
        _ret_kernel,
        grid=(B // nb, S // ts),
        in_specs=[tile(D), nxt(D), tile(RET_DK), nxt(RET_DK), tile(RET_DK), nxt(RET_DK),
                  _const_spec((D, qk)), _const_spec((D, qk)), _const_spec((D, vw)), _const_spec((D, vw)),
                  _const_spec((vw, D)),
                  _const_spec((H, C, C)), _const_spec((H, C, RET_DV)), _const_spec((H, C, RET_DK)),
                  _const_spec((H, 1, RET_DV))],
        out_specs=tile(D),
        out_shape=jax.ShapeDtypeStruct((B, S, D), BF16),
        scratch_shapes=[pltpu.VMEM((nb * H, RET_DK, RET_DV), F32),
                        pltpu.VMEM((2, R, qk), BF16), pltpu.VMEM((2, R, qk), F32),
                        pltpu.VMEM((2, R, vw), BF16), pltpu.VMEM((2, R, vw), F32)],
        compiler_params=_params(2),
        name="retention",
    )(h, h, cos, cos, sin, sin, wq, wk, wv, wg, wo, di, dq, dk, dc)


def _hg_level_matrices():
    C = CHUNK
    H = C // 2
    r = jnp.arange(C)
    stack = (r[:, None] >= r[None, :]).astype(BF16)
    t = jnp.arange(H)[:, None]
    s = jnp.arange(C)[None, :] % H
    masks = []
    for b in HG_LEVELS[1:]:
        masks.append(((t // (2 * b)) == (s // (2 * b))) & ((t & b) != 0) & ((s & b) == 0))
    masks.append(t == s)
    return stack, jnp.stack(masks).astype(F32)


def _hg_kernel(h_ref, hn_ref, wq_ref, wf_ref, wi_ref, wg_ref, wo_ref, lb_ref, nw_ref, cm_ref, mk_ref, o_ref,
               st_ref, q_s, k_s, v_s, lf_s, g_s):
    nb, ts, D = h_ref.shape
    C = CHUNK
    W = HG_HEADS * HG_DK
    n = ts // C
    lb = lb_ref[...]

    def project(hc, slot):
        q_s[slot] = _silu(_dot(hc, wq_ref[...]))
        hf = _dot(hc, wf_ref[...])
        lf_s[slot] = jnp.log(lb + (1.0 - lb) * _sigmoid(hf)) * LOG2E
        k_s[slot] = (1.0 - lb) * _sigmoid(-hf)
        v_s[slot] = _dot(hc, wi_ref[...]).astype(BF16)
        g_s[slot] = _silu(_dot(hc, wg_ref[...]))

    @pl.when(pl.program_id(1) == 0)
    def _():
        st_ref[...] = jnp.zeros_like(st_ref)
        project(h_ref[:, 0:C, :].reshape(nb * C, D), 0)

    odd_row = {b: (lax.broadcasted_iota(jnp.int32, (C, 1), 0) & b) != 0 for b in HG_LEVELS}

    def one_chunk(bi, slot):
        rows = slice(bi * C, (bi + 1) * C)
        lf = lf_s[slot, rows, :]
        q = q_s[slot, rows, :]
        k = k_s[slot, rows, :]
        G = _dot2(cm_ref[...], lf)
        H = C // 2
        acc = [None] * HG_HEADS
        cross = [None] * HG_HEADS
        left = lax.broadcasted_iota(jnp.int32, (H, C), 1) < H
        G3 = G.reshape(C // SUBLANES, SUBLANES, W)
        sub8 = lax.broadcasted_iota(jnp.int32, (C // SUBLANES, SUBLANES, 1), 1)

        def tile_row(i):
            return jnp.broadcast_to(G3[:, i:i + 1, :], G3.shape)
        for li, b in enumerate(HG_LEVELS):
            if b >= SUBLANES:
                pieces = []
                for j in range(C // b):
                    ref = G[(j // 2) * 2 * b + b - 1:(j // 2) * 2 * b + b, :]
                    blk = slice(j * b, (j + 1) * b)
                    if j % 2 == 1:
                        pieces.append(q[blk] * jnp.exp2(G[blk] - ref))
                    else:
                        pieces.append(k[blk] * jnp.exp2(ref - G[blk]))
                z = jnp.concatenate(pieces, axis=0)
            elif b > 1:
                ref = tile_row(3) if b == 4 else jnp.where(sub8 < 4, tile_row(1), tile_row(5))
                e = -jnp.abs(G3 - ref)
                z = jnp.where(odd_row[b], q, k) * jnp.exp2(e.reshape(C, W))
            else:
                z = jnp.where(odd_row[b], q * jnp.exp2(lf), k)
            z = z.astype(BF16)
            for hh in range(HG_HEADS):
                zh = z[:, hh * HG_DK:(hh + 1) * HG_DK]
                if li == 0:
                    cross[hh] = _dot_nt(zh[H:], zh)
                else:
                    zt, zb = zh[:H], zh[H:]
                    zero = jnp.zeros_like(zt)
                    rhs = jnp.concatenate([jnp.concatenate([zt, zero], axis=1),
                                           jnp.concatenate([zero, zb], axis=1)], axis=0)
                    a = _dot_nt(jnp.concatenate([zt, zb], axis=1), rhs) * mk_ref[li - 1]
                    acc[hh] = a if li == 1 else acc[hh] + a
        g_last = G[C - 1:C, :]
        qg = (q * jnp.exp2(G)).astype(BF16)
        kh = (k * jnp.exp2(g_last - G)).astype(BF16)
        qk = q * k
        dec = jnp.exp2(g_last)
        eye = mk_ref[len(HG_LEVELS) - 1]
        outs = []
        for hh in range(HG_HEADS):
            sl = slice(hh * HG_DK, (hh + 1) * HG_DK)
            vh = v_s[slot, rows, sl]
            diag = jnp.sum(qk[:, sl], axis=-1, keepdims=True)
            dblk = acc[hh] + jnp.where(left, diag[:H], diag[H:]) * eye
            a_top = jnp.where(left, dblk, 0.0).astype(BF16)
            a_bot = jnp.where(left, cross[hh], dblk).astype(BF16)
            st = st_ref[bi * HG_HEADS + hh]
            o = (jnp.concatenate([_dot(a_top, vh), _dot(a_bot, vh)], axis=0)
                 + _dot_nt(qg[:, sl], st.astype(BF16)))
            st_ref[bi * HG_HEADS + hh] = st * dec[:, sl] + _dot_tn(vh, kh[:, sl])
            on = o * lax.rsqrt(jnp.mean(o * o, axis=-1, keepdims=True) + EPS) * nw_ref[...]
            outs.append((on * g_s[slot, rows, sl]).astype(BF16))
        return jnp.concatenate(outs, axis=1)

    for j in range(n):
        nxt = h_ref[:, (j + 1) * C:(j + 2) * C, :] if j + 1 < n else hn_ref[...]
        project(nxt.reshape(nb * C, D), (j + 1) % 2)
        o = jnp.concatenate([one_chunk(bi, j % 2) for bi in range(nb)], axis=0)
        o_ref[:, j * C:(j + 1) * C, :] = _dot(o, wo_ref[...]).astype(o_ref.dtype).reshape(nb, C, D)


def _hgrn2(h, wq, wf, wi, wg, wo, lb, norm_w, ts, nb):
    B, S, D = h.shape
    W = HG_HEADS * HG_DK
    C = CHUNK
    n = ts // C
    assert n % 2 == 0
    R = nb * C
    cm, mk = _hg_level_matrices()
    tile = lambda w: pl.BlockSpec((nb, ts, w), lambda b, s: (b, s, 0))
    nxt = pl.BlockSpec((nb, C, D), lambda b, s: (b, jnp.minimum((s + 1) * n, S // C - 1), 0))
    return pl.pallas_call(
        _hg_kernel,
        grid=(B // nb, S // ts),
        in_specs=[tile(D), nxt] + [_const_spec((D, W))] * 4 + [_const_spec((W, D)), _const_spec((1, W)),
                  _const_spec((1, HG_DK)), _const_spec(cm.shape), _const_spec(mk.shape)],
        out_specs=tile(D),
        out_shape=jax.ShapeDtypeStruct((B, S, D), BF16),
        scratch_shapes=[pltpu.VMEM((nb * HG_HEADS, HG_DK, HG_DK), F32),
                        pltpu.VMEM((2, R, W), F32), pltpu.VMEM((2, R, W), F32), pltpu.VMEM((2, R, W), BF16),
                        pltpu.VMEM((2, R, W), F32), pltpu.VMEM((2, R, W), F32)],
        compiler_params=_params(2),
        name="hgrn2",
    )(h, h, wq, wf, wi, wg, wo, lb.reshape(1, W), norm_w.reshape(1, HG_DK), cm, mk)


def _shift_rows(x, k, head):
    r, w = x.shape
    t = x.reshape(r // SUBLANES, SUBLANES, w)
    rot = pltpu.roll(t, k, 1)
    hrot = pltpu.roll(head.reshape(1, SUBLANES, w), k, 1)
    prev = jnp.concatenate([hrot, rot[:-1]], axis=0)
    sub = lax.broadcasted_iota(jnp.int32, (1, SUBLANES, 1), 1)
    return jnp.where(sub < k, prev, rot).reshape(r, w)


def _softplus(x):
    return jnp.maximum(x, 0.0) + jnp.log(1.0 + jnp.exp(-jnp.abs(x)))


def _ssd_kernel(h_ref, hn_ref, wz_ref, wx_ref, wdt_ref, wo_ref, cw_ref, cb_ref, dtb_ref, a_ref, dl_ref, nw_ref,
                tril_ref, negm_ref, ex_ref, o_ref,
                st_ref, halo_s, u_s, xc_s, z_s, dt_s):
    nb, ts, D = h_ref.shape
    C = CHUNK
    n = ts // C
    inner = SSD_HEADS * SSD_P
    gw = SSD_GROUPS * SSD_N
    gp = SSD_HPG * SSD_P
    halo = SUBLANES

    def project(hc, slot):
        z_s[slot] = _silu(_dot(hc, wz_ref[...]))
        dt_s[slot] = _softplus(_dot(hc, wdt_ref[...]) + dtb_ref[...])
        u_s[...] = _dot(hc, wx_ref[...])
        for bi in range(nb):
            x = u_s[bi * C:(bi + 1) * C, :]
            hl = halo_s[bi]
            x1 = _shift_rows(x, 1, hl)
            pair = cw_ref[1:2, :] * x + cw_ref[0:1, :] * x1
            pair_hl = cw_ref[1:2, :] * hl + cw_ref[0:1, :] * pltpu.roll(hl, 1, 0)
            y = (cw_ref[3:4, :] * x + cw_ref[2:3, :] * x1) + _shift_rows(pair, 2, pair_hl)
            xc_s[slot, bi * C:(bi + 1) * C, :] = _silu(y + cb_ref[...])
            halo_s[bi] = u_s[(bi + 1) * C - halo:(bi + 1) * C, :]

    @pl.when(pl.program_id(1) == 0)
    def _():
        st_ref[...] = jnp.zeros_like(st_ref)
        halo_s[...] = jnp.zeros_like(halo_s)
        project(h_ref[:, 0:C, :].reshape(nb * C, D), 0)

    low_half = lax.broadcasted_iota(jnp.int32, (C, LANES), 1) < SSD_P

    def one_chunk(bi, slot):
        rows = slice(bi * C, (bi + 1) * C)
        dt = dt_s[slot, rows, :]
        da = dt * a_ref[...]
        cum = _dot3(tril_ref[...], da) * LOG2E
        cum_t = cum.T
        row_t = (cum - jnp.log(dt) * LOG2E).T
        cum_last = cum[C - 1:C, :]
        wfac = dt * jnp.exp2(cum_last - cum)
        w_hi = wfac.astype(BF16)
        w_lo = (wfac - w_hi.astype(F32)).astype(BF16)
        wexp = _dot(w_hi, ex_ref[...]) + _dot(w_lo, ex_ref[...])
        dl = jnp.broadcast_to(jnp.exp2(cum_last), (SUBLANES, LANES))
        d_hi = dl.astype(BF16)
        d_lo = (dl - d_hi.astype(F32)).astype(BF16)
        dexp = (_dot(d_hi, ex_ref[...]) + _dot(d_lo, ex_ref[...]))[0:1, :]
        ypairs = []
        for g in range(SSD_GROUPS):
            bg = xc_s[slot, rows, inner + g * SSD_N:inner + (g + 1) * SSD_N]
            cg = xc_s[slot, rows, inner + gw + g * SSD_N:inner + gw + (g + 1) * SSD_N]
            bgb = bg.astype(BF16)
            cb = _dot_nt(cg.astype(BF16), bgb)
            st = st_ref[bi * SSD_GROUPS + g]
            stb = st.astype(BF16)
            xg = xc_s[slot, rows, g * gp:(g + 1) * gp]
            xgb = xg.astype(BF16)
            for jp in range(SSD_HPG // 2):
                lsl = slice(jp * LANES, (jp + 1) * LANES)
                rhs = jnp.concatenate([xgb[:, lsl], stb[:, lsl]], axis=0)
                ys = []
                for hd in (g * SSD_HPG + 2 * jp, g * SSD_HPG + 2 * jp + 1):
                    colb = jnp.broadcast_to(cum[:, hd:hd + 1], (C, C))
                    w = cb * jnp.exp2(colb - row_t[hd:hd + 1, :] + negm_ref[...])
                    ce = cg * jnp.exp2(colb)
                    ys.append(_dot(jnp.concatenate([w.astype(BF16), ce.astype(BF16)], axis=1), rhs))
                ypairs.append(jnp.where(low_half, ys[0], ys[1]))
            gsl = slice(g * gp, (g + 1) * gp)
            st_ref[bi * SSD_GROUPS + g] = st * dexp[:, gsl] + _dot_tn(bgb, (xg * wexp[:, gsl]).astype(BF16))
        gi = inner // SSD_GROUPS
        ppg = gi // LANES
        outs = []
        for g in range(SSD_GROUPS):
            sl = slice(g * gi, (g + 1) * gi)
            y = jnp.concatenate(ypairs[g * ppg:(g + 1) * ppg], axis=1)
            seg = (y + dl_ref[:, sl] * xc_s[slot, rows, sl]) * z_s[slot, rows, sl]
            nrm = seg * lax.rsqrt(jnp.mean(seg * seg, axis=-1, keepdims=True) + EPS)
            outs.append((nrm * nw_ref[:, sl]).astype(BF16))
        return jnp.concatenate(outs, axis=1)

    for j in range(n):
        nxt = h_ref[:, (j + 1) * C:(j + 2) * C, :] if j + 1 < n else hn_ref[...]
        project(nxt.reshape(nb * C, D), (j + 1) % 2)
        o = jnp.concatenate([one_chunk(bi, j % 2) for bi in range(nb)], axis=0)
        o_ref[:, j * C:(j + 1) * C, :] = _dot(o, wo_ref[...]).astype(o_ref.dtype).reshape(nb, C, D)


def _ssd(h, wz, wx, wdt, wo, conv_w, conv_b, dt_bias, a_log, d_skip, norm_w, ts, nb):
    B, S, D = h.shape
    C = CHUNK
    n = ts // C
    assert n % 2 == 0
    inner = SSD_HEADS * SSD_P
    cdim = inner + 2 * SSD_GROUPS * SSD_N
    pad = LANES - SSD_HEADS
    wdt_p = jnp.pad(wdt, ((0, 0), (0, pad)))
    dtb = jnp.pad(dt_bias.astype(F32), (0, pad)).reshape(1, LANES)
    a = jnp.pad(-jnp.exp(a_log.astype(F32)), (0, pad)).reshape(1, LANES)
    dl = jnp.repeat(d_skip.astype(F32), SSD_P).reshape(1, inner)
    r = jnp.arange(C)
    tril = (r[:, None] >= r[None, :]).astype(BF16)
    negm = jnp.where(r[:, None] >= r[None, :], 0.0, NEG_BIG).astype(F32)
    ex = (jnp.arange(LANES)[:, None] == (jnp.arange(inner)[None, :] // SSD_P)).astype(BF16)
    tile = lambda w: pl.BlockSpec((nb, ts, w), lambda b, s: (b, s, 0))
    nxt = pl.BlockSpec((nb, C, D), lambda b, s: (b, jnp.minimum((s + 1) * n, S // C - 1), 0))
    R = nb * C
    return pl.pallas_call(
        _ssd_kernel,
        grid=(B // nb, S // ts),
        in_specs=[tile(D), nxt, _const_spec((D, inner)), _const_spec((D, cdim)), _const_spec((D, LANES)),
                  _const_spec((inner, D)), _const_spec((SSD_CONV, cdim)), _const_spec((1, cdim)),
                  _const_spec((1, LANES)), _const_spec((1, LANES)), _const_spec((1, inner)),
                  _const_spec((1, inner)), _const_spec((C, C)), _const_spec((C, C)),
                  _const_spec((LANES, inner))],
        out_specs=tile(D),
        out_shape=jax.ShapeDtypeStruct((B, S, D), BF16),
        scratch_shapes=[pltpu.VMEM((nb * SSD_GROUPS, SSD_N, SSD_HPG * SSD_P), F32),
                        pltpu.VMEM((nb, SUBLANES, cdim), F32), pltpu.VMEM((R, cdim), F32),
                        pltpu.VMEM((2, R, cdim), F32),
                        pltpu.VMEM((2, R, inner), F32), pltpu.VMEM((2, R, LANES), F32)],
        compiler_params=_params(2),
        name="ssd",
    )(h, h, wz, wx, wdt_p, wo, conv_w.astype(F32), conv_b.astype(F32).reshape(1, cdim), dtb, a, dl,
      norm_w.astype(F32).reshape(1, inner), tril, negm, ex)


def _merge_kernel(x_ref, h_ref, yr_ref, yh_ref, ys_ref, wgl_ref, wout_ref, mod_ref, nw_ref,
                  wr_hi_ref, wr_lo_ref, br_ref, ut_ref, x1_ref, h2_ref, rf_ref, ri_ref, cnt_ref, run_ref):
    D = x_ref.shape[1]
    ts = x_ref.shape[0]

    @pl.when((pl.program_id(0) == 0) & (pl.program_id(1) == 0))
    def _():
        run_ref[...] = jnp.zeros_like(run_ref)

    h = h_ref[...]
    gates = _sigmoid(_dot(h, wgl_ref[...]))
    m = (gates[:, 0:D] * yr_ref[...].astype(F32) + gates[:, D:2 * D] * yh_ref[...].astype(F32)
         + gates[:, 2 * D:3 * D] * ys_ref[...].astype(F32))
    y = _dot(m.astype(BF16), wout_ref[...])
    x1 = x_ref[...] + mod_ref[2:3, :] * y
    x1_ref[...] = x1
    h2 = _rms_mod(x1, nw_ref[...], mod_ref[4:5, :], mod_ref[3:4, :])
    h2_ref[...] = h2
    h_hi = h2.astype(BF16)
    h_lo = (h2 - h_hi.astype(F32)).astype(BF16)
    lg = (_dot_nt(wr_hi_ref[...], h_hi) + _dot_nt(wr_hi_ref[...], h_lo) + _dot_nt(wr_lo_ref[...], h_hi)
          + br_ref[...])
    gl = lg[0:MOE_GROUPS]
    gmax = jnp.max(gl, axis=0, keepdims=True)
    gi = lax.broadcasted_iota(jnp.int32, gl.shape, 0)
    gtop = jnp.min(jnp.where(gl == gmax, gi, MOE_GROUPS), axis=0, keepdims=True)
    p_g = 1.0 / jnp.sum(jnp.exp(gl - gmax), axis=0, keepdims=True)
    el = jnp.zeros((MOE_PER_GROUP, ts), F32)
    for g in range(MOE_GROUPS):
        lo = SUBLANES + g * MOE_PER_GROUP
        el = jnp.where(gtop == g, lg[lo:lo + MOE_PER_GROUP], el)
    ei = lax.broadcasted_iota(jnp.int32, el.shape, 0)
    m1 = jnp.max(el, axis=0, keepdims=True)
    i1 = jnp.min(jnp.where(el == m1, ei, MOE_PER_GROUP), axis=0, keepdims=True)
    el2 = jnp.where(ei == i1, -jnp.inf, el)
    m2 = jnp.max(el2, axis=0, keepdims=True)
    i2 = jnp.min(jnp.where(el2 == m2, ei, MOE_PER_GROUP), axis=0, keepdims=True)
    e21 = jnp.exp(m2 - m1)
    den = 1.0 + e21
    w1 = (1.0 / den) * p_g
    w2 = (e21 / den) * p_g
    zf = jnp.zeros((SUBLANES - 2, ts), F32)
    rf_ref[...] = jnp.concatenate([w1, w2, zf], axis=0)
    e1 = gtop * MOE_PER_GROUP + i1
    e2 = gtop * MOE_PER_GROUP + i2
    ex = lax.broadcasted_iota(jnp.int32, (MOE_EXPERTS, ts), 0)
    oh1 = jnp.where(ex == e1, 1.0, 0.0)
    oh2 = jnp.where(ex == e2, 1.0, 0.0)
    p1 = _dot(oh1.astype(BF16), ut_ref[...])
    p2 = _dot(oh2.astype(BF16), ut_ref[...])
    c1 = p1[:, ts - 1:ts]
    c2 = p2[:, ts - 1:ts]
    run = run_ref[...]
    base = run[:, 0:1]
    r1 = jnp.sum(oh1 * (p1 - 1.0 + base), axis=0, keepdims=True)
    r2 = jnp.sum(oh2 * (p2 - 1.0 + base + c1), axis=0, keepdims=True)
    run = run + (c1 + c2)
    run_ref[...] = run
    cnt_ref[...] = run
    ri_ref[...] = jnp.concatenate([e1, e2, r1.astype(jnp.int32), r2.astype(jnp.int32),
                                   jnp.zeros((SUBLANES - 4, ts), jnp.int32)], axis=0)


def _merge(x, h, y_ret, y_hg, y_ssd, wgl, wout, mod_l, norm_w, wr_hi, wr_lo, br, ts):
    B, S, D = x.shape
    ns = S // ts
    R = wr_hi.shape[0]
    tile = pl.BlockSpec((None, ts, D), lambda b, s: (b, s, 0))
    rt = pl.BlockSpec((SUBLANES, ts), lambda b, s: (0, b * ns + s))
    r = jnp.arange(ts)
    ut = (r[:, None] <= r[None, :]).astype(BF16)
    return pl.pallas_call(
        _merge_kernel,
        grid=(B, ns),
        in_specs=[tile, tile, tile, tile, tile, _const_spec((D, 3 * D)), _const_spec((D, D)),
                  pl.BlockSpec((None, 6, D), lambda b, s: (b, 0, 0)), _const_spec((1, D)),
                  _const_spec((R, D)), _const_spec((R, D)), _const_spec((R, 1)), _const_spec((ts, ts))],
        out_specs=[tile, tile, rt, rt, pl.BlockSpec((MOE_EXPERTS, LANES), lambda b, s: (0, 0))],
        out_shape=[jax.ShapeDtypeStruct((B, S, D), F32), jax.ShapeDtypeStruct((B, S, D), F32),
                   jax.ShapeDtypeStruct((SUBLANES, B * S), F32), jax.ShapeDtypeStruct((SUBLANES, B * S), jnp.int32),
                   jax.ShapeDtypeStruct((MOE_EXPERTS, LANES), F32)],
        scratch_shapes=[pltpu.VMEM((MOE_EXPERTS, LANES), F32)],
        compiler_params=_params(2),
        name="merge_router",
    )(x, h, y_ret, y_hg, y_ssd, wgl, wout, mod_l, norm_w.reshape(1, D), wr_hi, wr_lo, br, ut)


def _row_copy(src, src_row, dst, dst_row, sem):
    return pltpu.make_async_copy(src.at[pl.ds(src_row, 1), :], dst.at[pl.ds(dst_row, 1), :], sem)


DISPATCH_TILE = 1024
DMA_UNROLL = 256


def _dispatch_kernel(pend_ref, plen_ref, nact_ref, dest_ref, h2_ref, xs_hbm, zbuf, stage_s, sem, zsem):
    i = pl.program_id(0)
    td = dest_ref.shape[2]
    BLK = MOE_BLOCK
    nb = xs_hbm.shape[0] // BLK

    def zero_fill(wait):
        def go(start, size):
            cp = pltpu.make_async_copy(zbuf.at[pl.ds(0, size), :], xs_hbm.at[pl.ds(start, size), :], zsem)
            if wait:
                cp.wait()
            else:
                cp.start()

        for e in range(MOE_EXPERTS):
            ln = plen_ref[e]
            end = pend_ref[e]
            size = BLK // 2
            while size >= SUBLANES:
                @pl.when((ln & size) != 0)
                def _(size=size, ln=ln, end=end):
                    go(pl.multiple_of(end - (ln & ~(size - 1)), size), size)
                size //= 2
            for j in range(SUBLANES - 1):
                @pl.when(j < (ln & (SUBLANES - 1)))
                def _(j=j, ln=ln, end=end):
                    go(end - ln + j, 1)
        for j in range(MOE_EXPERTS):
            @pl.when(nact_ref[0] + j < nb)
            def _(j=j):
                go(pl.multiple_of((nact_ref[0] + j) * BLK, BLK), BLK)

    @pl.when(i == 0)
    def _():
        zbuf[...] = jnp.zeros_like(zbuf)
        zero_fill(False)

    slot = i % 2
    stage = stage_s.at[slot]
    stage[...] = h2_ref[...]

    def issue(j, c):
        base = pl.multiple_of(j * DMA_UNROLL, DMA_UNROLL)
        src = stage.at[pl.ds(base, DMA_UNROLL), :]
        for u in range(DMA_UNROLL):
            for k in range(2):
                _row_copy(src, u, xs_hbm, dest_ref[0, k, base + u], sem.at[slot]).start(priority=k)
        return c

    lax.fori_loop(0, td // DMA_UNROLL, issue, 0)

    def wait_tile(sl):
        for k in range(2):
            pltpu.make_async_copy(stage_s.at[sl], xs_hbm.at[pl.ds(0, td), :], sem.at[sl]).wait()

    @pl.when(i > 0)
    def _():
        wait_tile(1 - slot)

    @pl.when(i == pl.num_programs(0) - 1)
    def _():
        wait_tile(slot)

    @pl.when(i == 0)
    def _():
        zero_fill(True)


def _dispatch(h2, dest, pad_end, pad_len, nactive, nrows):
    T, D = h2.shape
    td = min(DISPATCH_TILE, T)
    nt = T // td
    dest3 = dest.reshape(2, nt, td).transpose(1, 0, 2)
    grid_spec = pltpu.PrefetchScalarGridSpec(
        num_scalar_prefetch=3,
        grid=(nt,),
        in_specs=[pl.BlockSpec((1, 2, td), lambda i, *_: (i, 0, 0), memory_space=pltpu.SMEM),
                  pl.BlockSpec((td, D), lambda i, *_: (i, 0))],
        out_specs=pl.BlockSpec(memory_space=pl.ANY),
        scratch_shapes=[pltpu.VMEM((MOE_BLOCK, D), F32), pltpu.VMEM((2, td, D), F32),
                        pltpu.SemaphoreType.DMA((2,)), pltpu.SemaphoreType.DMA(())])
    return pl.pallas_call(
        _dispatch_kernel,
        grid_spec=grid_spec,
        out_shape=jax.ShapeDtypeStruct((nrows, D), F32),
        compiler_params=_params(1),
        name="dispatch",
    )(pad_end, pad_len, nactive, dest3, h2)


def _expert_kernel(be_ref, na_ref, x_ref, w1_ref, w3_ref, w2_ref, y_ref, w1_s, w3_s, w2_s):
    i = pl.program_id(0)

    @pl.when((i == 0) | (be_ref[i] != be_ref[jnp.maximum(i - 1, 0)]))
    def _():
        w1_s[...] = w1_ref[...].astype(BF16)
        w3_s[...] = w3_ref[...].astype(BF16)
        w2_s[...] = w2_ref[...].astype(BF16)

    @pl.when(i < na_ref[0])
    def _():
        x = x_ref[...].astype(BF16)
        a = _dot(x, w1_s[...])
        b = _dot(x, w3_s[...])
        y_ref[...] = _dot((_silu(a) * b).astype(BF16), w2_s[...])

    @pl.when(i >= na_ref[0])
    def _():
        y_ref[...] = jnp.zeros_like(y_ref)


def _experts(xs, block_e, nactive, w1, w3, w2, layer):
    nrows, D = xs.shape
    BLK = MOE_BLOCK
    nb = nrows // BLK
    ff = w1.shape[3]
    wsel = lambda i, be, na: (layer, be[i], 0, 0)
    grid_spec = pltpu.PrefetchScalarGridSpec(
        num_scalar_prefetch=2,
        grid=(nb,),
        in_specs=[pl.BlockSpec((BLK, D), lambda i, be, na: (jnp.minimum(i, na[0] - 1), 0)),
                  pl.BlockSpec((None, None, D, ff), wsel), pl.BlockSpec((None, None, D, ff), wsel),
                  pl.BlockSpec((None, None, ff, D), wsel)],
        out_specs=pl.BlockSpec((BLK, D), lambda i, be, na: (i, 0)),
        scratch_shapes=[pltpu.VMEM((D, ff), BF16), pltpu.VMEM((D, ff), BF16), pltpu.VMEM((ff, D), BF16)])
    return pl.pallas_call(
        _expert_kernel,
        grid_spec=grid_spec,
        out_shape=jax.ShapeDtypeStruct((nrows, D), F32),
        compiler_params=_params(1),
        name="experts",
    )(block_e, nactive, xs, w1, w3, w2)


def _dispatch_plan(route_i, counts_f, T):
    BLK = MOE_BLOCK
    nb = -(-(2 * T) // BLK) + MOE_EXPERTS
    counts = counts_f[:, 0].astype(jnp.int32)
    padded = ((counts + BLK - 1) // BLK) * BLK
    pad_ends = jnp.cumsum(padded)
    pad_starts = pad_ends - padded
    e = route_i[0:2]
    rank = route_i[2:4]
    sel = e[:, :, None] == jnp.arange(MOE_EXPERTS, dtype=jnp.int32)[None, None, :]
    dest = rank + jnp.sum(jnp.where(sel, pad_starts[None, None, :], 0), axis=-1)
    nactive = (pad_ends[-1] // BLK).astype(jnp.int32)
    blk_id = jnp.minimum(jnp.arange(nb, dtype=jnp.int32), nactive - 1)
    block_e = jnp.sum((blk_id[:, None] * BLK >= pad_ends[None, :]).astype(jnp.int32), axis=1)
    block_e = jnp.minimum(block_e, MOE_EXPERTS - 1)
    return dest, block_e, pad_ends, padded - counts, nactive.reshape(1), nb * BLK


COMBINE_TILE = 256


def _combine_kernel(final, dcur_ref, dnxt_ref, x_ref, rw_ref, mod_ref, nw_ref, modn_ref, ys_hbm, *rest):
    if final:
        o_ref, gbuf, sem = rest
    else:
        x2_ref, h_ref, gbuf, sem = rest
    i = pl.program_id(0)
    n = pl.num_programs(0)
    ts = x_ref.shape[0]
    slot = i % 2

    def issue(dref, sl):
        for r in range(ts):
            for k in range(2):
                _row_copy(ys_hbm, dref[0, k, r], gbuf.at[sl, k], r, sem.at[sl]).start(priority=k)

    @pl.when(i == 0)
    def _():
        issue(dcur_ref, 0)

    @pl.when(i + 1 < n)
    def _():
        issue(dnxt_ref, 1 - slot)

    for k in range(2):
        pltpu.make_async_copy(ys_hbm.at[pl.ds(0, ts), :], gbuf.at[slot, k], sem.at[slot]).wait()
    y = rw_ref[:, 0:1] * gbuf[slot, 0] + rw_ref[:, 1:2] * gbuf[slot, 1]
    x2 = x_ref[...] + mod_ref[5:6, :] * y
    if final:
        o_ref[...] = (x2 * lax.rsqrt(jnp.mean(x2 * x2, axis=-1, keepdims=True) + EPS) * nw_ref[...]).astype(o_ref.dtype)
    else:
        x2_ref[...] = x2
        h_ref[...] = _rms_mod(x2, nw_ref[...], modn_ref[1:2, :], modn_ref[0:1, :]).astype(h_ref.dtype)


def _combine(x1, ys, dest, rw, mod_l, norm_w, mod_next, final, out_dtype):
    B, S, D = x1.shape
    T = B * S
    ts = min(COMBINE_TILE, S)
    nsb = S // ts
    nt = T // ts
    dest3 = dest.reshape(2, nt, ts).transpose(1, 0, 2)
    tile = pl.BlockSpec((ts, D), lambda i: (i, 0))
    modspec = pl.BlockSpec((None, 6, D), lambda i: (i // nsb, 0, 0))
    if final:
        out_specs = tile
        out_shape = jax.ShapeDtypeStruct((T, D), out_dtype)
    else:
        out_specs = [tile, tile]
        out_shape = [jax.ShapeDtypeStruct((T, D), F32), jax.ShapeDtypeStruct((T, D), BF16)]
    out = pl.pallas_call(
        functools.partial(_combine_kernel, final),
        grid=(nt,),
        in_specs=[pl.BlockSpec((1, 2, ts), lambda i: (i, 0, 0), memory_space=pltpu.SMEM),
                  pl.BlockSpec((1, 2, ts), lambda i: (jnp.minimum(i + 1, nt - 1), 0, 0), memory_space=pltpu.SMEM),
                  tile, pl.BlockSpec((ts, SUBLANES), lambda i: (i, 0)), modspec,
                  pl.BlockSpec((1, D), lambda i: (0, 0)), modspec,
                  pl.BlockSpec(memory_space=pl.ANY)],
        out_specs=out_specs,
        out_shape=out_shape,
        scratch_shapes=[pltpu.VMEM((2, 2, ts, D), F32), pltpu.SemaphoreType.DMA((2,))],
        compiler_params=_params(1),
        name="combine_final" if final else "combine_prenorm",
    )(dest3, dest3, x1.reshape(T, D), rw, mod_l, norm_w.reshape(1, D), mod_next, ys)
    if final:
        return out.reshape(B, S, D)
    return out[0].reshape(B, S, D), out[1].reshape(B, S, D)


def _seq_tile(S, want):
    ts = min(want, S)
    assert S % ts == 0 and ts % CHUNK == 0
    return ts


def kernel(x, c, positions, w_ada, b_ada, norm_mix_w, w_in, ssd_conv_w, ssd_conv_b, ssd_dt_bias, ssd_a_log,
           ssd_d, ssd_norm_w, hg_lb, hg_norm_w, w_ret_o, w_hg_o, w_ssd_o, w_out, norm_ffn_w, moe_w_group,
           moe_b_group, moe_w_expert, moe_b_expert, moe_w1, moe_w3, moe_w2, final_norm_w):
    B, S, D = x.shape
    L = w_ada.shape[0]
    T = B * S
    ts_el = _seq_tile(S, 512)
    ts_mix = _seq_tile(S, 256)
    nb_mix = 2 if B % 2 == 0 else 1
    nb_ret = 4 if B % 4 == 0 else nb_mix

    mod = _adaln(c, w_ada, b_ada)
    cos, sin = _rope_tables(positions, ts_el)
    lb_soft = jax.nn.softmax(hg_lb.astype(F32), axis=0)
    lb_all = jnp.cumsum(lb_soft, axis=0) - lb_soft[0]

    qk = RET_HEADS * RET_DK
    vw = RET_HEADS * RET_DV
    hgw = HG_HEADS * HG_DK
    inner = SSD_HEADS * SSD_P
    cdim = inner + 2 * SSD_GROUPS * SSD_N
    sizes = (qk, qk, vw, vw, hgw, hgw, hgw, hgw, inner, cdim, SSD_HEADS, 3 * D)
    offs = [0]
    for n in sizes:
        offs.append(offs[-1] + n)

    xcur = x.astype(F32)
    h = _prenorm(xcur, norm_mix_w[0].astype(F32), mod[0], ts_el)
    for l in range(L):
        wl = w_in[l].astype(BF16)
        cols = [wl[:, offs[i]:offs[i + 1]] for i in range(len(sizes))]
        y_ret = _retention(h, cos, sin, cols[0], cols[1], cols[2], cols[3], w_ret_o[l].astype(BF16), ts_mix,
                           nb_ret)
        y_hg = _hgrn2(h, cols[4], cols[5], cols[6], cols[7], w_hg_o[l].astype(BF16), lb_all[l],
                      hg_norm_w[l].astype(F32), ts_mix, nb_mix)
        y_ssd = _ssd(h, cols[8], cols[9], cols[10], w_ssd_o[l].astype(BF16), ssd_conv_w[l], ssd_conv_b[l],
                     ssd_dt_bias[l], ssd_a_log[l], ssd_d[l], ssd_norm_w[l], ts_mix, 1)
        wr = jnp.zeros((SUBLANES + MOE_EXPERTS, D), F32)
        wr = wr.at[0:MOE_GROUPS].set(moe_w_group[l].astype(F32).T).at[SUBLANES:].set(moe_w_expert[l].astype(F32).T)
        br = jnp.zeros((SUBLANES + MOE_EXPERTS, 1), F32)
        br = br.at[0:MOE_GROUPS, 0].set(moe_b_group[l].astype(F32)).at[SUBLANES:, 0].set(moe_b_expert[l].astype(F32))
        wr_hi = wr.astype(BF16)
        wr_lo = (wr - wr_hi.astype(F32)).astype(BF16)
        x1, h2, route_w, route_i, counts = _merge(xcur, h, y_ret, y_hg, y_ssd, cols[11], w_out[l].astype(BF16),
                                                  mod[l], norm_ffn_w[l].astype(F32), wr_hi, wr_lo, br, ts_el)
        dest, block_e, pad_end, pad_len, nactive, nrows = _dispatch_plan(route_i, counts, T)
        xs = _dispatch(h2.reshape(T, D), dest, pad_end, pad_len, nactive, nrows)
        ys = _experts(xs, block_e, nactive, moe_w1, moe_w3, moe_w2, l)
        rw = route_w.T
        if l + 1 < L:
            xcur, h = _combine(x1, ys, dest, rw, mod[l], norm_mix_w[l + 1].astype(F32), mod[l + 1], False, x.dtype)
        else:
            out = _combine(x1, ys, dest, rw, mod[l], final_norm_w.astype(F32), mod[l], True, x.dtype)
    return out
```
